```python
import math
import jax
import jax.numpy as jnp
from jax import lax
import numpy as np

D_MODEL = 1024
BATCH = 8
SEQ = 2048
DEPTH = 4

GRID_W = 64
CTX_LEN = 256
N_MIXERS = 4
GROUP_W = 256
CHUNK = 64
Q_BLOCK = 128
ROPE_BASE = 10000.0
MLA_HEADS = 4
MLA_NOPE = 64
MLA_ROPE = 32
MLA_V = 64
MLA_Q_LORA = 256
MLA_KV_LORA = 128
GLA_HEADS = 4
GLA_DK = 32
GLA_DV = 64
GLA_GATE_RANK = 16
GLA_GATE_NORM = 16.0
SSD_HEADS = 4
SSD_HEADDIM = 64
SSD_GROUPS = 2
SSD_STATE = 128
SSD_CONV = 3
RET_HEADS = 4
RET_DK = 32
RET_DV = 64
PEER_HEADS = 8
PEER_NKEYS = 128
PEER_EXPERTS = PEER_NKEYS * PEER_NKEYS
PEER_DKEY = 256
PEER_TOPK = 16
PEER_TOKEN_BLOCK = 128
DEEPNORM_ALPHA = (2 * DEPTH) ** 0.25
DEEPNORM_BETA = (8 * DEPTH) ** -0.25

SSD_BC = SSD_GROUPS * SSD_STATE
SSD_XBC = SSD_HEADS * SSD_HEADDIM + 2 * SSD_BC
MLA_SIZES = (MLA_Q_LORA, MLA_KV_LORA, MLA_ROPE)
GLA_SIZES = (GLA_HEADS * GLA_DK, GLA_HEADS * GLA_DK, GLA_HEADS * GLA_DV, GLA_HEADS * GLA_DV, GLA_GATE_RANK, GLA_GATE_RANK)
SSD_SIZES = (SSD_HEADS * SSD_HEADDIM, SSD_XBC, SSD_HEADS, SSD_HEADS)
RET_SIZES = (RET_HEADS * RET_DK, RET_HEADS * RET_DK, RET_HEADS * RET_DV, RET_HEADS * RET_DV)
IN_SIZES = MLA_SIZES + GLA_SIZES + SSD_SIZES + RET_SIZES
D_IN = sum(IN_SIZES)

kernel_name = 'hybrid_mla_gla_ssd_retnet_peer_dit'

F32 = jnp.float32


def layer_norm(x, g, b, eps=1e-5):
    xf = x.astype(F32)
    mu = xf.mean(-1, keepdims=True)
    xc = xf - mu
    var = jnp.mean(xc * xc, -1, keepdims=True)
    return (xc * lax.rsqrt(var + eps) * g + b).astype(x.dtype)


def rms_norm(x, g, eps=1e-6):
    xf = x.astype(F32)
    return (xf * lax.rsqrt(jnp.mean(xf * xf, -1, keepdims=True) + eps) * g).astype(x.dtype)


def head_norm(o, centre):
    of = o.astype(F32)
    if centre:
        of = of - of.mean(-1, keepdims=True)
    return (of * lax.rsqrt(jnp.mean(of * of, -1, keepdims=True) + 1e-6)).astype(o.dtype)


def modulate(x, shift, scale):
    return x * (1 + scale) + shift


def split_cols(p, sizes):
    return jnp.split(p, np.cumsum(sizes)[:-1].tolist(), axis=-1)


def partition(pieces):
    a = len(MLA_SIZES)
    b = a + len(GLA_SIZES)
    s = b + len(SSD_SIZES)
    return pieces[:a], pieces[a:b], pieces[b:s], pieces[s:]


def to_heads(t, h):
    b, l, _ = t.shape
    return t.reshape(b, l, h, -1).transpose(0, 2, 1, 3)


def merge_heads(t):
    b, h, l, d = t.shape
    return t.transpose(0, 2, 1, 3).reshape(b, l, h * d)


def rope_tables(pos, dim):
    half = dim // 2
    freqs = ROPE_BASE ** (-jnp.arange(half, dtype=F32) / half)
    ang = pos.astype(F32)[:, None] * freqs
    return jnp.cos(ang), jnp.sin(ang)


def rotate(x, tab):
    cos, sin = tab
    cos = cos.astype(x.dtype)
    sin = sin.astype(x.dtype)
    h = x.shape[-1] // 2
    x1, x2 = x[..., :h], x[..., h:]
    return jnp.concatenate([x1 * cos - x2 * sin, x1 * sin + x2 * cos], -1)


def axial_rope(x, axial):
    h = x.shape[-1] // 2
    return jnp.concatenate([rotate(x[..., :h], axial[0]), rotate(x[..., h:], axial[1])], -1)


def attention(q, k, v):
    s = jnp.einsum('bhqd,bhkd->bhqk', q, k).astype(F32) * (q.shape[-1] ** -0.5)
    return jnp.einsum('bhqk,bhkv->bhqv', jax.nn.softmax(s, axis=-1).astype(v.dtype), v)


def blocked_attention(q, k, v):
    b, h, l, d = q.shape
    nb = l // Q_BLOCK
    qb = q.reshape(b, h, nb, Q_BLOCK, d).transpose(2, 0, 1, 3, 4)
    ob = lax.map(lambda qq: attention(qq, k, v), qb)
    return ob.transpose(1, 2, 0, 3, 4).reshape(b, h, l, -1)


def chunk_scan(q, k, v, lg, s0):
    b, h, l, _ = k.shape
    n = l // CHUNK
    dt = v.dtype
    mask = jnp.tril(jnp.ones((CHUNK, CHUNK), bool))[:, :, None]

    def chunks(t):
        return t.reshape(b, h, n, CHUNK, t.shape[-1]).transpose(2, 0, 1, 3, 4)

    def step(s, inp):
        qc, kc, vc, gc = inp
        cum = jnp.cumsum(gc.astype(F32), axis=2)
        last = cum[:, :, -1:]
        inter = jnp.einsum('bhik,bhkv->bhiv', qc * jnp.exp(cum).astype(dt), s)
        decay = jnp.exp(jnp.where(mask, cum[:, :, :, None] - cum[:, :, None], -jnp.inf)).astype(dt)
        if gc.shape[-1] == 1:
            att = jnp.einsum('bhik,bhjk->bhij', qc, kc) * decay[..., 0]
        else:
            att = jnp.einsum('bhik,bhjk,bhijk->bhij', qc, kc, decay)
        o = inter + jnp.einsum('bhij,bhjv->bhiv', att, vc)
        s = jnp.exp(last[:, :, 0])[..., None].astype(dt) * s + jnp.einsum('bhjk,bhjv->bhkv', kc * jnp.exp(last - cum).astype(dt), vc)
        return s, o

    s_fin, o = lax.scan(step, s0, (chunks(q), chunks(k), chunks(v), chunks(lg)))
    return o.transpose(1, 2, 0, 3, 4).reshape(b, h, l, -1), s_fin


def final_state(k, v, lg):
    cum = jnp.cumsum(lg.astype(F32), axis=2)
    return jnp.einsum('bhjk,bhjv->bhkv', k * jnp.exp(cum[:, :, -1:] - cum).astype(v.dtype), v)


def bidir_scan(lat_dirs, ctx_dirs, need_ctx):
    out_lat, out_ctx = [], []
    for d, (lat, ctxd) in enumerate(zip(lat_dirs, ctx_dirs)):
        flip = (lambda t: jnp.flip(t, axis=2)) if d == 1 else (lambda t: t)
        qc, kc, vc, gc = (None if t is None else flip(t) for t in ctxd)
        if need_ctx:
            s0 = jnp.zeros(kc.shape[:2] + (kc.shape[-1], vc.shape[-1]), vc.dtype)
            oc, state = chunk_scan(qc, kc, vc, gc, s0)
            out_ctx.append(flip(oc))
        else:
            state = final_state(kc, vc, gc)
        ql, kl, vl, gl = (flip(t) for t in lat)
        ol, _ = chunk_scan(ql, kl, vl, gl, state)
        out_lat.append(flip(ol))
    return out_lat[0] + out_lat[1], (out_ctx[0] + out_ctx[1] if need_ctx else None)


def mla_q(c_q, g_q, w_uq, axial):
    q = to_heads(rms_norm(c_q, g_q) @ w_uq, MLA_HEADS)
    if axial is None:
        return q
    return jnp.concatenate([q[..., :MLA_NOPE], axial_rope(q[..., MLA_NOPE:], axial)], -1)


def mla_kv(c_kv, k_rope, g_kv, w_uk, w_uv, axial):
    kv = rms_norm(c_kv, g_kv)
    k_nope = to_heads(kv @ w_uk, MLA_HEADS)
    v = to_heads(kv @ w_uv, MLA_HEADS)
    if axial is not None:
        k_rope = axial_rope(k_rope, axial)
    k_rope = jnp.broadcast_to(k_rope[:, None], k_nope.shape[:3] + (MLA_ROPE,))
    return jnp.concatenate([k_nope, k_rope], -1), v


def mla_mixer(pl, pc, g_q, w_uq, g_kv, w_uk, w_uv, axial, need_ctx):
    cq_l, ckv_l, kr_l = pl
    cq_c, ckv_c, kr_c = pc
    k_c, v_c = mla_kv(ckv_c, kr_c, g_kv, w_uk, w_uv, None)
    k_l, v_l = mla_kv(ckv_l, kr_l, g_kv, w_uk, w_uv, axial)
    q_l = mla_q(cq_l, g_q, w_uq, axial)
    o_l = blocked_attention(q_l, jnp.concatenate([k_c, k_l], 2), jnp.concatenate([v_c, v_l], 2))
    o_c = merge_heads(attention(mla_q(cq_c, g_q, w_uq, None), k_c, v_c)) if need_ctx else None
    return merge_heads(o_l), o_c


def gla_prep(p, w_gf, b_gf, w_gb, b_gb, with_q):
    q, k, v, r, lr_f, lr_b = p
    q = (to_heads(q, GLA_HEADS) * GLA_DK ** -0.5) if with_q else None
    k = to_heads(k, GLA_HEADS)
    v = to_heads(v, GLA_HEADS)

    def log_gate(lr, w, b):
        return to_heads(jax.nn.log_sigmoid((lr @ w + b).astype(F32)) / GLA_GATE_NORM, GLA_HEADS)

    return (q, k, v, log_gate(lr_f, w_gf, b_gf)), (q, k, v, log_gate(lr_b, w_gb, b_gb)), r


def gla_mixer(pl, pc, w_gf, b_gf, w_gb, b_gb, g_out, need_ctx):
    lat_f, lat_b, r_l = gla_prep(pl, w_gf, b_gf, w_gb, b_gb, True)
    ctx_f, ctx_b, r_c = gla_prep(pc, w_gf, b_gf, w_gb, b_gb, need_ctx)
    o_l, o_c = bidir_scan((lat_f, lat_b), (ctx_f, ctx_b), need_ctx)

    def out(o, r):
        return merge_heads(head_norm(o, False)) * g_out * jax.nn.silu(r)

    return out(o_l, r_l), (out(o_c, r_c) if need_ctx else None)


def depthwise_conv(x, w, b):
    pad = w.shape[0] // 2
    y = lax.conv_general_dilated(x, w[:, None, :], (1,), [(pad, pad)],
                                 dimension_numbers=('NWC', 'WIO', 'NWC'), feature_group_count=x.shape[-1])
    return y + b


def ssd_prep(p, conv_w, conv_b, dtb_f, dtb_b, alog_f, alog_b, with_q):
    z, xbc, dt_f, dt_b = p
    xbc = jax.nn.silu(depthwise_conv(xbc, conv_w, conv_b))
    xs, bm, cm = split_cols(xbc, (SSD_HEADS * SSD_HEADDIM, SSD_BC, SSD_BC))
    rep = SSD_HEADS // SSD_GROUPS
    xh = to_heads(xs, SSD_HEADS)
    bh = jnp.repeat(to_heads(bm, SSD_GROUPS), rep, axis=1)
    ch = jnp.repeat(to_heads(cm, SSD_GROUPS), rep, axis=1) if with_q else None

    def direction(dt_raw, dt_bias, a_log):
        dt = jax.nn.softplus(dt_raw.astype(F32) + dt_bias).transpose(0, 2, 1)[..., None]
        la = dt * -jnp.exp(a_log.astype(F32))[:, None, None]
        return (ch, bh * dt.astype(bh.dtype), xh, la)

    return direction(dt_f, dtb_f, alog_f), direction(dt_b, dtb_b, alog_b), xh, z


def ssd_mixer(pl, pc, conv_w, conv_b, dtb_f, dtb_b, alog_f, alog_b, d_skip, g_out, need_ctx):
    lat_f, lat_b, x_l, z_l = ssd_prep(pl, conv_w, conv_b, dtb_f, dtb_b, alog_f, alog_b, True)
    ctx_f, ctx_b, x_c, z_c = ssd_prep(pc, conv_w, conv_b, dtb_f, dtb_b, alog_f, alog_b, need_ctx)
    o_l, o_c = bidir_scan((lat_f, lat_b), (ctx_f, ctx_b), need_ctx)

    def out(o, xh, z):
        y = merge_heads(o + d_skip[:, None, None] * xh)
        return rms_norm(y * jax.nn.silu(z), g_out)

    return out(o_l, x_l, z_l), (out(o_c, x_c, z_c) if need_ctx else None)


def ret_log_decay():
    return jnp.log1p(-jnp.exp2(-5.0 - jnp.arange(RET_HEADS, dtype=F32)))


def ret_prep(p, rope, with_q):
    q, k, v, g = p
    q = to_heads(q, RET_HEADS) if with_q else None
    k = to_heads(k, RET_HEADS) * RET_DK ** -0.5
    if rope is not None:
        q = rotate(q, rope)
        k = rotate(k, rope)
    v = to_heads(v, RET_HEADS)
    la = jnp.broadcast_to(ret_log_decay()[:, None, None], (k.shape[0], RET_HEADS, k.shape[2], 1))
    d = (q, k, v, la)
    return (d, d), g


def ret_mixer(pl, pc, rope, need_ctx):
    lat, g_l = ret_prep(pl, rope, True)
    ctx_d, g_c = ret_prep(pc, None, need_ctx)
    o_l, o_c = bidir_scan(lat, ctx_d, need_ctx)

    def out(o, g):
        return merge_heads(head_norm(o, True)) * jax.nn.silu(g)

    return out(o_l, g_l), (out(o_c, g_c) if need_ctx else None)


def peer(x, w_q, sub_keys, u_tab, v_tab):
    b, l, d = x.shape
    t = x.reshape(-1, d)
    nb = t.shape[0] // PEER_TOKEN_BLOCK
    half = PEER_DKEY // 2

    def block(xb):
        q = (xb @ w_q).reshape(xb.shape[0], PEER_HEADS, PEER_DKEY)
        s1 = jnp.einsum('thd,kd->thk', q[..., :half], sub_keys[0]).astype(F32)
        s2 = jnp.einsum('thd,kd->thk', q[..., half:], sub_keys[1]).astype(F32)
        v1, i1 = lax.top_k(s1, PEER_TOPK)
        v2, i2 = lax.top_k(s2, PEER_TOPK)
        cand = (v1[..., :, None] + v2[..., None, :]).reshape(xb.shape[0], PEER_HEADS, PEER_TOPK * PEER_TOPK)
        vs, ci = lax.top_k(cand, PEER_TOPK)
        e = (jnp.take_along_axis(i1, ci // PEER_TOPK, -1) * PEER_NKEYS
             + jnp.take_along_axis(i2, ci % PEER_TOPK, -1))
        gate = jax.nn.softmax(vs, axis=-1).astype(xb.dtype)
        u = jnp.take(u_tab, e, axis=0)
        vv = jnp.take(v_tab, e, axis=0)
        act = jax.nn.gelu(jnp.einsum('td,thkd->thk', xb, u), approximate=False) * gate
        return jnp.einsum('thk,thkd->td', act, vv)

    out = lax.map(block, t.reshape(nb, PEER_TOKEN_BLOCK, d))
    return out.reshape(b, l, d)


def setup_inputs(seed: int = 0) -> dict:
    key = jax.random.key(seed)
    ks = iter(jax.random.split(key, 40))

    def nrm(shape, scale):
        return jax.random.normal(next(ks), shape, F32) * scale

    def gain(shape):
        return 1.0 + nrm(shape, 0.02)

    def dt_bias():
        dt = jnp.exp(jax.random.uniform(next(ks), (DEPTH, SSD_HEADS), F32) * (math.log(0.1) - math.log(0.001)) + math.log(0.001))
        return dt + jnp.log(-jnp.expm1(-dt))

    def a_log():
        return jnp.log(jax.random.uniform(next(ks), (DEPTH, SSD_HEADS), F32, 1.0, 16.0))

    L = DEPTH
    D = D_MODEL
    D_MIX = N_MIXERS * GROUP_W
    return {
        'x': nrm((BATCH, SEQ, D), 1.0),
        'c': nrm((BATCH, D), 1.0),
        'ctx': nrm((BATCH, CTX_LEN, D), 1.0),
        'c_ctx': nrm((D,), 1.0),
        'w_ada': nrm((L, D, 6 * D), D ** -0.5),
        'b_ada': nrm((L, 6 * D), 0.02),
        'w_in': nrm((L, D, D_IN), D ** -0.5),
        'mla_g_q': gain((L, MLA_Q_LORA)),
        'mla_w_uq': nrm((L, MLA_Q_LORA, MLA_HEADS * (MLA_NOPE + MLA_ROPE)), MLA_Q_LORA ** -0.5),
        'mla_g_kv': gain((L, MLA_KV_LORA)),
        'mla_w_uk': nrm((L, MLA_KV_LORA, MLA_HEADS * MLA_NOPE), MLA_KV_LORA ** -0.5),
        'mla_w_uv': nrm((L, MLA_KV_LORA, MLA_HEADS * MLA_V), MLA_KV_LORA ** -0.5),
        'gla_w_gf': nrm((L, GLA_GATE_RANK, GLA_HEADS * GLA_DK), GLA_GATE_RANK ** -0.5),
        'gla_b_gf': nrm((L, GLA_HEADS * GLA_DK), 0.02),
        'gla_w_gb': nrm((L, GLA_GATE_RANK, GLA_HEADS * GLA_DK), GLA_GATE_RANK ** -0.5),
        'gla_b_gb': nrm((L, GLA_HEADS * GLA_DK), 0.02),
        'gla_g': gain((L, GLA_HEADS * GLA_DV)),
        'ssd_conv_w': nrm((L, SSD_CONV, SSD_XBC), SSD_CONV ** -0.5),
        'ssd_conv_b': nrm((L, SSD_XBC), 0.02),
        'ssd_dt_bias_f': dt_bias(),
        'ssd_dt_bias_b': dt_bias(),
        'ssd_a_log_f': a_log(),
        'ssd_a_log_b': a_log(),
        'ssd_d': gain((L, SSD_HEADS)),
        'ssd_g': gain((L, SSD_HEADS * SSD_HEADDIM)),
        'w_out': nrm((L, D_MIX, D), DEEPNORM_BETA * D_MIX ** -0.5),
        'ln1_g': gain((L, D)),
        'ln1_b': nrm((L, D), 0.02),
        'peer_w_q': nrm((L, D, PEER_HEADS * PEER_DKEY), D ** -0.5),
        'peer_keys': nrm((L, 2, PEER_NKEYS, PEER_DKEY // 2), (PEER_DKEY // 2) ** -0.5),
        'peer_u': nrm((L, PEER_EXPERTS, D), D ** -0.5),
        'peer_v': nrm((L, PEER_EXPERTS, D), DEEPNORM_BETA * PEER_HEADS ** -0.5),
        'ln2_g': gain((L, D)),
        'ln2_b': nrm((L, D), 0.02),
    }


def reference(x, c, ctx, c_ctx, w_ada, b_ada, w_in, mla_g_q, mla_w_uq, mla_g_kv, mla_w_uk, mla_w_uv,
              gla_w_gf, gla_b_gf, gla_w_gb, gla_b_gb, gla_g, ssd_conv_w, ssd_conv_b, ssd_dt_bias_f,
              ssd_dt_bias_b, ssd_a_log_f, ssd_a_log_b, ssd_d, ssd_g, w_out, ln1_g, ln1_b, peer_w_q,
              peer_keys, peer_u, peer_v, ln2_g, ln2_b):
    seq = x.shape[1]
    rows = seq // GRID_W
    row = jnp.repeat(jnp.arange(rows), GRID_W)
    col = jnp.tile(jnp.arange(GRID_W), rows)
    axial = (rope_tables(row, MLA_ROPE // 2), rope_tables(col, MLA_ROPE // 2))
    ret_rope = rope_tables(jnp.arange(seq), RET_DK)
    s_lat = jax.nn.silu(c)
    s_ctx = jax.nn.silu(c_ctx)
    xc = ctx
    for i in range(DEPTH):
        need_ctx = i < DEPTH - 1
        mod = [t[:, None, :] for t in jnp.split(s_lat @ w_ada[i] + b_ada[i], 6, axis=-1)]
        mod_c = jnp.split(s_ctx @ w_ada[i] + b_ada[i], 6, axis=-1)
        p_mla, p_gla, p_ssd, p_ret = partition(split_cols(modulate(x, mod[0], mod[1]) @ w_in[i], IN_SIZES))
        c_mla, c_gla, c_ssd, c_ret = partition(split_cols(modulate(xc, mod_c[0], mod_c[1]) @ w_in[i], IN_SIZES))
        outs = (
            mla_mixer(p_mla, c_mla, mla_g_q[i], mla_w_uq[i], mla_g_kv[i], mla_w_uk[i], mla_w_uv[i], axial, need_ctx),
            gla_mixer(p_gla, c_gla, gla_w_gf[i], gla_b_gf[i], gla_w_gb[i], gla_b_gb[i], gla_g[i], need_ctx),
            ssd_mixer(p_ssd, c_ssd, ssd_conv_w[i], ssd_conv_b[i], ssd_dt_bias_f[i], ssd_dt_bias_b[i],
                      ssd_a_log_f[i], ssd_a_log_b[i], ssd_d[i], ssd_g[i], need_ctx),
            ret_mixer(p_ret, c_ret, ret_rope, need_ctx),
        )
        y = jnp.concatenate([o[0] for o in outs], -1) @ w_out[i]
        x = layer_norm(DEEPNORM_ALPHA * x + mod[2] * y, ln1_g[i], ln1_b[i])
        f = peer(modulate(x, mod[3], mod[4]), peer_w_q[i], peer_keys[i], peer_u[i], peer_v[i])
        x = layer_norm(DEEPNORM_ALPHA * x + mod[5] * f, ln2_g[i], ln2_b[i])
        if need_ctx:
            yc = jnp.concatenate([o[1] for o in outs], -1) @ w_out[i]
            xc = layer_norm(DEEPNORM_ALPHA * xc + mod_c[2] * yc, ln1_g[i], ln1_b[i])
            fc = peer(modulate(xc, mod_c[3], mod_c[4]), peer_w_q[i], peer_keys[i], peer_u[i], peer_v[i])
            xc = layer_norm(DEEPNORM_ALPHA * xc + mod_c[5] * fc, ln2_g[i], ln2_b[i])
    return x
```

```python
import functools
import math

import jax
import jax.numpy as jnp
import numpy as np
from jax import lax
from jax.experimental import pallas as pl
from jax.experimental.pallas import tpu as pltpu

D_MODEL = 1024
DEPTH = 4
GRID_W = 64
CHUNK = 64
Q_BLOCK = 128
ROPE_BASE = 10000.0
MLA_HEADS, MLA_NOPE, MLA_ROPE, MLA_V, MLA_Q_LORA, MLA_KV_LORA = 4, 64, 32, 64, 256, 128
GLA_HEADS, GLA_DK, GLA_DV, GLA_GATE_RANK, GLA_GATE_NORM = 4, 32, 64, 16, 16.0
SSD_HEADS, SSD_HEADDIM, SSD_GROUPS, SSD_STATE = 4, 64, 2, 128
RET_HEADS, RET_DK, RET_DV = 4, 32, 64
PEER_HEADS, PEER_NKEYS, PEER_DKEY, PEER_TOPK, PEER_TOKEN_BLOCK = 8, 128, 256, 16, 128
DEEPNORM_ALPHA = (2 * DEPTH) ** 0.25

SSD_BC = SSD_GROUPS * SSD_STATE
SSD_XBC = SSD_HEADS * SSD_HEADDIM + 2 * SSD_BC
MLA_SIZES = (MLA_Q_LORA, MLA_KV_LORA, MLA_ROPE)
GLA_SIZES = (GLA_HEADS * GLA_DK, GLA_HEADS * GLA_DK, GLA_HEADS * GLA_DV, GLA_HEADS * GLA_DV, GLA_GATE_RANK, GLA_GATE_RANK)
SSD_SIZES = (SSD_HEADS * SSD_HEADDIM, SSD_XBC, SSD_HEADS, SSD_HEADS)
RET_SIZES = (RET_HEADS * RET_DK, RET_HEADS * RET_DK, RET_HEADS * RET_DV, RET_HEADS * RET_DV)
IN_SIZES = MLA_SIZES + GLA_SIZES + SSD_SIZES + RET_SIZES

F32 = jnp.float32
BF16 = jnp.bfloat16
LANE = 128


def _mm_body(a_ref, b_ref, o_ref):
    o_ref[...] = jnp.dot(a_ref[...].astype(BF16), b_ref[...].astype(BF16), preferred_element_type=F32)


def _mm(a, b, tm=512, tn=512):
    lead = a.shape[:-1]
    k = a.shape[-1]
    n = b.shape[-1]
    a2 = a.reshape(-1, k)
    m = a2.shape[0]
    tm = min(tm, m)
    n_pad = -(-n // tn) * tn if n > tn else -(-n // LANE) * LANE
    tn = min(tn, n_pad)
    if n_pad != n:
        b = jnp.pad(b, ((0, 0), (0, n_pad - n)))
    out = pl.pallas_call(
        _mm_body,
        grid=(m // tm, n_pad // tn),
        in_specs=[pl.BlockSpec((tm, k), lambda i, j: (i, 0)), pl.BlockSpec((k, tn), lambda i, j: (0, j))],
        out_specs=pl.BlockSpec((tm, tn), lambda i, j: (i, j)),
        out_shape=jax.ShapeDtypeStruct((m, n_pad), F32),
    )(a2, b)
    return out[:, :n].reshape(lead + (n,))


def layer_norm(x, g, b, eps=1e-5):
    mu = x.mean(-1, keepdims=True)
    xc = x - mu
    var = jnp.mean(xc * xc, -1, keepdims=True)
    return xc * lax.rsqrt(var + eps) * g + b


def rms_norm(x, g, eps=1e-6):
    return x * lax.rsqrt(jnp.mean(x * x, -1, keepdims=True) + eps) * g


def head_norm(o, centre):
    if centre:
        o = o - o.mean(-1, keepdims=True)
    return o * lax.rsqrt(jnp.mean(o * o, -1, keepdims=True) + 1e-6)


def modulate(x, shift, scale):
    return x * (1 + scale) + shift


def split_cols(p, sizes):
    return jnp.split(p, np.cumsum(sizes)[:-1].tolist(), axis=-1)


def partition(pieces):
    a = len(MLA_SIZES)
    b = a + len(GLA_SIZES)
    s = b + len(SSD_SIZES)
    return pieces[:a], pieces[a:b], pieces[b:s], pieces[s:]


def to_heads(t, h):
    b, l, _ = t.shape
    return t.reshape(b, l, h, -1).transpose(0, 2, 1, 3)


def merge_heads(t):
    b, h, l, d = t.shape
    return t.transpose(0, 2, 1, 3).reshape(b, l, h * d)


def rope_tables(pos, dim):
    half = dim // 2
    freqs = ROPE_BASE ** (-jnp.arange(half, dtype=F32) / half)
    ang = pos.astype(F32)[:, None] * freqs
    return jnp.cos(ang), jnp.sin(ang)


def rotate(x, tab):
    cos, sin = tab
    h = x.shape[-1] // 2
    x1, x2 = x[..., :h], x[..., h:]
    return jnp.concatenate([x1 * cos - x2 * sin, x1 * sin + x2 * cos], -1)


def axial_rope(x, axial):
    h = x.shape[-1] // 2
    return jnp.concatenate([rotate(x[..., :h], axial[0]), rotate(x[..., h:], axial[1])], -1)


def attention(q, k, v):
    s = jnp.einsum('bhqd,bhkd->bhqk', q, k) * (q.shape[-1] ** -0.5)
    return jnp.einsum('bhqk,bhkv->bhqv', jax.nn.softmax(s, axis=-1), v)


def blocked_attention(q, k, v):
    b, h, l, d = q.shape
    nb = l // Q_BLOCK
    qb = q.reshape(b, h, nb, Q_BLOCK, d).transpose(2, 0, 1, 3, 4)
    ob = lax.map(lambda qq: attention(qq, k, v), qb)
    return ob.transpose(1, 2, 0, 3, 4).reshape(b, h, l, -1)


def chunk_scan(q, k, v, lg, s0):
    b, h, l, _ = k.shape
    n = l // CHUNK
    mask = jnp.tril(jnp.ones((CHUNK, CHUNK), bool))[:, :, None]

    def chunks(t):
        return t.reshape(b, h, n, CHUNK, t.shape[-1]).transpose(2, 0, 1, 3, 4)

    def step(s, inp):
        qc, kc, vc, gc = inp
        cum = jnp.cumsum(gc, axis=2)
        last = cum[:, :, -1:]
        inter = jnp.einsum('bhik,bhkv->bhiv', qc * jnp.exp(cum), s)
        decay = jnp.exp(jnp.where(mask, cum[:, :, :, None] - cum[:, :, None], -jnp.inf))
        if gc.shape[-1] == 1:
            att = jnp.einsum('bhik,bhjk->bhij', qc, kc) * decay[..., 0]
        else:
            att = jnp.einsum('bhik,bhjk,bhijk->bhij', qc, kc, decay)
        o = inter + jnp.einsum('bhij,bhjv->bhiv', att, vc)
        s = jnp.exp(last[:, :, 0])[..., None] * s + jnp.einsum('bhjk,bhjv->bhkv', kc * jnp.exp(last - cum), vc)
        return s, o

    s_fin, o = lax.scan(step, s0, (chunks(q), chunks(k), chunks(v), chunks(lg)))
    return o.transpose(1, 2, 0, 3, 4).reshape(b, h, l, -1), s_fin


def final_state(k, v, lg):
    cum = jnp.cumsum(lg, axis=2)
    return jnp.einsum('bhjk,bhjv->bhkv', k * jnp.exp(cum[:, :, -1:] - cum), v)


def bidir_scan(lat_dirs, ctx_dirs, need_ctx):
    out_lat, out_ctx = [], []
    for d, (lat, ctxd) in enumerate(zip(lat_dirs, ctx_dirs)):
        flip = (lambda t: jnp.flip(t, axis=2)) if d == 1 else (lambda t: t)
        qc, kc, vc, gc = (None if t is None else flip(t) for t in ctxd)
        if need_ctx:
            s0 = jnp.zeros(kc.shape[:2] + (kc.shape[-1], vc.shape[-1]), vc.dtype)
            oc, state = chunk_scan(qc, kc, vc, gc, s0)
            out_ctx.append(flip(oc))
        else:
            state = final_state(kc, vc, gc)
        ql, kl, vl, gl = (flip(t) for t in lat)
        ol, _ = chunk_scan(ql, kl, vl, gl, state)
        out_lat.append(flip(ol))
    return out_lat[0] + out_lat[1], (out_ctx[0] + out_ctx[1] if need_ctx else None)


def mla_q(c_q, g_q, w_uq, axial):
    q = to_heads(rms_norm(c_q, g_q) @ w_uq, MLA_HEADS)
    if axial is None:
        return q
    return jnp.concatenate([q[..., :MLA_NOPE], axial_rope(q[..., MLA_NOPE:], axial)], -1)


def mla_kv(c_kv, k_rope, g_kv, w_uk, w_uv, axial):
    kv = rms_norm(c_kv, g_kv)
    k_nope = to_heads(kv @ w_uk, MLA_HEADS)
    v = to_heads(kv @ w_uv, MLA_HEADS)
    if axial is not None:
        k_rope = axial_rope(k_rope, axial)
    k_rope = jnp.broadcast_to(k_rope[:, None], k_nope.shape[:3] + (MLA_ROPE,))
    return jnp.concatenate([k_nope, k_rope], -1), v


def mla_mixer(pl_, pc, g_q, w_uq, g_kv, w_uk, w_uv, axial, need_ctx):
    cq_l, ckv_l, kr_l = pl_
    cq_c, ckv_c, kr_c = pc
    k_c, v_c = mla_kv(ckv_c, kr_c, g_kv, w_uk, w_uv, None)
    k_l, v_l = mla_kv(ckv_l, kr_l, g_kv, w_uk, w_uv, axial)
    q_l = mla_q(cq_l, g_q, w_uq, axial)
    o_l = blocked_attention(q_l, jnp.concatenate([k_c, k_l], 2), jnp.concatenate([v_c, v_l], 2))
    o_c = merge_heads(attention(mla_q(cq_c, g_q, w_uq, None), k_c, v_c)) if need_ctx else None
    return merge_heads(o_l), o_c


def gla_prep(p, w_gf, b_gf, w_gb, b_gb, with_q):
    q, k, v, r, lr_f, lr_b = p
    q = (to_heads(q, GLA_HEADS) * GLA_DK ** -0.5) if with_q else None
    k = to_heads(k, GLA_HEADS)
    v = to_heads(v, GLA_HEADS)

    def log_gate(lr, w, b):
        return to_heads(jax.nn.log_sigmoid(lr @ w + b) / GLA_GATE_NORM, GLA_HEADS)

    return (q, k, v, log_gate(lr_f, w_gf, b_gf)), (q, k, v, log_gate(lr_b, w_gb, b_gb)), r


def gla_mixer(pl_, pc, w_gf, b_gf, w_gb, b_gb, g_out, need_ctx):
    lat_f, lat_b, r_l = gla_prep(pl_, w_gf, b_gf, w_gb, b_gb, True)
    ctx_f, ctx_b, r_c = gla_prep(pc, w_gf, b_gf, w_gb, b_gb, need_ctx)
    o_l, o_c = bidir_scan((lat_f, lat_b), (ctx_f, ctx_b), need_ctx)

    def out(o, r):
        return merge_heads(head_norm(o, False)) * g_out * jax.nn.silu(r)

    return out(o_l, r_l), (out(o_c, r_c) if need_ctx else None)


def depthwise_conv(x, w, b):
    pad = w.shape[0] // 2
    y = lax.conv_general_dilated(x, w[:, None, :], (1,), [(pad, pad)],
                                 dimension_numbers=('NWC', 'WIO', 'NWC'), feature_group_count=x.shape[-1])
    return y + b


def ssd_prep(p, conv_w, conv_b, dtb_f, dtb_b, alog_f, alog_b, with_q):
    z, xbc, dt_f, dt_b = p
    xbc = jax.nn.silu(depthwise_conv(xbc, conv_w, conv_b))
    xs, bm, cm = split_cols(xbc, (SSD_HEADS * SSD_HEADDIM, SSD_BC, SSD_BC))
    rep = SSD_HEADS // SSD_GROUPS
    xh = to_heads(xs, SSD_HEADS)
    bh = jnp.repeat(to_heads(bm, SSD_GROUPS), rep, axis=1)
    ch = jnp.repeat(to_heads(cm, SSD_GROUPS), rep, axis=1) if with_q else None

    def direction(dt_raw, dt_bias, a_log):
        dt = jax.nn.softplus(dt_raw + dt_bias).transpose(0, 2, 1)[..., None]
        la = dt * -jnp.exp(a_log)[:, None, None]
        return (ch, bh * dt, xh, la)

    return direction(dt_f, dtb_f, alog_f), direction(dt_b, dtb_b, alog_b), xh, z


def ssd_mixer(pl_, pc, conv_w, conv_b, dtb_f, dtb_b, alog_f, alog_b, d_skip, g_out, need_ctx):
    lat_f, lat_b, x_l, z_l = ssd_prep(pl_, conv_w, conv_b, dtb_f, dtb_b, alog_f, alog_b, True)
    ctx_f, ctx_b, x_c, z_c = ssd_prep(pc, conv_w, conv_b, dtb_f, dtb_b, alog_f, alog_b, need_ctx)
    o_l, o_c = bidir_scan((lat_f, lat_b), (ctx_f, ctx_b), need_ctx)

    def out(o, xh, z):
        y = merge_heads(o + d_skip[:, None, None] * xh)
        return rms_norm(y * jax.nn.silu(z), g_out)

    return out(o_l, x_l, z_l), (out(o_c, x_c, z_c) if need_ctx else None)


def ret_log_decay():
    return jnp.log1p(-jnp.exp2(-5.0 - jnp.arange(RET_HEADS, dtype=F32)))


def ret_prep(p, rope, with_q):
    q, k, v, g = p
    q = to_heads(q, RET_HEADS) if with_q else None
    k = to_heads(k, RET_HEADS) * RET_DK ** -0.5
    if rope is not None:
        q = rotate(q, rope)
        k = rotate(k, rope)
    v = to_heads(v, RET_HEADS)
    la = jnp.broadcast_to(ret_log_decay()[:, None, None], (k.shape[0], RET_HEADS, k.shape[2], 1))
    d = (q, k, v, la)
    return (d, d), g


def ret_mixer(pl_, pc, rope, need_ctx):
    lat, g_l = ret_prep(pl_, rope, True)
    ctx_d, g_c = ret_prep(pc, None, need_ctx)
    o_l, o_c = bidir_scan(lat, ctx_d, need_ctx)

    def out(o, g):
        return merge_heads(head_norm(o, True)) * jax.nn.silu(g)

    return out(o_l, g_l), (out(o_c, g_c) if need_ctx else None)


def peer(x, w_q, sub_keys, u_tab, v_tab):
    b, l, d = x.shape
    t = x.reshape(-1, d)
    nb = t.shape[0] // PEER_TOKEN_BLOCK
    half = PEER_DKEY // 2

    def block(xb):
        q = _mm(xb, w_q).reshape(xb.shape[0], PEER_HEADS, PEER_DKEY)
        s1 = jnp.einsum('thd,kd->thk', q[..., :half], sub_keys[0])
        s2 = jnp.einsum('thd,kd->thk', q[..., half:], sub_keys[1])
        v1, i1 = lax.top_k(s1, PEER_TOPK)
        v2, i2 = lax.top_k(s2, PEER_TOPK)
        cand = (v1[..., :, None] + v2[..., None, :]).reshape(xb.shape[0], PEER_HEADS, PEER_TOPK * PEER_TOPK)
        vs, ci = lax.top_k(cand, PEER_TOPK)
        e = (jnp.take_along_axis(i1, ci // PEER_TOPK, -1) * PEER_NKEYS
             + jnp.take_along_axis(i2, ci % PEER_TOPK, -1))
        gate = jax.nn.softmax(vs, axis=-1)
        u = jnp.take(u_tab, e, axis=0)
        vv = jnp.take(v_tab, e, axis=0)
        act = jax.nn.gelu(jnp.einsum('td,thkd->thk', xb, u), approximate=False) * gate
        return jnp.einsum('thk,thkd->td', act, vv)

    out = lax.map(block, t.reshape(nb, PEER_TOKEN_BLOCK, d))
    return out.reshape(b, l, d)


def kernel(x, c, ctx, c_ctx, w_ada, b_ada, w_in, mla_g_q, mla_w_uq, mla_g_kv, mla_w_uk, mla_w_uv, gla_w_gf, gla_b_gf, gla_w_gb, gla_b_gb, gla_g, ssd_conv_w, ssd_conv_b, ssd_dt_bias_f, ssd_dt_bias_b, ssd_a_log_f, ssd_a_log_b, ssd_d, ssd_g, w_out, ln1_g, ln1_b, peer_w_q, peer_keys, peer_u, peer_v, ln2_g, ln2_b):
    seq = x.shape[1]
    rows = seq // GRID_W
    row = jnp.repeat(jnp.arange(rows), GRID_W)
    col = jnp.tile(jnp.arange(GRID_W), rows)
    axial = (rope_tables(row, MLA_ROPE // 2), rope_tables(col, MLA_ROPE // 2))
    ret_rope = rope_tables(jnp.arange(seq), RET_DK)
    s_lat = jax.nn.silu(c)
    s_ctx = jax.nn.silu(c_ctx)
    xc = ctx
    for i in range(DEPTH):
        need_ctx = i < DEPTH - 1
        mod = [t[:, None, :] for t in jnp.split(s_lat @ w_ada[i] + b_ada[i], 6, axis=-1)]
        mod_c = jnp.split(s_ctx @ w_ada[i] + b_ada[i], 6, axis=-1)
        p_mla, p_gla, p_ssd, p_ret = partition(split_cols(_mm(modulate(x, mod[0], mod[1]), w_in[i]), IN_SIZES))
        c_mla, c_gla, c_ssd, c_ret = partition(split_cols(_mm(modulate(xc, mod_c[0], mod_c[1]), w_in[i]), IN_SIZES))
        outs = (
            mla_mixer(p_mla, c_mla, mla_g_q[i], mla_w_uq[i], mla_g_kv[i], mla_w_uk[i], mla_w_uv[i], axial, need_ctx),
            gla_mixer(p_gla, c_gla, gla_w_gf[i], gla_b_gf[i], gla_w_gb[i], gla_b_gb[i], gla_g[i], need_ctx),
            ssd_mixer(p_ssd, c_ssd, ssd_conv_w[i], ssd_conv_b[i], ssd_dt_bias_f[i], ssd_dt_bias_b[i],
                      ssd_a_log_f[i], ssd_a_log_b[i], ssd_d[i], ssd_g[i], need_ctx),
            ret_mixer(p_ret, c_ret, ret_rope, need_ctx),
        )
        y = _mm(jnp.concatenate([o[0] for o in outs], -1), w_out[i])
        x = layer_norm(DEEPNORM_ALPHA * x + mod[2] * y, ln1_g[i], ln1_b[i])
        f = peer(modulate(x, mod[3], mod[4]), peer_w_q[i], peer_keys[i], peer_u[i], peer_v[i])
        x = layer_norm(DEEPNORM_ALPHA * x + mod[5] * f, ln2_g[i], ln2_b[i])
        if need_ctx:
            yc = _mm(jnp.concatenate([o[1] for o in outs], -1), w_out[i])
            xc = layer_norm(DEEPNORM_ALPHA * xc + mod_c[2] * yc, ln1_g[i], ln1_b[i])
            fc = peer(modulate(xc, mod_c[3], mod_c[4]), peer_w_q[i], peer_keys[i], peer_u[i], peer_v[i])
            xc = layer_norm(DEEPNORM_ALPHA * xc + mod_c[5] * fc, ln2_g[i], ln2_b[i])
    return x
```

```python
import functools
import math

import jax
import jax.numpy as jnp
import numpy as np
from jax import lax
from jax.experimental import pallas as pl
from jax.experimental.pallas import tpu as pltpu

D_MODEL = 1024
DEPTH = 4
GRID_W = 64
CHUNK = 64
Q_BLOCK = 128
ROPE_BASE = 10000.0
MLA_HEADS, MLA_NOPE, MLA_ROPE, MLA_V, MLA_Q_LORA, MLA_KV_LORA = 4, 64, 32, 64, 256, 128
GLA_HEADS, GLA_DK, GLA_DV, GLA_GATE_RANK, GLA_GATE_NORM = 4, 32, 64, 16, 16.0
SSD_HEADS, SSD_HEADDIM, SSD_GROUPS, SSD_STATE = 4, 64, 2, 128
RET_HEADS, RET_DK, RET_DV = 4, 32, 64
PEER_HEADS, PEER_NKEYS, PEER_DKEY, PEER_TOPK, PEER_TOKEN_BLOCK = 8, 128, 256, 16, 128
DEEPNORM_ALPHA = (2 * DEPTH) ** 0.25

SSD_BC = SSD_GROUPS * SSD_STATE
SSD_XBC = SSD_HEADS * SSD_HEADDIM + 2 * SSD_BC
MLA_SIZES = (MLA_Q_LORA, MLA_KV_LORA, MLA_ROPE)
GLA_SIZES = (GLA_HEADS * GLA_DK, GLA_HEADS * GLA_DK, GLA_HEADS * GLA_DV, GLA_HEADS * GLA_DV, GLA_GATE_RANK, GLA_GATE_RANK)
SSD_SIZES = (SSD_HEADS * SSD_HEADDIM, SSD_XBC, SSD_HEADS, SSD_HEADS)
RET_SIZES = (RET_HEADS * RET_DK, RET_HEADS * RET_DK, RET_HEADS * RET_DV, RET_HEADS * RET_DV)
IN_SIZES = MLA_SIZES + GLA_SIZES + SSD_SIZES + RET_SIZES

F32 = jnp.float32
BF16 = jnp.bfloat16
LANE = 128


def _mm_body(a_ref, b_ref, o_ref):
    o_ref[...] = jnp.dot(a_ref[...].astype(BF16), b_ref[...].astype(BF16), preferred_element_type=F32)


def _mm(a, b, tm=512, tn=512):
    lead = a.shape[:-1]
    k = a.shape[-1]
    n = b.shape[-1]
    a2 = a.reshape(-1, k)
    m = a2.shape[0]
    tm = min(tm, m)
    n_pad = -(-n // tn) * tn if n > tn else -(-n // LANE) * LANE
    tn = min(tn, n_pad)
    if n_pad != n:
        b = jnp.pad(b, ((0, 0), (0, n_pad - n)))
    out = pl.pallas_call(
        _mm_body,
        grid=(m // tm, n_pad // tn),
        in_specs=[pl.BlockSpec((tm, k), lambda i, j: (i, 0)), pl.BlockSpec((k, tn), lambda i, j: (0, j))],
        out_specs=pl.BlockSpec((tm, tn), lambda i, j: (i, j)),
        out_shape=jax.ShapeDtypeStruct((m, n_pad), F32),
    )(a2, b)
    return out[:, :n].reshape(lead + (n,))


def layer_norm(x, g, b, eps=1e-5):
    mu = x.mean(-1, keepdims=True)
    xc = x - mu
    var = jnp.mean(xc * xc, -1, keepdims=True)
    return xc * lax.rsqrt(var + eps) * g + b


def rms_norm(x, g, eps=1e-6):
    return x * lax.rsqrt(jnp.mean(x * x, -1, keepdims=True) + eps) * g


def head_norm(o, centre):
    if centre:
        o = o - o.mean(-1, keepdims=True)
    return o * lax.rsqrt(jnp.mean(o * o, -1, keepdims=True) + 1e-6)


def modulate(x, shift, scale):
    return x * (1 + scale) + shift


def split_cols(p, sizes):
    return jnp.split(p, np.cumsum(sizes)[:-1].tolist(), axis=-1)


def partition(pieces):
    a = len(MLA_SIZES)
    b = a + len(GLA_SIZES)
    s = b + len(SSD_SIZES)
    return pieces[:a], pieces[a:b], pieces[b:s], pieces[s:]


def to_heads(t, h):
    b, l, _ = t.shape
    return t.reshape(b, l, h, -1).transpose(0, 2, 1, 3)


def merge_heads(t):
    b, h, l, d = t.shape
    return t.transpose(0, 2, 1, 3).reshape(b, l, h * d)


def rope_tables(pos, dim):
    half = dim // 2
    freqs = ROPE_BASE ** (-jnp.arange(half, dtype=F32) / half)
    ang = pos.astype(F32)[:, None] * freqs
    return jnp.cos(ang), jnp.sin(ang)


def rotate(x, tab):
    cos, sin = tab
    h = x.shape[-1] // 2
    x1, x2 = x[..., :h], x[..., h:]
    return jnp.concatenate([x1 * cos - x2 * sin, x1 * sin + x2 * cos], -1)


def axial_rope(x, axial):
    h = x.shape[-1] // 2
    return jnp.concatenate([rotate(x[..., :h], axial[0]), rotate(x[..., h:], axial[1])], -1)


def attention(q, k, v):
    s = jnp.einsum('bhqd,bhkd->bhqk', q, k) * (q.shape[-1] ** -0.5)
    return jnp.einsum('bhqk,bhkv->bhqv', jax.nn.softmax(s, axis=-1), v)


def blocked_attention(q, k, v):
    b, h, l, d = q.shape
    nb = l // Q_BLOCK
    qb = q.reshape(b, h, nb, Q_BLOCK, d).transpose(2, 0, 1, 3, 4)
    ob = lax.map(lambda qq: attention(qq, k, v), qb)
    return ob.transpose(1, 2, 0, 3, 4).reshape(b, h, l, -1)


def chunk_scan(q, k, v, lg, s0):
    b, h, l, _ = k.shape
    n = l // CHUNK
    mask = jnp.tril(jnp.ones((CHUNK, CHUNK), bool))[:, :, None]

    def chunks(t):
        return t.reshape(b, h, n, CHUNK, t.shape[-1]).transpose(2, 0, 1, 3, 4)

    def step(s, inp):
        qc, kc, vc, gc = inp
        cum = jnp.cumsum(gc, axis=2)
        last = cum[:, :, -1:]
        inter = jnp.einsum('bhik,bhkv->bhiv', qc * jnp.exp(cum), s)
        decay = jnp.exp(jnp.where(mask, cum[:, :, :, None] - cum[:, :, None], -jnp.inf))
        if gc.shape[-1] == 1:
            att = jnp.einsum('bhik,bhjk->bhij', qc, kc) * decay[..., 0]
        else:
            att = jnp.einsum('bhik,bhjk,bhijk->bhij', qc, kc, decay)
        o = inter + jnp.einsum('bhij,bhjv->bhiv', att, vc)
        s = jnp.exp(last[:, :, 0])[..., None] * s + jnp.einsum('bhjk,bhjv->bhkv', kc * jnp.exp(last - cum), vc)
        return s, o

    s_fin, o = lax.scan(step, s0, (chunks(q), chunks(k), chunks(v), chunks(lg)))
    return o.transpose(1, 2, 0, 3, 4).reshape(b, h, l, -1), s_fin


def final_state(k, v, lg):
    cum = jnp.cumsum(lg, axis=2)
    return jnp.einsum('bhjk,bhjv->bhkv', k * jnp.exp(cum[:, :, -1:] - cum), v)


def bidir_scan(lat_dirs, ctx_dirs, need_ctx):
    out_lat, out_ctx = [], []
    for d, (lat, ctxd) in enumerate(zip(lat_dirs, ctx_dirs)):
        flip = (lambda t: jnp.flip(t, axis=2)) if d == 1 else (lambda t: t)
        qc, kc, vc, gc = (None if t is None else flip(t) for t in ctxd)
        if need_ctx:
            s0 = jnp.zeros(kc.shape[:2] + (kc.shape[-1], vc.shape[-1]), vc.dtype)
            oc, state = chunk_scan(qc, kc, vc, gc, s0)
            out_ctx.append(flip(oc))
        else:
            state = final_state(kc, vc, gc)
        ql, kl, vl, gl = (flip(t) for t in lat)
        ol, _ = chunk_scan(ql, kl, vl, gl, state)
        out_lat.append(flip(ol))
    return out_lat[0] + out_lat[1], (out_ctx[0] + out_ctx[1] if need_ctx else None)


def mla_q(c_q, g_q, w_uq, axial):
    q = to_heads(rms_norm(c_q, g_q) @ w_uq, MLA_HEADS)
    if axial is None:
        return q
    return jnp.concatenate([q[..., :MLA_NOPE], axial_rope(q[..., MLA_NOPE:], axial)], -1)


def mla_kv(c_kv, k_rope, g_kv, w_uk, w_uv, axial):
    kv = rms_norm(c_kv, g_kv)
    k_nope = to_heads(kv @ w_uk, MLA_HEADS)
    v = to_heads(kv @ w_uv, MLA_HEADS)
    if axial is not None:
        k_rope = axial_rope(k_rope, axial)
    k_rope = jnp.broadcast_to(k_rope[:, None], k_nope.shape[:3] + (MLA_ROPE,))
    return jnp.concatenate([k_nope, k_rope], -1), v


def mla_mixer(pl_, pc, g_q, w_uq, g_kv, w_uk, w_uv, axial, need_ctx):
    cq_l, ckv_l, kr_l = pl_
    cq_c, ckv_c, kr_c = pc
    k_c, v_c = mla_kv(ckv_c, kr_c, g_kv, w_uk, w_uv, None)
    k_l, v_l = mla_kv(ckv_l, kr_l, g_kv, w_uk, w_uv, axial)
    q_l = mla_q(cq_l, g_q, w_uq, axial)
    o_l = blocked_attention(q_l, jnp.concatenate([k_c, k_l], 2), jnp.concatenate([v_c, v_l], 2))
    o_c = merge_heads(attention(mla_q(cq_c, g_q, w_uq, None), k_c, v_c)) if need_ctx else None
    return merge_heads(o_l), o_c


def gla_prep(p, w_gf, b_gf, w_gb, b_gb, with_q):
    q, k, v, r, lr_f, lr_b = p
    q = (to_heads(q, GLA_HEADS) * GLA_DK ** -0.5) if with_q else None
    k = to_heads(k, GLA_HEADS)
    v = to_heads(v, GLA_HEADS)

    def log_gate(lr, w, b):
        return to_heads(jax.nn.log_sigmoid(lr @ w + b) / GLA_GATE_NORM, GLA_HEADS)

    return (q, k, v, log_gate(lr_f, w_gf, b_gf)), (q, k, v, log_gate(lr_b, w_gb, b_gb)), r


def gla_mixer(pl_, pc, w_gf, b_gf, w_gb, b_gb, g_out, need_ctx):
    lat_f, lat_b, r_l = gla_prep(pl_, w_gf, b_gf, w_gb, b_gb, True)
    ctx_f, ctx_b, r_c = gla_prep(pc, w_gf, b_gf, w_gb, b_gb, need_ctx)
    o_l, o_c = bidir_scan((lat_f, lat_b), (ctx_f, ctx_b), need_ctx)

    def out(o, r):
        return merge_heads(head_norm(o, False)) * g_out * jax.nn.silu(r)

    return out(o_l, r_l), (out(o_c, r_c) if need_ctx else None)


def depthwise_conv(x, w, b):
    pad = w.shape[0] // 2
    y = lax.conv_general_dilated(x, w[:, None, :], (1,), [(pad, pad)],
                                 dimension_numbers=('NWC', 'WIO', 'NWC'), feature_group_count=x.shape[-1])
    return y + b


def ssd_prep(p, conv_w, conv_b, dtb_f, dtb_b, alog_f, alog_b, with_q):
    z, xbc, dt_f, dt_b = p
    xbc = jax.nn.silu(depthwise_conv(xbc, conv_w, conv_b))
    xs, bm, cm = split_cols(xbc, (SSD_HEADS * SSD_HEADDIM, SSD_BC, SSD_BC))
    rep = SSD_HEADS // SSD_GROUPS
    xh = to_heads(xs, SSD_HEADS)
    bh = jnp.repeat(to_heads(bm, SSD_GROUPS), rep, axis=1)
    ch = jnp.repeat(to_heads(cm, SSD_GROUPS), rep, axis=1) if with_q else None

    def direction(dt_raw, dt_bias, a_log):
        dt = jax.nn.softplus(dt_raw + dt_bias).transpose(0, 2, 1)[..., None]
        la = dt * -jnp.exp(a_log)[:, None, None]
        return (ch, bh * dt, xh, la)

    return direction(dt_f, dtb_f, alog_f), direction(dt_b, dtb_b, alog_b), xh, z


def ssd_mixer(pl_, pc, conv_w, conv_b, dtb_f, dtb_b, alog_f, alog_b, d_skip, g_out, need_ctx):
    lat_f, lat_b, x_l, z_l = ssd_prep(pl_, conv_w, conv_b, dtb_f, dtb_b, alog_f, alog_b, True)
    ctx_f, ctx_b, x_c, z_c = ssd_prep(pc, conv_w, conv_b, dtb_f, dtb_b, alog_f, alog_b, need_ctx)
    o_l, o_c = bidir_scan((lat_f, lat_b), (ctx_f, ctx_b), need_ctx)

    def out(o, xh, z):
        y = merge_heads(o + d_skip[:, None, None] * xh)
        return rms_norm(y * jax.nn.silu(z), g_out)

    return out(o_l, x_l, z_l), (out(o_c, x_c, z_c) if need_ctx else None)


def ret_log_decay():
    return jnp.log1p(-jnp.exp2(-5.0 - jnp.arange(RET_HEADS, dtype=F32)))


def ret_prep(p, rope, with_q):
    q, k, v, g = p
    q = to_heads(q, RET_HEADS) if with_q else None
    k = to_heads(k, RET_HEADS) * RET_DK ** -0.5
    if rope is not None:
        q = rotate(q, rope)
        k = rotate(k, rope)
    v = to_heads(v, RET_HEADS)
    la = jnp.broadcast_to(ret_log_decay()[:, None, None], (k.shape[0], RET_HEADS, k.shape[2], 1))
    d = (q, k, v, la)
    return (d, d), g


def ret_mixer(pl_, pc, rope, need_ctx):
    lat, g_l = ret_prep(pl_, rope, True)
    ctx_d, g_c = ret_prep(pc, None, need_ctx)
    o_l, o_c = bidir_scan(lat, ctx_d, need_ctx)

    def out(o, g):
        return merge_heads(head_norm(o, True)) * jax.nn.silu(g)

    return out(o_l, g_l), (out(o_c, g_c) if need_ctx else None)


PEER_TT = 512
PEER_EB = 512
SUBLANE = 8
_NT = (((1,), (1,)), ((), ()))
NEG_INF = float('-inf')


def _top_values(s, rows_ref):
    prev = None
    for k in range(PEER_TOPK):
        cur = s if prev is None else jnp.where(s < prev, s, NEG_INF)
        prev = jnp.max(cur, axis=0, keepdims=True)
        rows_ref[k:k + 1, :] = prev


def _peer_score_body(x_ref, wqt_ref, k0_ref, k1_ref, s1_ref, e1_ref, s2_ref, e2_ref, tau_ref,
                     qt_scr, v1_scr, v2_scr):
    half = PEER_DKEY // 2
    qt_scr[...] = lax.dot_general(wqt_ref[...], x_ref[...], _NT, preferred_element_type=F32)
    row = lax.broadcasted_iota(jnp.int32, (SUBLANE, LANE), 0)

    def strip(h, tg):
        s1 = s1_ref[h, tg]
        s2 = s2_ref[h, tg]
        _top_values(s1, v1_scr)
        _top_values(s2, v2_scr)
        cand = []
        for b in range(PEER_TOPK):
            n_a = PEER_TOPK // (b + 1)
            v2b = v2_scr[b:b + 1, :]
            for a0 in range(0, n_a, SUBLANE):
                c = v1_scr[a0:a0 + SUBLANE, :] + v2b
                if n_a - a0 < SUBLANE:
                    c = jnp.where(row < n_a - a0, c, NEG_INF)
                cand.append(c)
        prev = None
        cnt = jnp.zeros((1, LANE), F32)
        tau = jnp.full((1, LANE), NEG_INF, F32)
        for _ in range(PEER_TOPK):
            best = None
            for c in cand:
                cur = c if prev is None else jnp.where(c < prev, c, NEG_INF)
                best = cur if best is None else jnp.maximum(best, cur)
            m = jnp.max(best, axis=0, keepdims=True)
            hits = None
            for c in cand:
                eq = jnp.where(c == m, 1.0, 0.0)
                hits = eq if hits is None else hits + eq
            tau = jnp.where(cnt < PEER_TOPK, m, tau)
            cnt = cnt + jnp.sum(hits, axis=0, keepdims=True)
            prev = m
        m1 = v1_scr[0:1, :]
        m2 = v2_scr[0:1, :]
        zs = None
        for c in cand:
            z = jnp.where(c >= tau, jnp.exp(c - (m1 + m2)), 0.0)
            zs = z if zs is None else zs + z
        z_tot = jnp.sum(zs, axis=0, keepdims=True)
        e1_ref[h, tg] = jnp.exp(s1 - m1) / z_tot
        e2_ref[h, tg] = jnp.exp(s2 - m2)
        tau_ref[h, tg] = jnp.broadcast_to(tau, (SUBLANE, LANE))

    def head(h, carry):
        base = pl.multiple_of(h * PEER_DKEY, PEER_DKEY)
        s1 = jnp.dot(k0_ref[...], qt_scr[pl.ds(base, half), :].astype(BF16), preferred_element_type=F32)
        s2 = jnp.dot(k1_ref[...], qt_scr[pl.ds(base + half, half), :].astype(BF16), preferred_element_type=F32)
        for tg in range(x_ref.shape[0] // LANE):
            s1_ref[h, tg] = s1[:, tg * LANE:(tg + 1) * LANE]
            s2_ref[h, tg] = s2[:, tg * LANE:(tg + 1) * LANE]
        for tg in range(x_ref.shape[0] // LANE):
            strip(h, tg)
        return carry

    lax.fori_loop(0, PEER_HEADS, head, 0)


PEER_JQ = 4


def _peer_expert_body(x_ref, u_ref, vt_ref, s1_ref, e1_ref, s2_ref, e2_ref, tau_ref, o_ref,
                      h_scr, g_scr, acc_scr, bs1_scr, be1_scr):
    e = pl.program_id(1)
    tt = x_ref.shape[0]
    n_i = u_ref.shape[0] // PEER_NKEYS
    span = PEER_JQ * SUBLANE

    @pl.when(e == 0)
    def _():
        acc_scr[...] = jnp.zeros_like(acc_scr)

    h_scr[...] = lax.dot_general(u_ref[...], x_ref[...], _NT, preferred_element_type=F32)

    def expert_row(i_loc, carry):
        i_glob = e * n_i + i_loc
        for tg in range(tt // LANE):
            for h in range(PEER_HEADS):
                bs1_scr[tg, h] = jnp.broadcast_to(s1_ref[h, tg, pl.ds(i_glob, 1), :], (SUBLANE, LANE))
                be1_scr[tg, h] = jnp.broadcast_to(e1_ref[h, tg, pl.ds(i_glob, 1), :], (SUBLANE, LANE))
        r_i = pl.multiple_of(i_loc * PEER_NKEYS, PEER_NKEYS)
        for tg in range(tt // LANE):
            lanes = slice(tg * LANE, (tg + 1) * LANE)
            for j0 in range(0, PEER_NKEYS, span):
                g = [None] * PEER_JQ
                for h in range(PEER_HEADS):
                    a = bs1_scr[tg, h]
                    b = be1_scr[tg, h]
                    tau = tau_ref[h, tg]
                    for k in range(PEER_JQ):
                        d = a + s2_ref[h, tg, j0 + k * SUBLANE:j0 + (k + 1) * SUBLANE, :]
                        p = b * e2_ref[h, tg, j0 + k * SUBLANE:j0 + (k + 1) * SUBLANE, :]
                        gh = jnp.where(d >= tau, p, 0.0)
                        g[k] = gh if g[k] is None else g[k] + gh
                for k in range(PEER_JQ):
                    g_scr[pl.ds(r_i + j0 + k * SUBLANE, SUBLANE), lanes] = g[k]
        return carry

    lax.fori_loop(0, n_i, expert_row, 0)
    hv = h_scr[...]
    act = 0.5 * hv * (1.0 + lax.erf(hv * (2.0 ** -0.5))) * g_scr[...]
    acc_scr[...] += jnp.dot(vt_ref[...], act.astype(BF16), preferred_element_type=F32)

    @pl.when(e == pl.num_programs(1) - 1)
    def _():
        o_ref[...] = acc_scr[...].T


def peer_dense(xm, wqt, k0, k1, u_bf, vt_bf, tt=PEER_TT, eb=PEER_EB):
    t, d = xm.shape
    n_e = u_bf.shape[0]
    xb = xm.astype(BF16)
    hk = (PEER_HEADS, t // LANE, PEER_NKEYS, LANE)
    sblk = pl.BlockSpec((PEER_HEADS, tt // LANE, PEER_NKEYS, LANE), lambda i: (0, i, 0, 0))
    tau_shape = (PEER_HEADS, t // LANE, SUBLANE, LANE)
    tau_blk = (PEER_HEADS, tt // LANE, SUBLANE, LANE)
    s1, e1, s2, e2, tau = pl.pallas_call(
        _peer_score_body,
        grid=(t // tt,),
        in_specs=[pl.BlockSpec((tt, d), lambda i: (i, 0)),
                  pl.BlockSpec(wqt.shape, lambda i: (0, 0)),
                  pl.BlockSpec(k0.shape, lambda i: (0, 0)),
                  pl.BlockSpec(k1.shape, lambda i: (0, 0))],
        out_specs=[sblk, sblk, sblk, sblk, pl.BlockSpec(tau_blk, lambda i: (0, i, 0, 0))],
        out_shape=[jax.ShapeDtypeStruct(hk, F32)] * 4 + [jax.ShapeDtypeStruct(tau_shape, F32)],
        scratch_shapes=[pltpu.VMEM((PEER_HEADS * PEER_DKEY, tt), F32),
                        pltpu.VMEM((PEER_TOPK, LANE), F32), pltpu.VMEM((PEER_TOPK, LANE), F32)],
        compiler_params=pltpu.CompilerParams(vmem_limit_bytes=48 * 2 ** 20),
        name='peer_score',
    )(xb, wqt, k0, k1)
    sblk2 = pl.BlockSpec((PEER_HEADS, tt // LANE, PEER_NKEYS, LANE), lambda i, j: (0, i, 0, 0))
    return pl.pallas_call(
        _peer_expert_body,
        grid=(t // tt, n_e // eb),
        in_specs=[pl.BlockSpec((tt, d), lambda i, j: (i, 0)),
                  pl.BlockSpec((eb, d), lambda i, j: (j, 0)),
                  pl.BlockSpec((d, eb), lambda i, j: (0, j)),
                  sblk2, sblk2, sblk2, sblk2,
                  pl.BlockSpec(tau_blk, lambda i, j: (0, i, 0, 0))],
        out_specs=pl.BlockSpec((tt, d), lambda i, j: (i, 0)),
        out_shape=jax.ShapeDtypeStruct((t, d), F32),
        scratch_shapes=[pltpu.VMEM((eb, tt), F32), pltpu.VMEM((eb, tt), F32), pltpu.VMEM((d, tt), F32)]
        + [pltpu.VMEM((tt // LANE, PEER_HEADS, SUBLANE, LANE), F32)] * 2,
        compiler_params=pltpu.CompilerParams(vmem_limit_bytes=48 * 2 ** 20,
                                             dimension_semantics=("arbitrary", "arbitrary")),
        name='peer_expert',
    )(xb, u_bf, vt_bf, s1, e1, s2, e2, tau)


def kernel(x, c, ctx, c_ctx, w_ada, b_ada, w_in, mla_g_q, mla_w_uq, mla_g_kv, mla_w_uk, mla_w_uv, gla_w_gf, gla_b_gf, gla_w_gb, gla_b_gb, gla_g, ssd_conv_w, ssd_conv_b, ssd_dt_bias_f, ssd_dt_bias_b, ssd_a_log_f, ssd_a_log_b, ssd_d, ssd_g, w_out, ln1_g, ln1_b, peer_w_q, peer_keys, peer_u, peer_v, ln2_g, ln2_b):
    seq = x.shape[1]
    rows = seq // GRID_W
    row = jnp.repeat(jnp.arange(rows), GRID_W)
    col = jnp.tile(jnp.arange(GRID_W), rows)
    axial = (rope_tables(row, MLA_ROPE // 2), rope_tables(col, MLA_ROPE // 2))
    ret_rope = rope_tables(jnp.arange(seq), RET_DK)
    s_lat = jax.nn.silu(c)
    s_ctx = jax.nn.silu(c_ctx)
    xc = ctx
    for i in range(DEPTH):
        need_ctx = i < DEPTH - 1
        mod = [t[:, None, :] for t in jnp.split(s_lat @ w_ada[i] + b_ada[i], 6, axis=-1)]
        mod_c = jnp.split(s_ctx @ w_ada[i] + b_ada[i], 6, axis=-1)
        p_mla, p_gla, p_ssd, p_ret = partition(split_cols(_mm(modulate(x, mod[0], mod[1]), w_in[i]), IN_SIZES))
        c_mla, c_gla, c_ssd, c_ret = partition(split_cols(_mm(modulate(xc, mod_c[0], mod_c[1]), w_in[i]), IN_SIZES))
        outs = (
            mla_mixer(p_mla, c_mla, mla_g_q[i], mla_w_uq[i], mla_g_kv[i], mla_w_uk[i], mla_w_uv[i], axial, need_ctx),
            gla_mixer(p_gla, c_gla, gla_w_gf[i], gla_b_gf[i], gla_w_gb[i], gla_b_gb[i], gla_g[i], need_ctx),
            ssd_mixer(p_ssd, c_ssd, ssd_conv_w[i], ssd_conv_b[i], ssd_dt_bias_f[i], ssd_dt_bias_b[i],
                      ssd_a_log_f[i], ssd_a_log_b[i], ssd_d[i], ssd_g[i], need_ctx),
            ret_mixer(p_ret, c_ret, ret_rope, need_ctx),
        )
        y = _mm(jnp.concatenate([o[0] for o in outs], -1), w_out[i])
        x = layer_norm(DEEPNORM_ALPHA * x + mod[2] * y, ln1_g[i], ln1_b[i])
        toks = [modulate(x, mod[3], mod[4]).reshape(-1, D_MODEL)]
        if need_ctx:
            yc = _mm(jnp.concatenate([o[1] for o in outs], -1), w_out[i])
            xc = layer_norm(DEEPNORM_ALPHA * xc + mod_c[2] * yc, ln1_g[i], ln1_b[i])
            toks.append(modulate(xc, mod_c[3], mod_c[4]).reshape(-1, D_MODEL))
        n_lat = toks[0].shape[0]
        f_all = peer_dense(jnp.concatenate(toks, 0), peer_w_q[i].T.astype(BF16), peer_keys[i, 0].astype(BF16),
                           peer_keys[i, 1].astype(BF16), peer_u[i].astype(BF16), peer_v[i].T.astype(BF16))
        x = layer_norm(DEEPNORM_ALPHA * x + mod[5] * f_all[:n_lat].reshape(x.shape), ln2_g[i], ln2_b[i])
        if need_ctx:
            xc = layer_norm(DEEPNORM_ALPHA * xc + mod_c[5] * f_all[n_lat:].reshape(xc.shape), ln2_g[i], ln2_b[i])
    return x
```

```python
import functools
import math

import jax
import jax.numpy as jnp
import numpy as np
from jax import lax
from jax.experimental import pallas as pl
from jax.experimental.pallas import tpu as pltpu

D_MODEL = 1024
DEPTH = 4
GRID_W = 64
CHUNK = 64
Q_BLOCK = 128
ROPE_BASE = 10000.0
MLA_HEADS, MLA_NOPE, MLA_ROPE, MLA_V, MLA_Q_LORA, MLA_KV_LORA = 4, 64, 32, 64, 256, 128
GLA_HEADS, GLA_DK, GLA_DV, GLA_GATE_RANK, GLA_GATE_NORM = 4, 32, 64, 16, 16.0
SSD_HEADS, SSD_HEADDIM, SSD_GROUPS, SSD_STATE = 4, 64, 2, 128
RET_HEADS, RET_DK, RET_DV = 4, 32, 64
PEER_HEADS, PEER_NKEYS, PEER_DKEY, PEER_TOPK, PEER_TOKEN_BLOCK = 8, 128, 256, 16, 128
DEEPNORM_ALPHA = (2 * DEPTH) ** 0.25

SSD_BC = SSD_GROUPS * SSD_STATE
SSD_XBC = SSD_HEADS * SSD_HEADDIM + 2 * SSD_BC
MLA_SIZES = (MLA_Q_LORA, MLA_KV_LORA, MLA_ROPE)
GLA_SIZES = (GLA_HEADS * GLA_DK, GLA_HEADS * GLA_DK, GLA_HEADS * GLA_DV, GLA_HEADS * GLA_DV, GLA_GATE_RANK, GLA_GATE_RANK)
SSD_SIZES = (SSD_HEADS * SSD_HEADDIM, SSD_XBC, SSD_HEADS, SSD_HEADS)
RET_SIZES = (RET_HEADS * RET_DK, RET_HEADS * RET_DK, RET_HEADS * RET_DV, RET_HEADS * RET_DV)
IN_SIZES = MLA_SIZES + GLA_SIZES + SSD_SIZES + RET_SIZES

F32 = jnp.float32
BF16 = jnp.bfloat16
LANE = 128


def _mm_body(a_ref, b_ref, o_ref):
    o_ref[...] = jnp.dot(a_ref[...].astype(BF16), b_ref[...].astype(BF16), preferred_element_type=F32)


def _mm(a, b, tm=512, tn=512):
    lead = a.shape[:-1]
    k = a.shape[-1]
    n = b.shape[-1]
    a2 = a.reshape(-1, k)
    m = a2.shape[0]
    tm = math.gcd(tm, m)
    n_pad = -(-n // tn) * tn if n > tn else -(-n // LANE) * LANE
    tn = min(tn, n_pad)
    if n_pad != n:
        b = jnp.pad(b, ((0, 0), (0, n_pad - n)))
    out = pl.pallas_call(
        _mm_body,
        grid=(m // tm, n_pad // tn),
        in_specs=[pl.BlockSpec((tm, k), lambda i, j: (i, 0)), pl.BlockSpec((k, tn), lambda i, j: (0, j))],
        out_specs=pl.BlockSpec((tm, tn), lambda i, j: (i, j)),
        out_shape=jax.ShapeDtypeStruct((m, n_pad), F32),
    )(a2, b)
    return out[:, :n].reshape(lead + (n,))


def layer_norm(x, g, b, eps=1e-5):
    mu = x.mean(-1, keepdims=True)
    xc = x - mu
    var = jnp.mean(xc * xc, -1, keepdims=True)
    return xc * lax.rsqrt(var + eps) * g + b


def rms_norm(x, g, eps=1e-6):
    return x * lax.rsqrt(jnp.mean(x * x, -1, keepdims=True) + eps) * g


def head_norm(o, centre):
    if centre:
        o = o - o.mean(-1, keepdims=True)
    return o * lax.rsqrt(jnp.mean(o * o, -1, keepdims=True) + 1e-6)


def modulate(x, shift, scale):
    return x * (1 + scale) + shift


def split_cols(p, sizes):
    return jnp.split(p, np.cumsum(sizes)[:-1].tolist(), axis=-1)


def partition(pieces):
    a = len(MLA_SIZES)
    b = a + len(GLA_SIZES)
    s = b + len(SSD_SIZES)
    return pieces[:a], pieces[a:b], pieces[b:s], pieces[s:]


def to_heads(t, h):
    b, l, _ = t.shape
    return t.reshape(b, l, h, -1).transpose(0, 2, 1, 3)


def merge_heads(t):
    b, h, l, d = t.shape
    return t.transpose(0, 2, 1, 3).reshape(b, l, h * d)


def rope_tables(pos, dim):
    half = dim // 2
    freqs = ROPE_BASE ** (-jnp.arange(half, dtype=F32) / half)
    ang = pos.astype(F32)[:, None] * freqs
    return jnp.cos(ang), jnp.sin(ang)


def rotate(x, tab):
    cos, sin = tab
    h = x.shape[-1] // 2
    x1, x2 = x[..., :h], x[..., h:]
    return jnp.concatenate([x1 * cos - x2 * sin, x1 * sin + x2 * cos], -1)


def axial_rope(x, axial):
    h = x.shape[-1] // 2
    return jnp.concatenate([rotate(x[..., :h], axial[0]), rotate(x[..., h:], axial[1])], -1)


def attention(q, k, v):
    s = jnp.einsum('bhqd,bhkd->bhqk', q, k) * (q.shape[-1] ** -0.5)
    return jnp.einsum('bhqk,bhkv->bhqv', jax.nn.softmax(s, axis=-1), v)


def blocked_attention(q, k, v):
    b, h, l, d = q.shape
    nb = l // Q_BLOCK
    qb = q.reshape(b, h, nb, Q_BLOCK, d).transpose(2, 0, 1, 3, 4)
    ob = lax.map(lambda qq: attention(qq, k, v), qb)
    return ob.transpose(1, 2, 0, 3, 4).reshape(b, h, l, -1)


def chunk_scan(q, k, v, lg, s0):
    b, h, l, _ = k.shape
    n = l // CHUNK
    mask = jnp.tril(jnp.ones((CHUNK, CHUNK), bool))[:, :, None]

    def chunks(t):
        return t.reshape(b, h, n, CHUNK, t.shape[-1]).transpose(2, 0, 1, 3, 4)

    def step(s, inp):
        qc, kc, vc, gc = inp
        cum = jnp.cumsum(gc, axis=2)
        last = cum[:, :, -1:]
        inter = jnp.einsum('bhik,bhkv->bhiv', qc * jnp.exp(cum), s)
        decay = jnp.exp(jnp.where(mask, cum[:, :, :, None] - cum[:, :, None], -jnp.inf))
        if gc.shape[-1] == 1:
            att = jnp.einsum('bhik,bhjk->bhij', qc, kc) * decay[..., 0]
        else:
            att = jnp.einsum('bhik,bhjk,bhijk->bhij', qc, kc, decay)
        o = inter + jnp.einsum('bhij,bhjv->bhiv', att, vc)
        s = jnp.exp(last[:, :, 0])[..., None] * s + jnp.einsum('bhjk,bhjv->bhkv', kc * jnp.exp(last - cum), vc)
        return s, o

    s_fin, o = lax.scan(step, s0, (chunks(q), chunks(k), chunks(v), chunks(lg)))
    return o.transpose(1, 2, 0, 3, 4).reshape(b, h, l, -1), s_fin


def final_state(k, v, lg):
    cum = jnp.cumsum(lg, axis=2)
    return jnp.einsum('bhjk,bhjv->bhkv', k * jnp.exp(cum[:, :, -1:] - cum), v)


def bidir_scan(lat_dirs, ctx_dirs, need_ctx):
    out_lat, out_ctx = [], []
    for d, (lat, ctxd) in enumerate(zip(lat_dirs, ctx_dirs)):
        flip = (lambda t: jnp.flip(t, axis=2)) if d == 1 else (lambda t: t)
        qc, kc, vc, gc = (None if t is None else flip(t) for t in ctxd)
        if need_ctx:
            s0 = jnp.zeros(kc.shape[:2] + (kc.shape[-1], vc.shape[-1]), vc.dtype)
            oc, state = chunk_scan(qc, kc, vc, gc, s0)
            out_ctx.append(flip(oc))
        else:
            state = final_state(kc, vc, gc)
        ql, kl, vl, gl = (flip(t) for t in lat)
        ol, _ = chunk_scan(ql, kl, vl, gl, state)
        out_lat.append(flip(ol))
    return out_lat[0] + out_lat[1], (out_ctx[0] + out_ctx[1] if need_ctx else None)


def mla_q(c_q, g_q, w_uq, axial):
    q = to_heads(rms_norm(c_q, g_q) @ w_uq, MLA_HEADS)
    if axial is None:
        return q
    return jnp.concatenate([q[..., :MLA_NOPE], axial_rope(q[..., MLA_NOPE:], axial)], -1)


def mla_kv(c_kv, k_rope, g_kv, w_uk, w_uv, axial):
    kv = rms_norm(c_kv, g_kv)
    k_nope = to_heads(kv @ w_uk, MLA_HEADS)
    v = to_heads(kv @ w_uv, MLA_HEADS)
    if axial is not None:
        k_rope = axial_rope(k_rope, axial)
    k_rope = jnp.broadcast_to(k_rope[:, None], k_nope.shape[:3] + (MLA_ROPE,))
    return jnp.concatenate([k_nope, k_rope], -1), v


def mla_mixer(pl_, pc, g_q, w_uq, g_kv, w_uk, w_uv, axial, need_ctx):
    cq_l, ckv_l, kr_l = pl_
    cq_c, ckv_c, kr_c = pc
    k_c, v_c = mla_kv(ckv_c, kr_c, g_kv, w_uk, w_uv, None)
    k_l, v_l = mla_kv(ckv_l, kr_l, g_kv, w_uk, w_uv, axial)
    q_l = mla_q(cq_l, g_q, w_uq, axial)
    o_l = blocked_attention(q_l, jnp.concatenate([k_c, k_l], 2), jnp.concatenate([v_c, v_l], 2))
    o_c = merge_heads(attention(mla_q(cq_c, g_q, w_uq, None), k_c, v_c)) if need_ctx else None
    return merge_heads(o_l), o_c


def gla_prep(p, w_gf, b_gf, w_gb, b_gb, with_q):
    q, k, v, r, lr_f, lr_b = p
    q = (to_heads(q, GLA_HEADS) * GLA_DK ** -0.5) if with_q else None
    k = to_heads(k, GLA_HEADS)
    v = to_heads(v, GLA_HEADS)

    def log_gate(lr, w, b):
        return to_heads(jax.nn.log_sigmoid(lr @ w + b) / GLA_GATE_NORM, GLA_HEADS)

    return (q, k, v, log_gate(lr_f, w_gf, b_gf)), (q, k, v, log_gate(lr_b, w_gb, b_gb)), r


def gla_mixer(pl_, pc, w_gf, b_gf, w_gb, b_gb, g_out, need_ctx):
    lat_f, lat_b, r_l = gla_prep(pl_, w_gf, b_gf, w_gb, b_gb, True)
    ctx_f, ctx_b, r_c = gla_prep(pc, w_gf, b_gf, w_gb, b_gb, need_ctx)
    o_l, o_c = bidir_scan((lat_f, lat_b), (ctx_f, ctx_b), need_ctx)

    def out(o, r):
        return merge_heads(head_norm(o, False)) * g_out * jax.nn.silu(r)

    return out(o_l, r_l), (out(o_c, r_c) if need_ctx else None)


def depthwise_conv(x, w, b):
    pad = w.shape[0] // 2
    y = lax.conv_general_dilated(x, w[:, None, :], (1,), [(pad, pad)],
                                 dimension_numbers=('NWC', 'WIO', 'NWC'), feature_group_count=x.shape[-1])
    return y + b


def ssd_prep(p, conv_w, conv_b, dtb_f, dtb_b, alog_f, alog_b, with_q):
    z, xbc, dt_f, dt_b = p
    xbc = jax.nn.silu(depthwise_conv(xbc, conv_w, conv_b))
    xs, bm, cm = split_cols(xbc, (SSD_HEADS * SSD_HEADDIM, SSD_BC, SSD_BC))
    rep = SSD_HEADS // SSD_GROUPS
    xh = to_heads(xs, SSD_HEADS)
    bh = jnp.repeat(to_heads(bm, SSD_GROUPS), rep, axis=1)
    ch = jnp.repeat(to_heads(cm, SSD_GROUPS), rep, axis=1) if with_q else None

    def direction(dt_raw, dt_bias, a_log):
        dt = jax.nn.softplus(dt_raw + dt_bias).transpose(0, 2, 1)[..., None]
        la = dt * -jnp.exp(a_log)[:, None, None]
        return (ch, bh * dt, xh, la)

    return direction(dt_f, dtb_f, alog_f), direction(dt_b, dtb_b, alog_b), xh, z


def ssd_mixer(pl_, pc, conv_w, conv_b, dtb_f, dtb_b, alog_f, alog_b, d_skip, g_out, need_ctx):
    lat_f, lat_b, x_l, z_l = ssd_prep(pl_, conv_w, conv_b, dtb_f, dtb_b, alog_f, alog_b, True)
    ctx_f, ctx_b, x_c, z_c = ssd_prep(pc, conv_w, conv_b, dtb_f, dtb_b, alog_f, alog_b, need_ctx)
    o_l, o_c = bidir_scan((lat_f, lat_b), (ctx_f, ctx_b), need_ctx)

    def out(o, xh, z):
        y = merge_heads(o + d_skip[:, None, None] * xh)
        return rms_norm(y * jax.nn.silu(z), g_out)

    return out(o_l, x_l, z_l), (out(o_c, x_c, z_c) if need_ctx else None)


def ret_log_decay():
    return jnp.log1p(-jnp.exp2(-5.0 - jnp.arange(RET_HEADS, dtype=F32)))


def ret_prep(p, rope, with_q):
    q, k, v, g = p
    q = to_heads(q, RET_HEADS) if with_q else None
    k = to_heads(k, RET_HEADS) * RET_DK ** -0.5
    if rope is not None:
        q = rotate(q, rope)
        k = rotate(k, rope)
    v = to_heads(v, RET_HEADS)
    la = jnp.broadcast_to(ret_log_decay()[:, None, None], (k.shape[0], RET_HEADS, k.shape[2], 1))
    d = (q, k, v, la)
    return (d, d), g


def ret_mixer(pl_, pc, rope, need_ctx):
    lat, g_l = ret_prep(pl_, rope, True)
    ctx_d, g_c = ret_prep(pc, None, need_ctx)
    o_l, o_c = bidir_scan(lat, ctx_d, need_ctx)

    def out(o, g):
        return merge_heads(head_norm(o, True)) * jax.nn.silu(g)

    return out(o_l, g_l), (out(o_c, g_c) if need_ctx else None)


PEER_TT = 512
PEER_EB = 512
SUBLANE = 8
_NT = (((1,), (1,)), ((), ()))
NEG_INF = float('-inf')


def _top_values(s, rows_ref):
    prev = None
    for k in range(PEER_TOPK):
        cur = s if prev is None else jnp.where(s < prev, s, NEG_INF)
        prev = jnp.max(cur, axis=0, keepdims=True)
        rows_ref[k:k + 1, :] = prev


def _peer_score_body(x_ref, wqt_ref, k0_ref, k1_ref, s1_ref, e1_ref, s2_ref, e2_ref, tau_ref,
                     qt_scr, v1_scr, v2_scr):
    half = PEER_DKEY // 2
    qt_scr[...] = lax.dot_general(wqt_ref[...], x_ref[...], _NT, preferred_element_type=F32)
    row = lax.broadcasted_iota(jnp.int32, (SUBLANE, LANE), 0)

    def strip(h, tg):
        s1 = s1_ref[h, tg]
        s2 = s2_ref[h, tg]
        _top_values(s1, v1_scr)
        _top_values(s2, v2_scr)
        cand = []
        for b in range(PEER_TOPK):
            n_a = PEER_TOPK // (b + 1)
            v2b = v2_scr[b:b + 1, :]
            for a0 in range(0, n_a, SUBLANE):
                c = v1_scr[a0:a0 + SUBLANE, :] + v2b
                if n_a - a0 < SUBLANE:
                    c = jnp.where(row < n_a - a0, c, NEG_INF)
                cand.append(c)
        prev = None
        cnt = jnp.zeros((1, LANE), F32)
        tau = jnp.full((1, LANE), NEG_INF, F32)
        for _ in range(PEER_TOPK):
            best = None
            for c in cand:
                cur = c if prev is None else jnp.where(c < prev, c, NEG_INF)
                best = cur if best is None else jnp.maximum(best, cur)
            m = jnp.max(best, axis=0, keepdims=True)
            hits = None
            for c in cand:
                eq = jnp.where(c == m, 1.0, 0.0)
                hits = eq if hits is None else hits + eq
            tau = jnp.where(cnt < PEER_TOPK, m, tau)
            cnt = cnt + jnp.sum(hits, axis=0, keepdims=True)
            prev = m
        m1 = v1_scr[0:1, :]
        m2 = v2_scr[0:1, :]
        zs = None
        for c in cand:
            z = jnp.where(c >= tau, jnp.exp(c - (m1 + m2)), 0.0)
            zs = z if zs is None else zs + z
        z_tot = jnp.sum(zs, axis=0, keepdims=True)
        e1_ref[h, tg] = jnp.exp(s1 - m1) / z_tot
        e2_ref[h, tg] = jnp.exp(s2 - m2)
        tau_ref[h, tg] = jnp.broadcast_to(tau, (SUBLANE, LANE))

    def head(h, carry):
        base = pl.multiple_of(h * PEER_DKEY, PEER_DKEY)
        s1 = jnp.dot(k0_ref[...], qt_scr[pl.ds(base, half), :].astype(BF16), preferred_element_type=F32)
        s2 = jnp.dot(k1_ref[...], qt_scr[pl.ds(base + half, half), :].astype(BF16), preferred_element_type=F32)
        for tg in range(x_ref.shape[0] // LANE):
            s1_ref[h, tg] = s1[:, tg * LANE:(tg + 1) * LANE]
            s2_ref[h, tg] = s2[:, tg * LANE:(tg + 1) * LANE]
        for tg in range(x_ref.shape[0] // LANE):
            strip(h, tg)
        return carry

    lax.fori_loop(0, PEER_HEADS, head, 0)


PEER_JQ = 4


def _peer_expert_body(x_ref, u_ref, vt_ref, s1_ref, e1_ref, s2_ref, e2_ref, tau_ref, o_ref,
                      h_scr, g_scr, act_scr, acc_scr, bs1_scr, be1_scr):
    e = pl.program_id(1)
    n_blocks = pl.num_programs(1) - 1
    tt = x_ref.shape[0]
    n_i = u_ref.shape[0] // PEER_NKEYS
    span = PEER_JQ * SUBLANE

    @pl.when(e == 0)
    def _():
        acc_scr[...] = jnp.zeros_like(acc_scr)
        act_scr[...] = jnp.zeros_like(act_scr)

    acc_scr[...] += jnp.dot(vt_ref[...], act_scr[...], preferred_element_type=F32)
    h_scr[...] = lax.dot_general(u_ref[...], x_ref[...], _NT, preferred_element_type=F32)

    def expert_row(i_loc, carry):
        i_glob = jnp.minimum(e, n_blocks - 1) * n_i + i_loc
        r_i = pl.multiple_of(i_loc * PEER_NKEYS, PEER_NKEYS)
        for tg in range(tt // LANE):
            for h in range(PEER_HEADS):
                bs1_scr[i_loc, tg, h] = jnp.broadcast_to(s1_ref[h, tg, pl.ds(i_glob, 1), :], (SUBLANE, LANE))
                be1_scr[i_loc, tg, h] = jnp.broadcast_to(e1_ref[h, tg, pl.ds(i_glob, 1), :], (SUBLANE, LANE))
        for tg in range(tt // LANE):
            lanes = slice(tg * LANE, (tg + 1) * LANE)
            for j0 in range(0, PEER_NKEYS, span):
                g = [None] * PEER_JQ
                for h in range(PEER_HEADS):
                    a = bs1_scr[i_loc, tg, h]
                    b = be1_scr[i_loc, tg, h]
                    tau = tau_ref[h, tg]
                    for k in range(PEER_JQ):
                        d = a + s2_ref[h, tg, j0 + k * SUBLANE:j0 + (k + 1) * SUBLANE, :]
                        p = b * e2_ref[h, tg, j0 + k * SUBLANE:j0 + (k + 1) * SUBLANE, :]
                        gh = jnp.where(d >= tau, p, 0.0)
                        g[k] = gh if g[k] is None else g[k] + gh
                for k in range(PEER_JQ):
                    g_scr[pl.ds(r_i + j0 + k * SUBLANE, SUBLANE), lanes] = g[k]
        return carry

    lax.fori_loop(0, n_i, expert_row, 0)
    hv = h_scr[...]
    act_scr[...] = (0.5 * hv * (1.0 + lax.erf(hv * (2.0 ** -0.5))) * g_scr[...]).astype(BF16)

    @pl.when(e == n_blocks)
    def _():
        o_ref[...] = acc_scr[...].T


def peer_dense(xm, wqt, k0, k1, u_bf, vt_bf, tt=PEER_TT, eb=PEER_EB):
    t, d = xm.shape
    n_blocks = u_bf.shape[0] // eb
    xb = xm.astype(BF16)
    hk = (PEER_HEADS, t // LANE, PEER_NKEYS, LANE)
    sblk = pl.BlockSpec((PEER_HEADS, tt // LANE, PEER_NKEYS, LANE), lambda i: (0, i, 0, 0))
    tau_shape = (PEER_HEADS, t // LANE, SUBLANE, LANE)
    tau_blk = (PEER_HEADS, tt // LANE, SUBLANE, LANE)
    s1, e1, s2, e2, tau = pl.pallas_call(
        _peer_score_body,
        grid=(t // tt,),
        in_specs=[pl.BlockSpec((tt, d), lambda i: (i, 0)),
                  pl.BlockSpec(wqt.shape, lambda i: (0, 0)),
                  pl.BlockSpec(k0.shape, lambda i: (0, 0)),
                  pl.BlockSpec(k1.shape, lambda i: (0, 0))],
        out_specs=[sblk, sblk, sblk, sblk, pl.BlockSpec(tau_blk, lambda i: (0, i, 0, 0))],
        out_shape=[jax.ShapeDtypeStruct(hk, F32)] * 4 + [jax.ShapeDtypeStruct(tau_shape, F32)],
        scratch_shapes=[pltpu.VMEM((PEER_HEADS * PEER_DKEY, tt), F32),
                        pltpu.VMEM((PEER_TOPK, LANE), F32), pltpu.VMEM((PEER_TOPK, LANE), F32)],
        compiler_params=pltpu.CompilerParams(vmem_limit_bytes=48 * 2 ** 20),
        name='peer_score',
    )(xb, wqt, k0, k1)
    sblk2 = pl.BlockSpec((PEER_HEADS, tt // LANE, PEER_NKEYS, LANE), lambda i, j: (0, i, 0, 0))
    return pl.pallas_call(
        _peer_expert_body,
        grid=(t // tt, n_blocks + 1),
        in_specs=[pl.BlockSpec((tt, d), lambda i, j: (i, 0)),
                  pl.BlockSpec((eb, d), lambda i, j: (jnp.minimum(j, n_blocks - 1), 0)),
                  pl.BlockSpec((d, eb), lambda i, j: (0, jnp.maximum(j - 1, 0))),
                  sblk2, sblk2, sblk2, sblk2,
                  pl.BlockSpec(tau_blk, lambda i, j: (0, i, 0, 0))],
        out_specs=pl.BlockSpec((tt, d), lambda i, j: (i, 0)),
        out_shape=jax.ShapeDtypeStruct((t, d), F32),
        scratch_shapes=[pltpu.VMEM((eb, tt), F32), pltpu.VMEM((eb, tt), F32), pltpu.VMEM((eb, tt), BF16),
                        pltpu.VMEM((d, tt), F32)]
        + [pltpu.VMEM((eb // PEER_NKEYS, tt // LANE, PEER_HEADS, SUBLANE, LANE), F32)] * 2,
        compiler_params=pltpu.CompilerParams(vmem_limit_bytes=48 * 2 ** 20,
                                             dimension_semantics=("arbitrary", "arbitrary")),
        name='peer_expert',
    )(xb, u_bf, vt_bf, s1, e1, s2, e2, tau)


SCAN_TL = 256
CTX_LEN = 256
_HI = lax.Precision.HIGHEST
_TN = (((0,), (0,)), ((), ()))


def _scan_body(qf_ref, kf_ref, vf_ref, lgf_ref, ltf_ref, qb_ref, kb_ref, vb_ref, lgb_ref, ltb_ref,
               of_ref, ob_ref, sf_scr, sb_scr, *, chunk, heads, per_dim):
    tl, kd = qf_ref.shape[1], qf_ref.shape[2]
    vd = vf_ref.shape[2]
    dk, dv = kd // heads, vd // heads

    @pl.when(pl.program_id(1) == 0)
    def _():
        sf_scr[...] = jnp.zeros_like(sf_scr)
        sb_scr[...] = jnp.zeros_like(sb_scr)

    ri = lax.broadcasted_iota(jnp.int32, (chunk, chunk), 0)
    ci = lax.broadcasted_iota(jnp.int32, (chunk, chunk), 1)
    khead = lax.broadcasted_iota(jnp.int32, (1, kd), 1) // dk
    vhead = lax.broadcasted_iota(jnp.int32, (1, vd), 1) // dv
    same_head = (lax.broadcasted_iota(jnp.int32, (vd, kd), 0) // dv
                 == lax.broadcasted_iota(jnp.int32, (vd, kd), 1) // dk)

    def per_key(x8):
        out = jnp.zeros((x8.shape[0], kd), F32)
        for h in range(heads):
            out = jnp.where(khead == h, x8[:, h:h + 1], out)
        return out

    def sweep(q_ref, k_ref, v_ref, lg_ref, lt_ref, o_ref, s_scr, fwd):
        keep = (ri >= ci) if fwd else (ri <= ci)
        tri = keep.astype(F32)
        order = range(tl // chunk) if fwd else reversed(range(tl // chunk))
        for c in order:
            rows = slice(c * chunk, (c + 1) * chunk)
            q, k, v = q_ref[0, rows, :], k_ref[0, rows, :], v_ref[0, rows, :]
            cum = jnp.dot(tri, lg_ref[0, rows, :], precision=_HI, preferred_element_type=F32)
            tot = cum[chunk - 1:chunk, :] if fwd else cum[0:1, :]
            if per_dim:
                e_in, e_out, e_tot = jnp.exp(cum), jnp.exp(tot - cum), jnp.exp(tot)
                mid = cum[chunk // 2:chunk // 2 + 1, :]
                q_att = (q * jnp.exp(cum - mid))
                k_att = (k * jnp.exp(mid - cum)).astype(BF16)
            else:
                e_in, e_out, e_tot = per_key(jnp.exp(cum)), per_key(jnp.exp(tot - cum)), per_key(jnp.exp(tot))
                cum_t = lax.dot_general(lt_ref[0, :, rows], tri, _NT, precision=_HI, preferred_element_type=F32)
                q_att, k_att = q, k.astype(BF16)
            s = s_scr[...]
            o = lax.dot_general((q * e_in).astype(BF16), s.astype(BF16), _NT, preferred_element_type=F32)
            for h in range(heads):
                sc = lax.dot_general(jnp.where(khead == h, q_att, 0.0).astype(BF16), k_att, _NT,
                                     preferred_element_type=F32)
                if per_dim:
                    att = jnp.where(keep, sc, 0.0)
                else:
                    att = sc * jnp.exp(jnp.where(keep, cum[:, h:h + 1] - cum_t[h:h + 1, :], NEG_INF))
                o = o + jnp.dot(att.astype(BF16), jnp.where(vhead == h, v, 0.0).astype(BF16),
                                preferred_element_type=F32)
            o_ref[0, rows, :] = o
            kv = lax.dot_general(v.astype(BF16), (k * e_out).astype(BF16), _TN, preferred_element_type=F32)
            s_scr[...] = s * e_tot + jnp.where(same_head, kv, 0.0)

    sweep(qf_ref, kf_ref, vf_ref, lgf_ref, ltf_ref, of_ref, sf_scr, True)
    sweep(qb_ref, kb_ref, vb_ref, lgb_ref, ltb_ref, ob_ref, sb_scr, False)


def scan_bidir(q, k_f, k_b, v, lg_f, lg_b, lt_f, lt_b, *, heads, per_dim, chunk):
    b, l, kd = q.shape
    vd = v.shape[-1]
    n, n_ctx = l // SCAN_TL, CTX_LEN // SCAN_TL

    def bwd_group(s):
        return jnp.where(s < n_ctx, n_ctx - 1 - s, n - 1 - (s - n_ctx))

    def specs(group):
        tok = lambda w: pl.BlockSpec((1, SCAN_TL, w), lambda bi, s: (bi, group(s), 0))
        return [tok(kd), tok(kd), tok(vd), tok(LANE),
                pl.BlockSpec((1, SUBLANE, SCAN_TL), lambda bi, s: (bi, 0, group(s)))]

    fwd_group = lambda s: s
    o_f, o_b = pl.pallas_call(
        functools.partial(_scan_body, chunk=chunk, heads=heads, per_dim=per_dim),
        grid=(b, n),
        in_specs=specs(fwd_group) + specs(bwd_group),
        out_specs=[pl.BlockSpec((1, SCAN_TL, vd), lambda bi, s: (bi, s, 0)),
                   pl.BlockSpec((1, SCAN_TL, vd), lambda bi, s: (bi, bwd_group(s), 0))],
        out_shape=[jax.ShapeDtypeStruct((b, l, vd), F32)] * 2,
        scratch_shapes=[pltpu.VMEM((vd, kd), F32)] * 2,
        compiler_params=pltpu.CompilerParams(dimension_semantics=("arbitrary", "arbitrary")),
        name='scan_bidir',
    )(q, k_f, v, lg_f, lt_f, q, k_b, v, lg_b, lt_b)
    return o_f + o_b


def _ctx_identity(tab):
    cos, sin = tab
    pad = (CTX_LEN, cos.shape[1])
    return jnp.concatenate([jnp.ones(pad, F32), cos], 0), jnp.concatenate([jnp.zeros(pad, F32), sin], 0)


def _pad_lanes(t):
    return jnp.pad(t, ((0, 0), (0, 0), (0, LANE - t.shape[-1])))


def _head_rows(t):
    return jnp.pad(t.transpose(0, 2, 1), ((0, 0), (0, SUBLANE - t.shape[-1]), (0, 0)))


def _split_heads(t, h):
    return t.reshape(t.shape[:-1] + (h, t.shape[-1] // h))


def _group_norm(o, heads, centre):
    o4 = _split_heads(o, heads)
    return head_norm(o4, centre).reshape(o.shape)


def _rotate_heads(t, heads, tab):
    cos, sin = tab
    return rotate(_split_heads(t, heads), (cos[:, None, :], sin[:, None, :])).reshape(t.shape)


def _segment_conv(x, w, b):
    def conv(t):
        z = jnp.zeros_like(t[:, :1])
        return (jnp.concatenate([z, t[:, :-1]], 1) * w[0] + t * w[1] + jnp.concatenate([t[:, 1:], z], 1) * w[2]) + b

    return jnp.concatenate([conv(x[:, :CTX_LEN]), conv(x[:, CTX_LEN:])], 1)


def gla_tokens(p, w_gf, b_gf, w_gb, b_gb, g_out):
    q, k, v, r, lr_f, lr_b = p
    no_rows = jnp.zeros((q.shape[0], SUBLANE, q.shape[1]), F32)

    def log_gate(lr, w, b):
        return jax.nn.log_sigmoid(_mm(lr, w) + b) / GLA_GATE_NORM

    o = scan_bidir(q * GLA_DK ** -0.5, k, k, v, log_gate(lr_f, w_gf, b_gf), log_gate(lr_b, w_gb, b_gb),
                   no_rows, no_rows, heads=GLA_HEADS, per_dim=True, chunk=CHUNK)
    return _group_norm(o, GLA_HEADS, False) * g_out * jax.nn.silu(r)


def ssd_tokens(p, conv_w, conv_b, dtb_f, dtb_b, alog_f, alog_b, d_skip, g_out):
    z, xbc, dt_f, dt_b = p
    xbc = jax.nn.silu(_segment_conv(xbc, conv_w, conv_b))
    xs, bm, cm = split_cols(xbc, (SSD_HEADS * SSD_HEADDIM, SSD_BC, SSD_BC))
    rep = SSD_HEADS // SSD_GROUPS

    def per_head(t):
        return jnp.repeat(_split_heads(t, SSD_GROUPS), rep, axis=2).reshape(t.shape[:-1] + (SSD_HEADS * SSD_STATE,))

    b_all = per_head(bm)

    def direction(dt_raw, dt_bias, a_log):
        dt = jax.nn.softplus(dt_raw + dt_bias)
        return b_all * jnp.repeat(dt, SSD_STATE, axis=-1), dt * -jnp.exp(a_log)

    k_f, la_f = direction(dt_f, dtb_f, alog_f)
    k_b, la_b = direction(dt_b, dtb_b, alog_b)
    o = scan_bidir(per_head(cm), k_f, k_b, xs, _pad_lanes(la_f), _pad_lanes(la_b), _head_rows(la_f), _head_rows(la_b),
                   heads=SSD_HEADS, per_dim=False, chunk=SCAN_TL)
    y = o + jnp.repeat(d_skip, SSD_HEADDIM) * xs
    return rms_norm(y * jax.nn.silu(z), g_out)


def ret_tokens(p, rope):
    q, k, v, g = p
    q = _rotate_heads(q, RET_HEADS, rope)
    k = _rotate_heads(k * RET_DK ** -0.5, RET_HEADS, rope)
    la = jnp.broadcast_to(ret_log_decay(), q.shape[:2] + (RET_HEADS,))
    o = scan_bidir(q, k, k, v, _pad_lanes(la), _pad_lanes(la), _head_rows(la), _head_rows(la),
                   heads=RET_HEADS, per_dim=False, chunk=SCAN_TL)
    return _group_norm(o, RET_HEADS, True) * jax.nn.silu(g)


def mla_tokens(p, g_q, w_uq, g_kv, w_uk, w_uv, axial):
    c_q, c_kv, k_r = p
    kv = rms_norm(c_kv, g_kv)
    k_nope = to_heads(_mm(kv, w_uk), MLA_HEADS)
    v = to_heads(_mm(kv, w_uv), MLA_HEADS)
    k_rope = axial_rope(k_r, axial)
    k = jnp.concatenate([k_nope, jnp.broadcast_to(k_rope[:, None], k_nope.shape[:3] + (MLA_ROPE,))], -1)
    q = to_heads(_mm(rms_norm(c_q, g_q), w_uq), MLA_HEADS)
    q = jnp.concatenate([q[..., :MLA_NOPE], axial_rope(q[..., MLA_NOPE:], axial)], -1)
    return attention_tokens(q.astype(BF16), k.astype(BF16), v.astype(BF16))


ATTN_TQ = 256


def _attn_body(q_ref, k_ref, v_ref, o_ref):
    heads, l, dv = v_ref.shape[1], v_ref.shape[2], v_ref.shape[3]
    col = lax.broadcasted_iota(jnp.int32, (1, l), 1)
    visible = (col < CTX_LEN) | (pl.program_id(1) > 0)
    scale = q_ref.shape[-1] ** -0.5
    for h in range(heads):
        s = lax.dot_general(q_ref[0, h], k_ref[0, h], _NT, preferred_element_type=F32) * scale
        s = jnp.where(visible, s, NEG_INF)
        p = jnp.exp(s - jnp.max(s, axis=1, keepdims=True))
        o = jnp.dot(p.astype(BF16), v_ref[0, h], preferred_element_type=F32)
        o_ref[0, :, h * dv:(h + 1) * dv] = o / jnp.sum(p, axis=1, keepdims=True)


def attention_tokens(q, k, v):
    b, h, l, d = q.shape
    dv = v.shape[-1]
    return pl.pallas_call(
        _attn_body,
        grid=(b, l // ATTN_TQ),
        in_specs=[pl.BlockSpec((1, h, ATTN_TQ, d), lambda bi, qi: (bi, 0, qi, 0)),
                  pl.BlockSpec((1, h, l, d), lambda bi, qi: (bi, 0, 0, 0)),
                  pl.BlockSpec((1, h, l, dv), lambda bi, qi: (bi, 0, 0, 0))],
        out_specs=pl.BlockSpec((1, ATTN_TQ, h * dv), lambda bi, qi: (bi, qi, 0)),
        out_shape=jax.ShapeDtypeStruct((b, l, h * dv), F32),
        compiler_params=pltpu.CompilerParams(vmem_limit_bytes=48 * 2 ** 20),
        name='mla_attention',
    )(q, k, v)


def kernel(x, c, ctx, c_ctx, w_ada, b_ada, w_in, mla_g_q, mla_w_uq, mla_g_kv, mla_w_uk, mla_w_uv, gla_w_gf, gla_b_gf, gla_w_gb, gla_b_gb, gla_g, ssd_conv_w, ssd_conv_b, ssd_dt_bias_f, ssd_dt_bias_b, ssd_a_log_f, ssd_a_log_b, ssd_d, ssd_g, w_out, ln1_g, ln1_b, peer_w_q, peer_keys, peer_u, peer_v, ln2_g, ln2_b):
    seq = x.shape[1]
    rows = seq // GRID_W
    row = jnp.repeat(jnp.arange(rows), GRID_W)
    col = jnp.tile(jnp.arange(GRID_W), rows)
    axial = (_ctx_identity(rope_tables(row, MLA_ROPE // 2)), _ctx_identity(rope_tables(col, MLA_ROPE // 2)))
    ret_rope = _ctx_identity(rope_tables(jnp.arange(seq), RET_DK))
    s_lat = jax.nn.silu(c)
    s_ctx = jax.nn.silu(c_ctx)
    xc = ctx
    for i in range(DEPTH):
        need_ctx = i < DEPTH - 1
        mod = [t[:, None, :] for t in jnp.split(s_lat @ w_ada[i] + b_ada[i], 6, axis=-1)]
        mod_c = jnp.split(s_ctx @ w_ada[i] + b_ada[i], 6, axis=-1)
        xin = jnp.concatenate([modulate(xc, mod_c[0], mod_c[1]), modulate(x, mod[0], mod[1])], 1)
        p_mla, p_gla, p_ssd, p_ret = partition(split_cols(_mm(xin, w_in[i]), IN_SIZES))
        outs = (
            mla_tokens(p_mla, mla_g_q[i], mla_w_uq[i], mla_g_kv[i], mla_w_uk[i], mla_w_uv[i], axial),
            gla_tokens(p_gla, gla_w_gf[i], gla_b_gf[i], gla_w_gb[i], gla_b_gb[i], gla_g[i]),
            ssd_tokens(p_ssd, ssd_conv_w[i], ssd_conv_b[i], ssd_dt_bias_f[i], ssd_dt_bias_b[i],
                       ssd_a_log_f[i], ssd_a_log_b[i], ssd_d[i], ssd_g[i]),
            ret_tokens(p_ret, ret_rope),
        )
        mixed = jnp.concatenate(outs, -1)
        y = _mm(mixed[:, CTX_LEN:], w_out[i])
        x = layer_norm(DEEPNORM_ALPHA * x + mod[2] * y, ln1_g[i], ln1_b[i])
        toks = [modulate(x, mod[3], mod[4]).reshape(-1, D_MODEL)]
        if need_ctx:
            yc = _mm(mixed[:, :CTX_LEN], w_out[i])
            xc = layer_norm(DEEPNORM_ALPHA * xc + mod_c[2] * yc, ln1_g[i], ln1_b[i])
            toks.append(modulate(xc, mod_c[3], mod_c[4]).reshape(-1, D_MODEL))
        n_lat = toks[0].shape[0]
        f_all = peer_dense(jnp.concatenate(toks, 0), peer_w_q[i].T.astype(BF16), peer_keys[i, 0].astype(BF16),
                           peer_keys[i, 1].astype(BF16), peer_u[i].astype(BF16), peer_v[i].T.astype(BF16))
        x = layer_norm(DEEPNORM_ALPHA * x + mod[5] * f_all[:n_lat].reshape(x.shape), ln2_g[i], ln2_b[i])
        if need_ctx:
            xc = layer_norm(DEEPNORM_ALPHA * xc + mod_c[5] * f_all[n_lat:].reshape(xc.shape), ln2_g[i], ln2_b[i])
    return x
```

```python
import functools
import math

import jax
import jax.numpy as jnp
import numpy as np
from jax import lax
from jax.experimental import pallas as pl
from jax.experimental.pallas import tpu as pltpu

D_MODEL = 1024
DEPTH = 4
GRID_W = 64
CHUNK = 64
Q_BLOCK = 128
ROPE_BASE = 10000.0
MLA_HEADS, MLA_NOPE, MLA_ROPE, MLA_V, MLA_Q_LORA, MLA_KV_LORA = 4, 64, 32, 64, 256, 128
GLA_HEADS, GLA_DK, GLA_DV, GLA_GATE_RANK, GLA_GATE_NORM = 4, 32, 64, 16, 16.0
SSD_HEADS, SSD_HEADDIM, SSD_GROUPS, SSD_STATE = 4, 64, 2, 128
RET_HEADS, RET_DK, RET_DV = 4, 32, 64
PEER_HEADS, PEER_NKEYS, PEER_DKEY, PEER_TOPK, PEER_TOKEN_BLOCK = 8, 128, 256, 16, 128
DEEPNORM_ALPHA = (2 * DEPTH) ** 0.25

SSD_BC = SSD_GROUPS * SSD_STATE
SSD_XBC = SSD_HEADS * SSD_HEADDIM + 2 * SSD_BC
MLA_SIZES = (MLA_Q_LORA, MLA_KV_LORA, MLA_ROPE)
GLA_SIZES = (GLA_HEADS * GLA_DK, GLA_HEADS * GLA_DK, GLA_HEADS * GLA_DV, GLA_HEADS * GLA_DV, GLA_GATE_RANK, GLA_GATE_RANK)
SSD_SIZES = (SSD_HEADS * SSD_HEADDIM, SSD_XBC, SSD_HEADS, SSD_HEADS)
RET_SIZES = (RET_HEADS * RET_DK, RET_HEADS * RET_DK, RET_HEADS * RET_DV, RET_HEADS * RET_DV)
IN_SIZES = MLA_SIZES + GLA_SIZES + SSD_SIZES + RET_SIZES

F32 = jnp.float32
BF16 = jnp.bfloat16
LANE = 128


def _mm_body(a_ref, b_ref, o_ref):
    o_ref[...] = jnp.dot(a_ref[...].astype(BF16), b_ref[...].astype(BF16), preferred_element_type=F32)


def _mm(a, b, tm=512, tn=512):
    lead = a.shape[:-1]
    k = a.shape[-1]
    n = b.shape[-1]
    a2 = a.reshape(-1, k)
    m = a2.shape[0]
    tm = math.gcd(tm, m)
    n_pad = -(-n // tn) * tn if n > tn else -(-n // LANE) * LANE
    tn = min(tn, n_pad)
    if n_pad != n:
        b = jnp.pad(b, ((0, 0), (0, n_pad - n)))
    out = pl.pallas_call(
        _mm_body,
        grid=(m // tm, n_pad // tn),
        in_specs=[pl.BlockSpec((tm, k), lambda i, j: (i, 0)), pl.BlockSpec((k, tn), lambda i, j: (0, j))],
        out_specs=pl.BlockSpec((tm, tn), lambda i, j: (i, j)),
        out_shape=jax.ShapeDtypeStruct((m, n_pad), F32),
    )(a2, b)
    return out[:, :n].reshape(lead + (n,))


def layer_norm(x, g, b, eps=1e-5):
    mu = x.mean(-1, keepdims=True)
    xc = x - mu
    var = jnp.mean(xc * xc, -1, keepdims=True)
    return xc * lax.rsqrt(var + eps) * g + b


def rms_norm(x, g, eps=1e-6):
    return x * lax.rsqrt(jnp.mean(x * x, -1, keepdims=True) + eps) * g


def head_norm(o, centre):
    if centre:
        o = o - o.mean(-1, keepdims=True)
    return o * lax.rsqrt(jnp.mean(o * o, -1, keepdims=True) + 1e-6)


def modulate(x, shift, scale):
    return x * (1 + scale) + shift


def split_cols(p, sizes):
    return jnp.split(p, np.cumsum(sizes)[:-1].tolist(), axis=-1)


def partition(pieces):
    a = len(MLA_SIZES)
    b = a + len(GLA_SIZES)
    s = b + len(SSD_SIZES)
    return pieces[:a], pieces[a:b], pieces[b:s], pieces[s:]


def to_heads(t, h):
    b, l, _ = t.shape
    return t.reshape(b, l, h, -1).transpose(0, 2, 1, 3)


def merge_heads(t):
    b, h, l, d = t.shape
    return t.transpose(0, 2, 1, 3).reshape(b, l, h * d)


def rope_tables(pos, dim):
    half = dim // 2
    freqs = ROPE_BASE ** (-jnp.arange(half, dtype=F32) / half)
    ang = pos.astype(F32)[:, None] * freqs
    return jnp.cos(ang), jnp.sin(ang)


def rotate(x, tab):
    cos, sin = tab
    h = x.shape[-1] // 2
    x1, x2 = x[..., :h], x[..., h:]
    return jnp.concatenate([x1 * cos - x2 * sin, x1 * sin + x2 * cos], -1)


def axial_rope(x, axial):
    h = x.shape[-1] // 2
    return jnp.concatenate([rotate(x[..., :h], axial[0]), rotate(x[..., h:], axial[1])], -1)


def attention(q, k, v):
    s = jnp.einsum('bhqd,bhkd->bhqk', q, k) * (q.shape[-1] ** -0.5)
    return jnp.einsum('bhqk,bhkv->bhqv', jax.nn.softmax(s, axis=-1), v)


def blocked_attention(q, k, v):
    b, h, l, d = q.shape
    nb = l // Q_BLOCK
    qb = q.reshape(b, h, nb, Q_BLOCK, d).transpose(2, 0, 1, 3, 4)
    ob = lax.map(lambda qq: attention(qq, k, v), qb)
    return ob.transpose(1, 2, 0, 3, 4).reshape(b, h, l, -1)


def chunk_scan(q, k, v, lg, s0):
    b, h, l, _ = k.shape
    n = l // CHUNK
    mask = jnp.tril(jnp.ones((CHUNK, CHUNK), bool))[:, :, None]

    def chunks(t):
        return t.reshape(b, h, n, CHUNK, t.shape[-1]).transpose(2, 0, 1, 3, 4)

    def step(s, inp):
        qc, kc, vc, gc = inp
        cum = jnp.cumsum(gc, axis=2)
        last = cum[:, :, -1:]
        inter = jnp.einsum('bhik,bhkv->bhiv', qc * jnp.exp(cum), s)
        decay = jnp.exp(jnp.where(mask, cum[:, :, :, None] - cum[:, :, None], -jnp.inf))
        if gc.shape[-1] == 1:
            att = jnp.einsum('bhik,bhjk->bhij', qc, kc) * decay[..., 0]
        else:
            att = jnp.einsum('bhik,bhjk,bhijk->bhij', qc, kc, decay)
        o = inter + jnp.einsum('bhij,bhjv->bhiv', att, vc)
        s = jnp.exp(last[:, :, 0])[..., None] * s + jnp.einsum('bhjk,bhjv->bhkv', kc * jnp.exp(last - cum), vc)
        return s, o

    s_fin, o = lax.scan(step, s0, (chunks(q), chunks(k), chunks(v), chunks(lg)))
    return o.transpose(1, 2, 0, 3, 4).reshape(b, h, l, -1), s_fin


def final_state(k, v, lg):
    cum = jnp.cumsum(lg, axis=2)
    return jnp.einsum('bhjk,bhjv->bhkv', k * jnp.exp(cum[:, :, -1:] - cum), v)


def bidir_scan(lat_dirs, ctx_dirs, need_ctx):
    out_lat, out_ctx = [], []
    for d, (lat, ctxd) in enumerate(zip(lat_dirs, ctx_dirs)):
        flip = (lambda t: jnp.flip(t, axis=2)) if d == 1 else (lambda t: t)
        qc, kc, vc, gc = (None if t is None else flip(t) for t in ctxd)
        if need_ctx:
            s0 = jnp.zeros(kc.shape[:2] + (kc.shape[-1], vc.shape[-1]), vc.dtype)
            oc, state = chunk_scan(qc, kc, vc, gc, s0)
            out_ctx.append(flip(oc))
        else:
            state = final_state(kc, vc, gc)
        ql, kl, vl, gl = (flip(t) for t in lat)
        ol, _ = chunk_scan(ql, kl, vl, gl, state)
        out_lat.append(flip(ol))
    return out_lat[0] + out_lat[1], (out_ctx[0] + out_ctx[1] if need_ctx else None)


def mla_q(c_q, g_q, w_uq, axial):
    q = to_heads(rms_norm(c_q, g_q) @ w_uq, MLA_HEADS)
    if axial is None:
        return q
    return jnp.concatenate([q[..., :MLA_NOPE], axial_rope(q[..., MLA_NOPE:], axial)], -1)


def mla_kv(c_kv, k_rope, g_kv, w_uk, w_uv, axial):
    kv = rms_norm(c_kv, g_kv)
    k_nope = to_heads(kv @ w_uk, MLA_HEADS)
    v = to_heads(kv @ w_uv, MLA_HEADS)
    if axial is not None:
        k_rope = axial_rope(k_rope, axial)
    k_rope = jnp.broadcast_to(k_rope[:, None], k_nope.shape[:3] + (MLA_ROPE,))
    return jnp.concatenate([k_nope, k_rope], -1), v


def mla_mixer(pl_, pc, g_q, w_uq, g_kv, w_uk, w_uv, axial, need_ctx):
    cq_l, ckv_l, kr_l = pl_
    cq_c, ckv_c, kr_c = pc
    k_c, v_c = mla_kv(ckv_c, kr_c, g_kv, w_uk, w_uv, None)
    k_l, v_l = mla_kv(ckv_l, kr_l, g_kv, w_uk, w_uv, axial)
    q_l = mla_q(cq_l, g_q, w_uq, axial)
    o_l = blocked_attention(q_l, jnp.concatenate([k_c, k_l], 2), jnp.concatenate([v_c, v_l], 2))
    o_c = merge_heads(attention(mla_q(cq_c, g_q, w_uq, None), k_c, v_c)) if need_ctx else None
    return merge_heads(o_l), o_c


def gla_prep(p, w_gf, b_gf, w_gb, b_gb, with_q):
    q, k, v, r, lr_f, lr_b = p
    q = (to_heads(q, GLA_HEADS) * GLA_DK ** -0.5) if with_q else None
    k = to_heads(k, GLA_HEADS)
    v = to_heads(v, GLA_HEADS)

    def log_gate(lr, w, b):
        return to_heads(jax.nn.log_sigmoid(lr @ w + b) / GLA_GATE_NORM, GLA_HEADS)

    return (q, k, v, log_gate(lr_f, w_gf, b_gf)), (q, k, v, log_gate(lr_b, w_gb, b_gb)), r


def gla_mixer(pl_, pc, w_gf, b_gf, w_gb, b_gb, g_out, need_ctx):
    lat_f, lat_b, r_l = gla_prep(pl_, w_gf, b_gf, w_gb, b_gb, True)
    ctx_f, ctx_b, r_c = gla_prep(pc, w_gf, b_gf, w_gb, b_gb, need_ctx)
    o_l, o_c = bidir_scan((lat_f, lat_b), (ctx_f, ctx_b), need_ctx)

    def out(o, r):
        return merge_heads(head_norm(o, False)) * g_out * jax.nn.silu(r)

    return out(o_l, r_l), (out(o_c, r_c) if need_ctx else None)


def depthwise_conv(x, w, b):
    pad = w.shape[0] // 2
    y = lax.conv_general_dilated(x, w[:, None, :], (1,), [(pad, pad)],
                                 dimension_numbers=('NWC', 'WIO', 'NWC'), feature_group_count=x.shape[-1])
    return y + b


def ssd_prep(p, conv_w, conv_b, dtb_f, dtb_b, alog_f, alog_b, with_q):
    z, xbc, dt_f, dt_b = p
    xbc = jax.nn.silu(depthwise_conv(xbc, conv_w, conv_b))
    xs, bm, cm = split_cols(xbc, (SSD_HEADS * SSD_HEADDIM, SSD_BC, SSD_BC))
    rep = SSD_HEADS // SSD_GROUPS
    xh = to_heads(xs, SSD_HEADS)
    bh = jnp.repeat(to_heads(bm, SSD_GROUPS), rep, axis=1)
    ch = jnp.repeat(to_heads(cm, SSD_GROUPS), rep, axis=1) if with_q else None

    def direction(dt_raw, dt_bias, a_log):
        dt = jax.nn.softplus(dt_raw + dt_bias).transpose(0, 2, 1)[..., None]
        la = dt * -jnp.exp(a_log)[:, None, None]
        return (ch, bh * dt, xh, la)

    return direction(dt_f, dtb_f, alog_f), direction(dt_b, dtb_b, alog_b), xh, z


def ssd_mixer(pl_, pc, conv_w, conv_b, dtb_f, dtb_b, alog_f, alog_b, d_skip, g_out, need_ctx):
    lat_f, lat_b, x_l, z_l = ssd_prep(pl_, conv_w, conv_b, dtb_f, dtb_b, alog_f, alog_b, True)
    ctx_f, ctx_b, x_c, z_c = ssd_prep(pc, conv_w, conv_b, dtb_f, dtb_b, alog_f, alog_b, need_ctx)
    o_l, o_c = bidir_scan((lat_f, lat_b), (ctx_f, ctx_b), need_ctx)

    def out(o, xh, z):
        y = merge_heads(o + d_skip[:, None, None] * xh)
        return rms_norm(y * jax.nn.silu(z), g_out)

    return out(o_l, x_l, z_l), (out(o_c, x_c, z_c) if need_ctx else None)


def ret_log_decay():
    return jnp.log1p(-jnp.exp2(-5.0 - jnp.arange(RET_HEADS, dtype=F32)))


def ret_prep(p, rope, with_q):
    q, k, v, g = p
    q = to_heads(q, RET_HEADS) if with_q else None
    k = to_heads(k, RET_HEADS) * RET_DK ** -0.5
    if rope is not None:
        q = rotate(q, rope)
        k = rotate(k, rope)
    v = to_heads(v, RET_HEADS)
    la = jnp.broadcast_to(ret_log_decay()[:, None, None], (k.shape[0], RET_HEADS, k.shape[2], 1))
    d = (q, k, v, la)
    return (d, d), g


def ret_mixer(pl_, pc, rope, need_ctx):
    lat, g_l = ret_prep(pl_, rope, True)
    ctx_d, g_c = ret_prep(pc, None, need_ctx)
    o_l, o_c = bidir_scan(lat, ctx_d, need_ctx)

    def out(o, g):
        return merge_heads(head_norm(o, True)) * jax.nn.silu(g)

    return out(o_l, g_l), (out(o_c, g_c) if need_ctx else None)


PEER_TT = 512
PEER_EB = 512
SUBLANE = 8
_NT = (((1,), (1,)), ((), ()))
NEG_INF = float('-inf')


def _top_values(s, rows_ref):
    prev = None
    for k in range(PEER_TOPK):
        cur = s if prev is None else jnp.where(s < prev, s, NEG_INF)
        prev = jnp.max(cur, axis=0, keepdims=True)
        rows_ref[k:k + 1, :] = prev


def _peer_score_body(x_ref, wqt_ref, k0_ref, k1_ref, s1_ref, e1_ref, s2_ref, e2_ref,
                     qt_scr, v1_scr, v2_scr):
    half = PEER_DKEY // 2
    qt_scr[...] = lax.dot_general(wqt_ref[...], x_ref[...], _NT, preferred_element_type=F32)
    row = lax.broadcasted_iota(jnp.int32, (SUBLANE, LANE), 0)

    def strip(h, tg):
        s1 = s1_ref[h, tg]
        s2 = s2_ref[h, tg]
        _top_values(s1, v1_scr)
        _top_values(s2, v2_scr)
        cand = []
        for b in range(PEER_TOPK):
            n_a = PEER_TOPK // (b + 1)
            v2b = v2_scr[b:b + 1, :]
            for a0 in range(0, n_a, SUBLANE):
                c = v1_scr[a0:a0 + SUBLANE, :] + v2b
                if n_a - a0 < SUBLANE:
                    c = jnp.where(row < n_a - a0, c, NEG_INF)
                cand.append(c)
        prev = None
        cnt = jnp.zeros((1, LANE), F32)
        tau = jnp.full((1, LANE), NEG_INF, F32)
        for _ in range(PEER_TOPK):
            best = None
            for c in cand:
                cur = c if prev is None else jnp.where(c < prev, c, NEG_INF)
                best = cur if best is None else jnp.maximum(best, cur)
            m = jnp.max(best, axis=0, keepdims=True)
            hits = None
            for c in cand:
                eq = jnp.where(c == m, 1.0, 0.0)
                hits = eq if hits is None else hits + eq
            tau = jnp.where(cnt < PEER_TOPK, m, tau)
            cnt = cnt + jnp.sum(hits, axis=0, keepdims=True)
            prev = m
        m1 = v1_scr[0:1, :]
        m2 = v2_scr[0:1, :]
        zs = None
        for c in cand:
            z = jnp.where(c >= tau, jnp.exp(c - (m1 + m2)), 0.0)
            zs = z if zs is None else zs + z
        z_tot = jnp.sum(zs, axis=0, keepdims=True)
        e1_ref[h, tg] = jnp.exp(s1 - m1) / z_tot
        e2_ref[h, tg] = jnp.exp(s2 - m2)
        c1 = jnp.full(s1.shape, jnp.inf, F32)
        for b in range(PEER_TOPK):
            v2b = v2_scr[b:b + 1, :]
            c1 = jnp.where(s1 + v2b >= tau, v2b, c1)
        s1_ref[h, tg] = c1

    def head(h, carry):
        base = pl.multiple_of(h * PEER_DKEY, PEER_DKEY)
        s1 = jnp.dot(k0_ref[...], qt_scr[pl.ds(base, half), :].astype(BF16), preferred_element_type=F32)
        s2 = jnp.dot(k1_ref[...], qt_scr[pl.ds(base + half, half), :].astype(BF16), preferred_element_type=F32)
        for tg in range(x_ref.shape[0] // LANE):
            s1_ref[h, tg] = s1[:, tg * LANE:(tg + 1) * LANE]
            s2_ref[h, tg] = s2[:, tg * LANE:(tg + 1) * LANE]
        for tg in range(x_ref.shape[0] // LANE):
            strip(h, tg)
        return carry

    lax.fori_loop(0, PEER_HEADS, head, 0)


PEER_JQ = 4


def _peer_expert_body(x_ref, u_ref, vt_ref, c1_ref, e1_ref, s2_ref, e2_ref, o_ref,
                      h_scr, g_scr, act_scr, acc_scr, bc1_scr, be1_scr):
    e = pl.program_id(1)
    n_blocks = pl.num_programs(1) - 1
    tt = x_ref.shape[0]
    n_i = u_ref.shape[0] // PEER_NKEYS
    span = PEER_JQ * SUBLANE

    @pl.when(e == 0)
    def _():
        acc_scr[...] = jnp.zeros_like(acc_scr)
        act_scr[...] = jnp.zeros_like(act_scr)

    m_rows = acc_scr.shape[0] // n_i

    def expert_pair(pair, carry):
        u0 = pl.multiple_of(pair * 2 * PEER_NKEYS, 2 * PEER_NKEYS)
        h_scr[pl.ds(u0, 2 * PEER_NKEYS), :] = lax.dot_general(u_ref[pl.ds(u0, 2 * PEER_NKEYS), :], x_ref[...], _NT,
                                                              preferred_element_type=F32)
        expert_row(2 * pair, carry)
        return expert_row(2 * pair + 1, carry)

    def expert_row(i_loc, carry):
        m0 = pl.multiple_of(i_loc * m_rows, m_rows)
        acc_scr[pl.ds(m0, m_rows), :] += jnp.dot(vt_ref[pl.ds(m0, m_rows), :], act_scr[...],
                                                  preferred_element_type=F32)
        i_glob = jnp.minimum(e, n_blocks - 1) * n_i + i_loc
        r_i = pl.multiple_of(i_loc * PEER_NKEYS, PEER_NKEYS)
        for tg in range(tt // LANE):
            for h in range(PEER_HEADS):
                bc1_scr[i_loc, tg, h] = jnp.broadcast_to(c1_ref[h, tg, pl.ds(i_glob, 1), :], (SUBLANE, LANE))
                be1_scr[i_loc, tg, h] = jnp.broadcast_to(e1_ref[h, tg, pl.ds(i_glob, 1), :], (SUBLANE, LANE))
        for tg in range(tt // LANE):
            lanes = slice(tg * LANE, (tg + 1) * LANE)
            for j0 in range(0, PEER_NKEYS, span):
                g = [None] * PEER_JQ
                for h in range(PEER_HEADS):
                    c = bc1_scr[i_loc, tg, h]
                    b = be1_scr[i_loc, tg, h]
                    for k in range(PEER_JQ):
                        rows = slice(j0 + k * SUBLANE, j0 + (k + 1) * SUBLANE)
                        gh = jnp.where(s2_ref[h, tg, rows, :] >= c, b * e2_ref[h, tg, rows, :], 0.0)
                        g[k] = gh if g[k] is None else g[k] + gh
                for k in range(PEER_JQ):
                    g_scr[pl.ds(r_i + j0 + k * SUBLANE, SUBLANE), lanes] = g[k]
        return carry

    lax.fori_loop(0, n_i // 2, expert_pair, 0)
    hv = h_scr[...]
    act_scr[...] = (0.5 * hv * (1.0 + lax.erf(hv * (2.0 ** -0.5))) * g_scr[...]).astype(BF16)

    @pl.when(e == n_blocks)
    def _():
        o_ref[...] = acc_scr[...].T


def peer_dense(xm, wqt, k0, k1, u_bf, vt_bf, tt=PEER_TT, eb=PEER_EB):
    t, d = xm.shape
    n_blocks = u_bf.shape[0] // eb
    xb = xm.astype(BF16)
    hk = (PEER_HEADS, t // LANE, PEER_NKEYS, LANE)
    sblk = pl.BlockSpec((PEER_HEADS, tt // LANE, PEER_NKEYS, LANE), lambda i: (0, i, 0, 0))
    c1, e1, s2, e2 = pl.pallas_call(
        _peer_score_body,
        grid=(t // tt,),
        in_specs=[pl.BlockSpec((tt, d), lambda i: (i, 0)),
                  pl.BlockSpec(wqt.shape, lambda i: (0, 0)),
                  pl.BlockSpec(k0.shape, lambda i: (0, 0)),
                  pl.BlockSpec(k1.shape, lambda i: (0, 0))],
        out_specs=[sblk, sblk, sblk, sblk],
        out_shape=[jax.ShapeDtypeStruct(hk, F32)] * 4,
        scratch_shapes=[pltpu.VMEM((PEER_HEADS * PEER_DKEY, tt), F32),
                        pltpu.VMEM((PEER_TOPK, LANE), F32), pltpu.VMEM((PEER_TOPK, LANE), F32)],
        compiler_params=pltpu.CompilerParams(vmem_limit_bytes=48 * 2 ** 20),
        name='peer_score',
    )(xb, wqt, k0, k1)
    sblk2 = pl.BlockSpec((PEER_HEADS, tt // LANE, PEER_NKEYS, LANE), lambda i, j: (0, i, 0, 0))
    return pl.pallas_call(
        _peer_expert_body,
        grid=(t // tt, n_blocks + 1),
        in_specs=[pl.BlockSpec((tt, d), lambda i, j: (i, 0)),
                  pl.BlockSpec((eb, d), lambda i, j: (jnp.minimum(j, n_blocks - 1), 0)),
                  pl.BlockSpec((d, eb), lambda i, j: (0, jnp.maximum(j - 1, 0))),
                  sblk2, sblk2, sblk2, sblk2],
        out_specs=pl.BlockSpec((tt, d), lambda i, j: (i, 0)),
        out_shape=jax.ShapeDtypeStruct((t, d), F32),
        scratch_shapes=[pltpu.VMEM((eb, tt), F32), pltpu.VMEM((eb, tt), F32), pltpu.VMEM((eb, tt), BF16),
                        pltpu.VMEM((d, tt), F32)]
        + [pltpu.VMEM((eb // PEER_NKEYS, tt // LANE, PEER_HEADS, SUBLANE, LANE), F32)] * 2,
        compiler_params=pltpu.CompilerParams(vmem_limit_bytes=48 * 2 ** 20,
                                             dimension_semantics=("arbitrary", "arbitrary")),
        name='peer_expert',
    )(xb, u_bf, vt_bf, c1, e1, s2, e2)


SCAN_TL = 256
CTX_LEN = 256
_HI = lax.Precision.HIGHEST
_TN = (((0,), (0,)), ((), ()))


def _scan_body(qf_ref, kf_ref, vf_ref, lgf_ref, ltf_ref, qb_ref, kb_ref, vb_ref, lgb_ref, ltb_ref,
               of_ref, ob_ref, sf_scr, sb_scr, *, chunk, heads, per_dim):
    tl, kd = qf_ref.shape[1], qf_ref.shape[2]
    vd = vf_ref.shape[2]
    dk, dv = kd // heads, vd // heads

    @pl.when(pl.program_id(1) == 0)
    def _():
        sf_scr[...] = jnp.zeros_like(sf_scr)
        sb_scr[...] = jnp.zeros_like(sb_scr)

    ri = lax.broadcasted_iota(jnp.int32, (chunk, chunk), 0)
    ci = lax.broadcasted_iota(jnp.int32, (chunk, chunk), 1)
    khead = lax.broadcasted_iota(jnp.int32, (1, kd), 1) // dk
    vhead = lax.broadcasted_iota(jnp.int32, (1, vd), 1) // dv
    same_head = (lax.broadcasted_iota(jnp.int32, (vd, kd), 0) // dv
                 == lax.broadcasted_iota(jnp.int32, (vd, kd), 1) // dk)

    def per_key(x8):
        out = jnp.zeros((x8.shape[0], kd), F32)
        for h in range(heads):
            out = jnp.where(khead == h, x8[:, h:h + 1], out)
        return out

    def sweep(q_ref, k_ref, v_ref, lg_ref, lt_ref, o_ref, s_scr, fwd):
        keep = (ri >= ci) if fwd else (ri <= ci)
        tri = keep.astype(F32)
        order = range(tl // chunk) if fwd else reversed(range(tl // chunk))
        for c in order:
            rows = slice(c * chunk, (c + 1) * chunk)
            q, k, v = q_ref[0, rows, :], k_ref[0, rows, :], v_ref[0, rows, :]
            cum = jnp.dot(tri, lg_ref[0, rows, :], precision=_HI, preferred_element_type=F32)
            tot = cum[chunk - 1:chunk, :] if fwd else cum[0:1, :]
            if per_dim:
                e_in, e_out, e_tot = jnp.exp(cum), jnp.exp(tot - cum), jnp.exp(tot)
                mid = cum[chunk // 2:chunk // 2 + 1, :]
                q_att = (q * jnp.exp(cum - mid))
                k_att = (k * jnp.exp(mid - cum)).astype(BF16)
            else:
                e_in, e_out, e_tot = per_key(jnp.exp(cum)), per_key(jnp.exp(tot - cum)), per_key(jnp.exp(tot))
                cum_t = lax.dot_general(lt_ref[0, :, rows], tri, _NT, precision=_HI, preferred_element_type=F32)
                q_att, k_att = q, k.astype(BF16)
            s = s_scr[...]
            o = lax.dot_general((q * e_in).astype(BF16), s.astype(BF16), _NT, preferred_element_type=F32)
            for h in range(heads):
                sc = lax.dot_general(jnp.where(khead == h, q_att, 0.0).astype(BF16), k_att, _NT,
                                     preferred_element_type=F32)
                if per_dim:
                    att = jnp.where(keep, sc, 0.0)
                else:
                    att = sc * jnp.exp(jnp.where(keep, cum[:, h:h + 1] - cum_t[h:h + 1, :], NEG_INF))
                o = o + jnp.dot(att.astype(BF16), jnp.where(vhead == h, v, 0.0).astype(BF16),
                                preferred_element_type=F32)
            o_ref[0, rows, :] = o
            kv = lax.dot_general(v.astype(BF16), (k * e_out).astype(BF16), _TN, preferred_element_type=F32)
            s_scr[...] = s * e_tot + jnp.where(same_head, kv, 0.0)

    sweep(qf_ref, kf_ref, vf_ref, lgf_ref, ltf_ref, of_ref, sf_scr, True)
    sweep(qb_ref, kb_ref, vb_ref, lgb_ref, ltb_ref, ob_ref, sb_scr, False)


def scan_bidir(q, k_f, k_b, v, lg_f, lg_b, lt_f, lt_b, *, heads, per_dim, chunk):
    b, l, kd = q.shape
    vd = v.shape[-1]
    n, n_ctx = l // SCAN_TL, CTX_LEN // SCAN_TL

    def bwd_group(s):
        return jnp.where(s < n_ctx, n_ctx - 1 - s, n - 1 - (s - n_ctx))

    def specs(group):
        tok = lambda w: pl.BlockSpec((1, SCAN_TL, w), lambda bi, s: (bi, group(s), 0))
        return [tok(kd), tok(kd), tok(vd), tok(LANE),
                pl.BlockSpec((1, SUBLANE, SCAN_TL), lambda bi, s: (bi, 0, group(s)))]

    fwd_group = lambda s: s
    o_f, o_b = pl.pallas_call(
        functools.partial(_scan_body, chunk=chunk, heads=heads, per_dim=per_dim),
        grid=(b, n),
        in_specs=specs(fwd_group) + specs(bwd_group),
        out_specs=[pl.BlockSpec((1, SCAN_TL, vd), lambda bi, s: (bi, s, 0)),
                   pl.BlockSpec((1, SCAN_TL, vd), lambda bi, s: (bi, bwd_group(s), 0))],
        out_shape=[jax.ShapeDtypeStruct((b, l, vd), F32)] * 2,
        scratch_shapes=[pltpu.VMEM((vd, kd), F32)] * 2,
        compiler_params=pltpu.CompilerParams(dimension_semantics=("arbitrary", "arbitrary")),
        name='scan_bidir',
    )(q, k_f, v, lg_f, lt_f, q, k_b, v, lg_b, lt_b)
    return o_f + o_b


def _ctx_identity(tab):
    cos, sin = tab
    pad = (CTX_LEN, cos.shape[1])
    return jnp.concatenate([jnp.ones(pad, F32), cos], 0), jnp.concatenate([jnp.zeros(pad, F32), sin], 0)


def _pad_lanes(t):
    return jnp.pad(t, ((0, 0), (0, 0), (0, LANE - t.shape[-1])))


def _head_rows(t):
    return jnp.pad(t.transpose(0, 2, 1), ((0, 0), (0, SUBLANE - t.shape[-1]), (0, 0)))


def _split_heads(t, h):
    return t.reshape(t.shape[:-1] + (h, t.shape[-1] // h))


def _group_norm(o, heads, centre):
    o4 = _split_heads(o, heads)
    return head_norm(o4, centre).reshape(o.shape)


def _rotate_heads(t, heads, tab):
    cos, sin = tab
    return rotate(_split_heads(t, heads), (cos[:, None, :], sin[:, None, :])).reshape(t.shape)


def _segment_conv(x, w, b):
    def conv(t):
        z = jnp.zeros_like(t[:, :1])
        return (jnp.concatenate([z, t[:, :-1]], 1) * w[0] + t * w[1] + jnp.concatenate([t[:, 1:], z], 1) * w[2]) + b

    return jnp.concatenate([conv(x[:, :CTX_LEN]), conv(x[:, CTX_LEN:])], 1)


def gla_tokens(p, w_gf, b_gf, w_gb, b_gb, g_out):
    q, k, v, r, lr_f, lr_b = p
    no_rows = jnp.zeros((q.shape[0], SUBLANE, q.shape[1]), F32)

    def log_gate(lr, w, b):
        return jax.nn.log_sigmoid(_mm(lr, w) + b) / GLA_GATE_NORM

    o = scan_bidir(q * GLA_DK ** -0.5, k, k, v, log_gate(lr_f, w_gf, b_gf), log_gate(lr_b, w_gb, b_gb),
                   no_rows, no_rows, heads=GLA_HEADS, per_dim=True, chunk=CHUNK)
    return _group_norm(o, GLA_HEADS, False) * g_out * jax.nn.silu(r)


def ssd_tokens(p, conv_w, conv_b, dtb_f, dtb_b, alog_f, alog_b, d_skip, g_out):
    z, xbc, dt_f, dt_b = p
    xbc = jax.nn.silu(_segment_conv(xbc, conv_w, conv_b))
    xs, bm, cm = split_cols(xbc, (SSD_HEADS * SSD_HEADDIM, SSD_BC, SSD_BC))
    rep = SSD_HEADS // SSD_GROUPS

    def per_head(t):
        return jnp.repeat(_split_heads(t, SSD_GROUPS), rep, axis=2).reshape(t.shape[:-1] + (SSD_HEADS * SSD_STATE,))

    b_all = per_head(bm)

    def direction(dt_raw, dt_bias, a_log):
        dt = jax.nn.softplus(dt_raw + dt_bias)
        return b_all * jnp.repeat(dt, SSD_STATE, axis=-1), dt * -jnp.exp(a_log)

    k_f, la_f = direction(dt_f, dtb_f, alog_f)
    k_b, la_b = direction(dt_b, dtb_b, alog_b)
    o = scan_bidir(per_head(cm), k_f, k_b, xs, _pad_lanes(la_f), _pad_lanes(la_b), _head_rows(la_f), _head_rows(la_b),
                   heads=SSD_HEADS, per_dim=False, chunk=SCAN_TL)
    y = o + jnp.repeat(d_skip, SSD_HEADDIM) * xs
    return rms_norm(y * jax.nn.silu(z), g_out)


def ret_tokens(p, rope):
    q, k, v, g = p
    q = _rotate_heads(q, RET_HEADS, rope)
    k = _rotate_heads(k * RET_DK ** -0.5, RET_HEADS, rope)
    la = jnp.broadcast_to(ret_log_decay(), q.shape[:2] + (RET_HEADS,))
    o = scan_bidir(q, k, k, v, _pad_lanes(la), _pad_lanes(la), _head_rows(la), _head_rows(la),
                   heads=RET_HEADS, per_dim=False, chunk=SCAN_TL)
    return _group_norm(o, RET_HEADS, True) * jax.nn.silu(g)


def mla_tokens(p, g_q, w_uq, g_kv, w_uk, w_uv, axial):
    c_q, c_kv, k_r = p
    kv = rms_norm(c_kv, g_kv)
    k_nope = to_heads(_mm(kv, w_uk), MLA_HEADS)
    v = to_heads(_mm(kv, w_uv), MLA_HEADS)
    k_rope = axial_rope(k_r, axial)
    k = jnp.concatenate([k_nope, jnp.broadcast_to(k_rope[:, None], k_nope.shape[:3] + (MLA_ROPE,))], -1)
    q = to_heads(_mm(rms_norm(c_q, g_q), w_uq), MLA_HEADS)
    q = jnp.concatenate([q[..., :MLA_NOPE], axial_rope(q[..., MLA_NOPE:], axial)], -1)
    return attention_tokens(q.astype(BF16), k.astype(BF16), v.astype(BF16))


ATTN_TQ = 256


def _attn_body(q_ref, k_ref, v_ref, o_ref):
    heads, l, dv = v_ref.shape[1], v_ref.shape[2], v_ref.shape[3]
    col = lax.broadcasted_iota(jnp.int32, (1, l), 1)
    visible = (col < CTX_LEN) | (pl.program_id(1) > 0)
    scale = q_ref.shape[-1] ** -0.5
    for h in range(heads):
        s = lax.dot_general(q_ref[0, h], k_ref[0, h], _NT, preferred_element_type=F32) * scale
        s = jnp.where(visible, s, NEG_INF)
        p = jnp.exp(s - jnp.max(s, axis=1, keepdims=True))
        o = jnp.dot(p.astype(BF16), v_ref[0, h], preferred_element_type=F32)
        o_ref[0, :, h * dv:(h + 1) * dv] = o / jnp.sum(p, axis=1, keepdims=True)


def attention_tokens(q, k, v):
    b, h, l, d = q.shape
    dv = v.shape[-1]
    return pl.pallas_call(
        _attn_body,
        grid=(b, l // ATTN_TQ),
        in_specs=[pl.BlockSpec((1, h, ATTN_TQ, d), lambda bi, qi: (bi, 0, qi, 0)),
                  pl.BlockSpec((1, h, l, d), lambda bi, qi: (bi, 0, 0, 0)),
                  pl.BlockSpec((1, h, l, dv), lambda bi, qi: (bi, 0, 0, 0))],
        out_specs=pl.BlockSpec((1, ATTN_TQ, h * dv), lambda bi, qi: (bi, qi, 0)),
        out_shape=jax.ShapeDtypeStruct((b, l, h * dv), F32),
        compiler_params=pltpu.CompilerParams(vmem_limit_bytes=48 * 2 ** 20),
        name='mla_attention',
    )(q, k, v)


def kernel(x, c, ctx, c_ctx, w_ada, b_ada, w_in, mla_g_q, mla_w_uq, mla_g_kv, mla_w_uk, mla_w_uv, gla_w_gf, gla_b_gf, gla_w_gb, gla_b_gb, gla_g, ssd_conv_w, ssd_conv_b, ssd_dt_bias_f, ssd_dt_bias_b, ssd_a_log_f, ssd_a_log_b, ssd_d, ssd_g, w_out, ln1_g, ln1_b, peer_w_q, peer_keys, peer_u, peer_v, ln2_g, ln2_b):
    seq = x.shape[1]
    rows = seq // GRID_W
    row = jnp.repeat(jnp.arange(rows), GRID_W)
    col = jnp.tile(jnp.arange(GRID_W), rows)
    axial = (_ctx_identity(rope_tables(row, MLA_ROPE // 2)), _ctx_identity(rope_tables(col, MLA_ROPE // 2)))
    ret_rope = _ctx_identity(rope_tables(jnp.arange(seq), RET_DK))
    s_lat = jax.nn.silu(c)
    s_ctx = jax.nn.silu(c_ctx)
    xc = ctx
    for i in range(DEPTH):
        need_ctx = i < DEPTH - 1
        mod = [t[:, None, :] for t in jnp.split(s_lat @ w_ada[i] + b_ada[i], 6, axis=-1)]
        mod_c = jnp.split(s_ctx @ w_ada[i] + b_ada[i], 6, axis=-1)
        xin = jnp.concatenate([modulate(xc, mod_c[0], mod_c[1]), modulate(x, mod[0], mod[1])], 1)
        p_mla, p_gla, p_ssd, p_ret = partition(split_cols(_mm(xin, w_in[i]), IN_SIZES))
        outs = (
            mla_tokens(p_mla, mla_g_q[i], mla_w_uq[i], mla_g_kv[i], mla_w_uk[i], mla_w_uv[i], axial),
            gla_tokens(p_gla, gla_w_gf[i], gla_b_gf[i], gla_w_gb[i], gla_b_gb[i], gla_g[i]),
            ssd_tokens(p_ssd, ssd_conv_w[i], ssd_conv_b[i], ssd_dt_bias_f[i], ssd_dt_bias_b[i],
                       ssd_a_log_f[i], ssd_a_log_b[i], ssd_d[i], ssd_g[i]),
            ret_tokens(p_ret, ret_rope),
        )
        mixed = jnp.concatenate(outs, -1)
        y = _mm(mixed[:, CTX_LEN:], w_out[i])
        x = layer_norm(DEEPNORM_ALPHA * x + mod[2] * y, ln1_g[i], ln1_b[i])
        toks = [modulate(x, mod[3], mod[4]).reshape(-1, D_MODEL)]
        if need_ctx:
            yc = _mm(mixed[:, :CTX_LEN], w_out[i])
            xc = layer_norm(DEEPNORM_ALPHA * xc + mod_c[2] * yc, ln1_g[i], ln1_b[i])
            toks.append(modulate(xc, mod_c[3], mod_c[4]).reshape(-1, D_MODEL))
        n_lat = toks[0].shape[0]
        f_all = peer_dense(jnp.concatenate(toks, 0), peer_w_q[i].T.astype(BF16), peer_keys[i, 0].astype(BF16),
                           peer_keys[i, 1].astype(BF16), peer_u[i].astype(BF16), peer_v[i].T.astype(BF16))
        x = layer_norm(DEEPNORM_ALPHA * x + mod[5] * f_all[:n_lat].reshape(x.shape), ln2_g[i], ln2_b[i])
        if need_ctx:
            xc = layer_norm(DEEPNORM_ALPHA * xc + mod_c[5] * f_all[n_lat:].reshape(xc.shape), ln2_g[i], ln2_b[i])
    return x
```

```python
import functools
import math

import jax
import jax.numpy as jnp
import numpy as np
from jax import lax
from jax.experimental import pallas as pl
from jax.experimental.pallas import tpu as pltpu

D_MODEL = 1024
DEPTH = 4
GRID_W = 64
CHUNK = 64
Q_BLOCK = 128
ROPE_BASE = 10000.0
MLA_HEADS, MLA_NOPE, MLA_ROPE, MLA_V, MLA_Q_LORA, MLA_KV_LORA = 4, 64, 32, 64, 256, 128
GLA_HEADS, GLA_DK, GLA_DV, GLA_GATE_RANK, GLA_GATE_NORM = 4, 32, 64, 16, 16.0
SSD_HEADS, SSD_HEADDIM, SSD_GROUPS, SSD_STATE = 4, 64, 2, 128
RET_HEADS, RET_DK, RET_DV = 4, 32, 64
PEER_HEADS, PEER_NKEYS, PEER_DKEY, PEER_TOPK, PEER_TOKEN_BLOCK = 8, 128, 256, 16, 128
DEEPNORM_ALPHA = (2 * DEPTH) ** 0.25

SSD_BC = SSD_GROUPS * SSD_STATE
SSD_XBC = SSD_HEADS * SSD_HEADDIM + 2 * SSD_BC
MLA_SIZES = (MLA_Q_LORA, MLA_KV_LORA, MLA_ROPE)
GLA_SIZES = (GLA_HEADS * GLA_DK, GLA_HEADS * GLA_DK, GLA_HEADS * GLA_DV, GLA_HEADS * GLA_DV, GLA_GATE_RANK, GLA_GATE_RANK)
SSD_SIZES = (SSD_HEADS * SSD_HEADDIM, SSD_XBC, SSD_HEADS, SSD_HEADS)
RET_SIZES = (RET_HEADS * RET_DK, RET_HEADS * RET_DK, RET_HEADS * RET_DV, RET_HEADS * RET_DV)
IN_SIZES = MLA_SIZES + GLA_SIZES + SSD_SIZES + RET_SIZES

F32 = jnp.float32
BF16 = jnp.bfloat16
LANE = 128


def _mm_body(a_ref, b_ref, o_ref):
    o_ref[...] = jnp.dot(a_ref[...].astype(BF16), b_ref[...].astype(BF16), preferred_element_type=F32)


def _mm(a, b, tm=512, tn=512):
    lead = a.shape[:-1]
    k = a.shape[-1]
    n = b.shape[-1]
    a2 = a.reshape(-1, k)
    m = a2.shape[0]
    tm = math.gcd(tm, m)
    n_pad = -(-n // tn) * tn if n > tn else -(-n // LANE) * LANE
    tn = min(tn, n_pad)
    if n_pad != n:
        b = jnp.pad(b, ((0, 0), (0, n_pad - n)))
    out = pl.pallas_call(
        _mm_body,
        grid=(m // tm, n_pad // tn),
        in_specs=[pl.BlockSpec((tm, k), lambda i, j: (i, 0)), pl.BlockSpec((k, tn), lambda i, j: (0, j))],
        out_specs=pl.BlockSpec((tm, tn), lambda i, j: (i, j)),
        out_shape=jax.ShapeDtypeStruct((m, n_pad), F32),
    )(a2, b)
    return out[:, :n].reshape(lead + (n,))


def layer_norm(x, g, b, eps=1e-5):
    mu = x.mean(-1, keepdims=True)
    xc = x - mu
    var = jnp.mean(xc * xc, -1, keepdims=True)
    return xc * lax.rsqrt(var + eps) * g + b


def rms_norm(x, g, eps=1e-6):
    return x * lax.rsqrt(jnp.mean(x * x, -1, keepdims=True) + eps) * g


def head_norm(o, centre):
    if centre:
        o = o - o.mean(-1, keepdims=True)
    return o * lax.rsqrt(jnp.mean(o * o, -1, keepdims=True) + 1e-6)


def modulate(x, shift, scale):
    return x * (1 + scale) + shift


def split_cols(p, sizes):
    return jnp.split(p, np.cumsum(sizes)[:-1].tolist(), axis=-1)


def partition(pieces):
    a = len(MLA_SIZES)
    b = a + len(GLA_SIZES)
    s = b + len(SSD_SIZES)
    return pieces[:a], pieces[a:b], pieces[b:s], pieces[s:]


def to_heads(t, h):
    b, l, _ = t.shape
    return t.reshape(b, l, h, -1).transpose(0, 2, 1, 3)


def merge_heads(t):
    b, h, l, d = t.shape
    return t.transpose(0, 2, 1, 3).reshape(b, l, h * d)


def rope_tables(pos, dim):
    half = dim // 2
    freqs = ROPE_BASE ** (-jnp.arange(half, dtype=F32) / half)
    ang = pos.astype(F32)[:, None] * freqs
    return jnp.cos(ang), jnp.sin(ang)


def rotate(x, tab):
    cos, sin = tab
    h = x.shape[-1] // 2
    x1, x2 = x[..., :h], x[..., h:]
    return jnp.concatenate([x1 * cos - x2 * sin, x1 * sin + x2 * cos], -1)


def axial_rope(x, axial):
    h = x.shape[-1] // 2
    return jnp.concatenate([rotate(x[..., :h], axial[0]), rotate(x[..., h:], axial[1])], -1)


def attention(q, k, v):
    s = jnp.einsum('bhqd,bhkd->bhqk', q, k) * (q.shape[-1] ** -0.5)
    return jnp.einsum('bhqk,bhkv->bhqv', jax.nn.softmax(s, axis=-1), v)


def blocked_attention(q, k, v):
    b, h, l, d = q.shape
    nb = l // Q_BLOCK
    qb = q.reshape(b, h, nb, Q_BLOCK, d).transpose(2, 0, 1, 3, 4)
    ob = lax.map(lambda qq: attention(qq, k, v), qb)
    return ob.transpose(1, 2, 0, 3, 4).reshape(b, h, l, -1)


def chunk_scan(q, k, v, lg, s0):
    b, h, l, _ = k.shape
    n = l // CHUNK
    mask = jnp.tril(jnp.ones((CHUNK, CHUNK), bool))[:, :, None]

    def chunks(t):
        return t.reshape(b, h, n, CHUNK, t.shape[-1]).transpose(2, 0, 1, 3, 4)

    def step(s, inp):
        qc, kc, vc, gc = inp
        cum = jnp.cumsum(gc, axis=2)
        last = cum[:, :, -1:]
        inter = jnp.einsum('bhik,bhkv->bhiv', qc * jnp.exp(cum), s)
        decay = jnp.exp(jnp.where(mask, cum[:, :, :, None] - cum[:, :, None], -jnp.inf))
        if gc.shape[-1] == 1:
            att = jnp.einsum('bhik,bhjk->bhij', qc, kc) * decay[..., 0]
        else:
            att = jnp.einsum('bhik,bhjk,bhijk->bhij', qc, kc, decay)
        o = inter + jnp.einsum('bhij,bhjv->bhiv', att, vc)
        s = jnp.exp(last[:, :, 0])[..., None] * s + jnp.einsum('bhjk,bhjv->bhkv', kc * jnp.exp(last - cum), vc)
        return s, o

    s_fin, o = lax.scan(step, s0, (chunks(q), chunks(k), chunks(v), chunks(lg)))
    return o.transpose(1, 2, 0, 3, 4).reshape(b, h, l, -1), s_fin


def final_state(k, v, lg):
    cum = jnp.cumsum(lg, axis=2)
    return jnp.einsum('bhjk,bhjv->bhkv', k * jnp.exp(cum[:, :, -1:] - cum), v)


def bidir_scan(lat_dirs, ctx_dirs, need_ctx):
    out_lat, out_ctx = [], []
    for d, (lat, ctxd) in enumerate(zip(lat_dirs, ctx_dirs)):
        flip = (lambda t: jnp.flip(t, axis=2)) if d == 1 else (lambda t: t)
        qc, kc, vc, gc = (None if t is None else flip(t) for t in ctxd)
        if need_ctx:
            s0 = jnp.zeros(kc.shape[:2] + (kc.shape[-1], vc.shape[-1]), vc.dtype)
            oc, state = chunk_scan(qc, kc, vc, gc, s0)
            out_ctx.append(flip(oc))
        else:
            state = final_state(kc, vc, gc)
        ql, kl, vl, gl = (flip(t) for t in lat)
        ol, _ = chunk_scan(ql, kl, vl, gl, state)
        out_lat.append(flip(ol))
    return out_lat[0] + out_lat[1], (out_ctx[0] + out_ctx[1] if need_ctx else None)


def mla_q(c_q, g_q, w_uq, axial):
    q = to_heads(rms_norm(c_q, g_q) @ w_uq, MLA_HEADS)
    if axial is None:
        return q
    return jnp.concatenate([q[..., :MLA_NOPE], axial_rope(q[..., MLA_NOPE:], axial)], -1)


def mla_kv(c_kv, k_rope, g_kv, w_uk, w_uv, axial):
    kv = rms_norm(c_kv, g_kv)
    k_nope = to_heads(kv @ w_uk, MLA_HEADS)
    v = to_heads(kv @ w_uv, MLA_HEADS)
    if axial is not None:
        k_rope = axial_rope(k_rope, axial)
    k_rope = jnp.broadcast_to(k_rope[:, None], k_nope.shape[:3] + (MLA_ROPE,))
    return jnp.concatenate([k_nope, k_rope], -1), v


def mla_mixer(pl_, pc, g_q, w_uq, g_kv, w_uk, w_uv, axial, need_ctx):
    cq_l, ckv_l, kr_l = pl_
    cq_c, ckv_c, kr_c = pc
    k_c, v_c = mla_kv(ckv_c, kr_c, g_kv, w_uk, w_uv, None)
    k_l, v_l = mla_kv(ckv_l, kr_l, g_kv, w_uk, w_uv, axial)
    q_l = mla_q(cq_l, g_q, w_uq, axial)
    o_l = blocked_attention(q_l, jnp.concatenate([k_c, k_l], 2), jnp.concatenate([v_c, v_l], 2))
    o_c = merge_heads(attention(mla_q(cq_c, g_q, w_uq, None), k_c, v_c)) if need_ctx else None
    return merge_heads(o_l), o_c


def gla_prep(p, w_gf, b_gf, w_gb, b_gb, with_q):
    q, k, v, r, lr_f, lr_b = p
    q = (to_heads(q, GLA_HEADS) * GLA_DK ** -0.5) if with_q else None
    k = to_heads(k, GLA_HEADS)
    v = to_heads(v, GLA_HEADS)

    def log_gate(lr, w, b):
        return to_heads(jax.nn.log_sigmoid(lr @ w + b) / GLA_GATE_NORM, GLA_HEADS)

    return (q, k, v, log_gate(lr_f, w_gf, b_gf)), (q, k, v, log_gate(lr_b, w_gb, b_gb)), r


def gla_mixer(pl_, pc, w_gf, b_gf, w_gb, b_gb, g_out, need_ctx):
    lat_f, lat_b, r_l = gla_prep(pl_, w_gf, b_gf, w_gb, b_gb, True)
    ctx_f, ctx_b, r_c = gla_prep(pc, w_gf, b_gf, w_gb, b_gb, need_ctx)
    o_l, o_c = bidir_scan((lat_f, lat_b), (ctx_f, ctx_b), need_ctx)

    def out(o, r):
        return merge_heads(head_norm(o, False)) * g_out * jax.nn.silu(r)

    return out(o_l, r_l), (out(o_c, r_c) if need_ctx else None)


def depthwise_conv(x, w, b):
    pad = w.shape[0] // 2
    y = lax.conv_general_dilated(x, w[:, None, :], (1,), [(pad, pad)],
                                 dimension_numbers=('NWC', 'WIO', 'NWC'), feature_group_count=x.shape[-1])
    return y + b


def ssd_prep(p, conv_w, conv_b, dtb_f, dtb_b, alog_f, alog_b, with_q):
    z, xbc, dt_f, dt_b = p
    xbc = jax.nn.silu(depthwise_conv(xbc, conv_w, conv_b))
    xs, bm, cm = split_cols(xbc, (SSD_HEADS * SSD_HEADDIM, SSD_BC, SSD_BC))
    rep = SSD_HEADS // SSD_GROUPS
    xh = to_heads(xs, SSD_HEADS)
    bh = jnp.repeat(to_heads(bm, SSD_GROUPS), rep, axis=1)
    ch = jnp.repeat(to_heads(cm, SSD_GROUPS), rep, axis=1) if with_q else None

    def direction(dt_raw, dt_bias, a_log):
        dt = jax.nn.softplus(dt_raw + dt_bias).transpose(0, 2, 1)[..., None]
        la = dt * -jnp.exp(a_log)[:, None, None]
        return (ch, bh * dt, xh, la)

    return direction(dt_f, dtb_f, alog_f), direction(dt_b, dtb_b, alog_b), xh, z


def ssd_mixer(pl_, pc, conv_w, conv_b, dtb_f, dtb_b, alog_f, alog_b, d_skip, g_out, need_ctx):
    lat_f, lat_b, x_l, z_l = ssd_prep(pl_, conv_w, conv_b, dtb_f, dtb_b, alog_f, alog_b, True)
    ctx_f, ctx_b, x_c, z_c = ssd_prep(pc, conv_w, conv_b, dtb_f, dtb_b, alog_f, alog_b, need_ctx)
    o_l, o_c = bidir_scan((lat_f, lat_b), (ctx_f, ctx_b), need_ctx)

    def out(o, xh, z):
        y = merge_heads(o + d_skip[:, None, None] * xh)
        return rms_norm(y * jax.nn.silu(z), g_out)

    return out(o_l, x_l, z_l), (out(o_c, x_c, z_c) if need_ctx else None)


def ret_log_decay():
    return jnp.log1p(-jnp.exp2(-5.0 - jnp.arange(RET_HEADS, dtype=F32)))


def ret_prep(p, rope, with_q):
    q, k, v, g = p
    q = to_heads(q, RET_HEADS) if with_q else None
    k = to_heads(k, RET_HEADS) * RET_DK ** -0.5
    if rope is not None:
        q = rotate(q, rope)
        k = rotate(k, rope)
    v = to_heads(v, RET_HEADS)
    la = jnp.broadcast_to(ret_log_decay()[:, None, None], (k.shape[0], RET_HEADS, k.shape[2], 1))
    d = (q, k, v, la)
    return (d, d), g


def ret_mixer(pl_, pc, rope, need_ctx):
    lat, g_l = ret_prep(pl_, rope, True)
    ctx_d, g_c = ret_prep(pc, None, need_ctx)
    o_l, o_c = bidir_scan(lat, ctx_d, need_ctx)

    def out(o, g):
        return merge_heads(head_norm(o, True)) * jax.nn.silu(g)

    return out(o_l, g_l), (out(o_c, g_c) if need_ctx else None)


PEER_TT = 512
PEER_EB = 512
SUBLANE = 8
_NT = (((1,), (1,)), ((), ()))
NEG_INF = float('-inf')
PEER_NCAND = sum(PEER_TOPK // (b + 1) for b in range(PEER_TOPK))


def _merge_exchanges(lo, hi, r):
    step = r * 2
    if step < hi - lo:
        yield from _merge_exchanges(lo, hi, step)
        yield from _merge_exchanges(lo + r, hi, step)
        yield from ((i, i + r) for i in range(lo + r, hi - r, step))
    else:
        yield (lo, lo + r)


def _sort_exchanges(lo, hi):
    if hi > lo:
        mid = lo + (hi - lo) // 2
        yield from _sort_exchanges(lo, mid)
        yield from _sort_exchanges(mid + 1, hi)
        yield from _merge_exchanges(lo, hi, 1)


def _top_values(s, rows_ref):
    n = s.shape[0] // SUBLANE
    v = [s[t * SUBLANE:(t + 1) * SUBLANE, :] for t in range(n)]
    for i, j in _sort_exchanges(0, n - 1):
        v[i], v[j] = jnp.maximum(v[i], v[j]), jnp.minimum(v[i], v[j])
    for k in range(PEER_TOPK):
        m = jnp.max(v[0], axis=0, keepdims=True)
        rows_ref[k:k + 1, :] = m
        pop = v[0] == m
        live = min(n, PEER_TOPK - k) - 1
        for r in range(live):
            v[r] = jnp.where(pop, v[r + 1] if r + 1 < n else NEG_INF, v[r])


def _peer_score_body(x_ref, wqt_ref, k0_ref, k1_ref, s1_ref, e1_ref, s2_ref, e2_ref,
                     qt_scr, v1_scr, v2_scr, cand_scr):
    half = PEER_DKEY // 2
    qt_scr[...] = lax.dot_general(wqt_ref[...], x_ref[...], _NT, preferred_element_type=F32)

    def strip(h, tg):
        s1 = s1_ref[h, tg]
        s2 = s2_ref[h, tg]
        _top_values(s1, v1_scr)
        _top_values(s2, v2_scr)
        r = 0
        for b in range(PEER_TOPK):
            for a in range(PEER_TOPK // (b + 1)):
                cand_scr[r:r + 1, :] = v1_scr[a:a + 1, :] + v2_scr[b:b + 1, :]
                r += 1
        cand_scr[r:, :] = jnp.full((cand_scr.shape[0] - r, LANE), NEG_INF, F32)
        cand = [cand_scr[t0:t0 + SUBLANE, :] for t0 in range(0, cand_scr.shape[0], SUBLANE)]
        prev = None
        cnt = jnp.zeros((1, LANE), F32)
        tau = jnp.full((1, LANE), NEG_INF, F32)
        for _ in range(PEER_TOPK):
            best = None
            for c in cand:
                cur = c if prev is None else jnp.where(c < prev, c, NEG_INF)
                best = cur if best is None else jnp.maximum(best, cur)
            m = jnp.max(best, axis=0, keepdims=True)
            hits = None
            for c in cand:
                eq = jnp.where(c == m, 1.0, 0.0)
                hits = eq if hits is None else hits + eq
            tau = jnp.where(cnt < PEER_TOPK, m, tau)
            cnt = cnt + jnp.sum(hits, axis=0, keepdims=True)
            prev = m
        m1 = v1_scr[0:1, :]
        m2 = v2_scr[0:1, :]
        zs = None
        for c in cand:
            z = jnp.where(c >= tau, jnp.exp(c - (m1 + m2)), 0.0)
            zs = z if zs is None else zs + z
        z_tot = jnp.sum(zs, axis=0, keepdims=True)
        e1_ref[h, tg] = jnp.exp(s1 - m1) / z_tot
        e2_ref[h, tg] = jnp.exp(s2 - m2)
        c1 = jnp.full(s1.shape, jnp.inf, F32)
        for b in range(PEER_TOPK):
            v2b = v2_scr[b:b + 1, :]
            c1 = jnp.where(s1 + v2b >= tau, v2b, c1)
        s1_ref[h, tg] = c1

    def head(h, carry):
        base = pl.multiple_of(h * PEER_DKEY, PEER_DKEY)
        s1 = jnp.dot(k0_ref[...], qt_scr[pl.ds(base, half), :].astype(BF16), preferred_element_type=F32)
        s2 = jnp.dot(k1_ref[...], qt_scr[pl.ds(base + half, half), :].astype(BF16), preferred_element_type=F32)
        for tg in range(x_ref.shape[0] // LANE):
            s1_ref[h, tg] = s1[:, tg * LANE:(tg + 1) * LANE]
            s2_ref[h, tg] = s2[:, tg * LANE:(tg + 1) * LANE]
        for tg in range(x_ref.shape[0] // LANE):
            strip(h, tg)
        return carry

    lax.fori_loop(0, PEER_HEADS, head, 0)


PEER_JQ = 4


def _peer_expert_body(x_ref, u_ref, vt_ref, c1_ref, e1_ref, s2_ref, e2_ref, o_ref,
                      h_scr, g_scr, act_scr, acc_scr, bc1_scr, be1_scr):
    e = pl.program_id(1)
    n_blocks = pl.num_programs(1) - 1
    tt = x_ref.shape[0]
    n_i = u_ref.shape[0] // PEER_NKEYS
    span = PEER_JQ * SUBLANE

    @pl.when(e == 0)
    def _():
        acc_scr[...] = jnp.zeros_like(acc_scr)
        act_scr[...] = jnp.zeros_like(act_scr)

    n_tg = tt // LANE
    m_rows = acc_scr.shape[0] // n_tg
    i_base = jnp.minimum(e, n_blocks - 1) * n_i
    for i_loc in range(n_i):
        for tg in range(n_tg):
            for h in range(PEER_HEADS):
                bc1_scr[i_loc, tg, h] = jnp.broadcast_to(c1_ref[h, tg, pl.ds(i_base + i_loc, 1), :], (SUBLANE, LANE))
                be1_scr[i_loc, tg, h] = jnp.broadcast_to(e1_ref[h, tg, pl.ds(i_base + i_loc, 1), :], (SUBLANE, LANE))
    h_scr[...] = lax.dot_general(u_ref[...], x_ref[...], _NT, preferred_element_type=F32)

    def token_strip(tg, carry):
        m0 = pl.multiple_of(tg * m_rows, m_rows)
        acc_scr[pl.ds(m0, m_rows), :] += jnp.dot(vt_ref[pl.ds(m0, m_rows), :], act_scr[...],
                                                  preferred_element_type=F32)
        for j0 in range(0, PEER_NKEYS, span):
            g = [[None] * PEER_JQ for _ in range(n_i)]
            for h in range(PEER_HEADS):
                cs = [bc1_scr[i_loc, tg, h] for i_loc in range(n_i)]
                bs = [be1_scr[i_loc, tg, h] for i_loc in range(n_i)]
                for k in range(PEER_JQ):
                    rows = slice(j0 + k * SUBLANE, j0 + (k + 1) * SUBLANE)
                    s2t = s2_ref[h, tg, rows, :]
                    e2t = e2_ref[h, tg, rows, :]
                    for i_loc in range(n_i):
                        gh = jnp.where(s2t >= cs[i_loc], bs[i_loc] * e2t, 0.0)
                        g[i_loc][k] = gh if g[i_loc][k] is None else g[i_loc][k] + gh
            for i_loc in range(n_i):
                for k in range(PEER_JQ):
                    r0 = i_loc * PEER_NKEYS + j0 + k * SUBLANE
                    g_scr[tg, r0:r0 + SUBLANE, :] = g[i_loc][k]
        return carry

    lax.fori_loop(0, n_tg, token_strip, 0)
    for tg in range(n_tg):
        hv = h_scr[:, tg * LANE:(tg + 1) * LANE]
        act_scr[:, tg * LANE:(tg + 1) * LANE] = (
            0.5 * hv * (1.0 + lax.erf(hv * (2.0 ** -0.5))) * g_scr[tg]).astype(BF16)

    @pl.when(e == n_blocks)
    def _():
        o_ref[...] = acc_scr[...].T


def peer_dense(xm, wqt, k0, k1, u_bf, vt_bf, tt=PEER_TT, eb=PEER_EB):
    t, d = xm.shape
    n_blocks = u_bf.shape[0] // eb
    xb = xm.astype(BF16)
    hk = (PEER_HEADS, t // LANE, PEER_NKEYS, LANE)
    sblk = pl.BlockSpec((PEER_HEADS, tt // LANE, PEER_NKEYS, LANE), lambda i: (0, i, 0, 0))
    c1, e1, s2, e2 = pl.pallas_call(
        _peer_score_body,
        grid=(t // tt,),
        in_specs=[pl.BlockSpec((tt, d), lambda i: (i, 0)),
                  pl.BlockSpec(wqt.shape, lambda i: (0, 0)),
                  pl.BlockSpec(k0.shape, lambda i: (0, 0)),
                  pl.BlockSpec(k1.shape, lambda i: (0, 0))],
        out_specs=[sblk, sblk, sblk, sblk],
        out_shape=[jax.ShapeDtypeStruct(hk, F32)] * 4,
        scratch_shapes=[pltpu.VMEM((PEER_HEADS * PEER_DKEY, tt), F32),
                        pltpu.VMEM((PEER_TOPK, LANE), F32), pltpu.VMEM((PEER_TOPK, LANE), F32),
                        pltpu.VMEM((-(-PEER_NCAND // SUBLANE) * SUBLANE, LANE), F32)],
        compiler_params=pltpu.CompilerParams(vmem_limit_bytes=48 * 2 ** 20),
        name='peer_score',
    )(xb, wqt, k0, k1)
    sblk2 = pl.BlockSpec((PEER_HEADS, tt // LANE, PEER_NKEYS, LANE), lambda i, j: (0, i, 0, 0))
    return pl.pallas_call(
        _peer_expert_body,
        grid=(t // tt, n_blocks + 1),
        in_specs=[pl.BlockSpec((tt, d), lambda i, j: (i, 0)),
                  pl.BlockSpec((eb, d), lambda i, j: (jnp.minimum(j, n_blocks - 1), 0)),
                  pl.BlockSpec((d, eb), lambda i, j: (0, jnp.maximum(j - 1, 0))),
                  sblk2, sblk2, sblk2, sblk2],
        out_specs=pl.BlockSpec((tt, d), lambda i, j: (i, 0)),
        out_shape=jax.ShapeDtypeStruct((t, d), F32),
        scratch_shapes=[pltpu.VMEM((eb, tt), F32), pltpu.VMEM((tt // LANE, eb, LANE), F32), pltpu.VMEM((eb, tt), BF16),
                        pltpu.VMEM((d, tt), F32)]
        + [pltpu.VMEM((eb // PEER_NKEYS, tt // LANE, PEER_HEADS, SUBLANE, LANE), F32)] * 2,
        compiler_params=pltpu.CompilerParams(vmem_limit_bytes=48 * 2 ** 20,
                                             dimension_semantics=("arbitrary", "arbitrary")),
        name='peer_expert',
    )(xb, u_bf, vt_bf, c1, e1, s2, e2)


SCAN_TL = 256
CTX_LEN = 256
_HI = lax.Precision.HIGHEST
_TN = (((0,), (0,)), ((), ()))


def _scan_body(qf_ref, kf_ref, vf_ref, lgf_ref, ltf_ref, qb_ref, kb_ref, vb_ref, lgb_ref, ltb_ref,
               of_ref, ob_ref, sf_scr, sb_scr, *, chunk, heads, per_dim):
    tl, kd = qf_ref.shape[1], qf_ref.shape[2]
    vd = vf_ref.shape[2]
    dk, dv = kd // heads, vd // heads

    @pl.when(pl.program_id(1) == 0)
    def _():
        sf_scr[...] = jnp.zeros_like(sf_scr)
        sb_scr[...] = jnp.zeros_like(sb_scr)

    ri = lax.broadcasted_iota(jnp.int32, (chunk, chunk), 0)
    ci = lax.broadcasted_iota(jnp.int32, (chunk, chunk), 1)
    khead = lax.broadcasted_iota(jnp.int32, (1, kd), 1) // dk
    vhead = lax.broadcasted_iota(jnp.int32, (1, vd), 1) // dv
    same_head = (lax.broadcasted_iota(jnp.int32, (vd, kd), 0) // dv
                 == lax.broadcasted_iota(jnp.int32, (vd, kd), 1) // dk)

    def per_key(x8):
        out = jnp.zeros((x8.shape[0], kd), F32)
        for h in range(heads):
            out = jnp.where(khead == h, x8[:, h:h + 1], out)
        return out

    def sweep(q_ref, k_ref, v_ref, lg_ref, lt_ref, o_ref, s_scr, fwd):
        keep = (ri >= ci) if fwd else (ri <= ci)
        tri = keep.astype(F32)
        order = range(tl // chunk) if fwd else reversed(range(tl // chunk))
        for c in order:
            rows = slice(c * chunk, (c + 1) * chunk)
            q, k, v = q_ref[0, rows, :], k_ref[0, rows, :], v_ref[0, rows, :]
            cum = jnp.dot(tri, lg_ref[0, rows, :], precision=_HI, preferred_element_type=F32)
            tot = cum[chunk - 1:chunk, :] if fwd else cum[0:1, :]
            if per_dim:
                e_in, e_out, e_tot = jnp.exp(cum), jnp.exp(tot - cum), jnp.exp(tot)
                mid = cum[chunk // 2:chunk // 2 + 1, :]
                q_att = (q * jnp.exp(cum - mid))
                k_att = (k * jnp.exp(mid - cum)).astype(BF16)
            else:
                e_in, e_out, e_tot = per_key(jnp.exp(cum)), per_key(jnp.exp(tot - cum)), per_key(jnp.exp(tot))
                cum_t = lax.dot_general(lt_ref[0, :, rows], tri, _NT, precision=_HI, preferred_element_type=F32)
                q_att, k_att = q, k.astype(BF16)
            s = s_scr[...]
            o = lax.dot_general((q * e_in).astype(BF16), s.astype(BF16), _NT, preferred_element_type=F32)
            for h in range(heads):
                sc = lax.dot_general(jnp.where(khead == h, q_att, 0.0).astype(BF16), k_att, _NT,
                                     preferred_element_type=F32)
                if per_dim:
                    att = jnp.where(keep, sc, 0.0)
                else:
                    att = sc * jnp.exp(jnp.where(keep, cum[:, h:h + 1] - cum_t[h:h + 1, :], NEG_INF))
                o = o + jnp.dot(att.astype(BF16), jnp.where(vhead == h, v, 0.0).astype(BF16),
                                preferred_element_type=F32)
            o_ref[0, rows, :] = o
            kv = lax.dot_general(v.astype(BF16), (k * e_out).astype(BF16), _TN, preferred_element_type=F32)
            s_scr[...] = s * e_tot + jnp.where(same_head, kv, 0.0)

    sweep(qf_ref, kf_ref, vf_ref, lgf_ref, ltf_ref, of_ref, sf_scr, True)
    sweep(qb_ref, kb_ref, vb_ref, lgb_ref, ltb_ref, ob_ref, sb_scr, False)


def scan_bidir(q, k_f, k_b, v, lg_f, lg_b, lt_f, lt_b, *, heads, per_dim, chunk):
    b, l, kd = q.shape
    vd = v.shape[-1]
    n, n_ctx = l // SCAN_TL, CTX_LEN // SCAN_TL

    def bwd_group(s):
        return jnp.where(s < n_ctx, n_ctx - 1 - s, n - 1 - (s - n_ctx))

    def specs(group):
        tok = lambda w: pl.BlockSpec((1, SCAN_TL, w), lambda bi, s: (bi, group(s), 0))
        return [tok(kd), tok(kd), tok(vd), tok(LANE),
                pl.BlockSpec((1, SUBLANE, SCAN_TL), lambda bi, s: (bi, 0, group(s)))]

    fwd_group = lambda s: s
    o_f, o_b = pl.pallas_call(
        functools.partial(_scan_body, chunk=chunk, heads=heads, per_dim=per_dim),
        grid=(b, n),
        in_specs=specs(fwd_group) + specs(bwd_group),
        out_specs=[pl.BlockSpec((1, SCAN_TL, vd), lambda bi, s: (bi, s, 0)),
                   pl.BlockSpec((1, SCAN_TL, vd), lambda bi, s: (bi, bwd_group(s), 0))],
        out_shape=[jax.ShapeDtypeStruct((b, l, vd), F32)] * 2,
        scratch_shapes=[pltpu.VMEM((vd, kd), F32)] * 2,
        compiler_params=pltpu.CompilerParams(dimension_semantics=("arbitrary", "arbitrary")),
        name='scan_bidir',
    )(q, k_f, v, lg_f, lt_f, q, k_b, v, lg_b, lt_b)
    return o_f + o_b


def _ctx_identity(tab):
    cos, sin = tab
    pad = (CTX_LEN, cos.shape[1])
    return jnp.concatenate([jnp.ones(pad, F32), cos], 0), jnp.concatenate([jnp.zeros(pad, F32), sin], 0)


def _pad_lanes(t):
    return jnp.pad(t, ((0, 0), (0, 0), (0, LANE - t.shape[-1])))


def _head_rows(t):
    return jnp.pad(t.transpose(0, 2, 1), ((0, 0), (0, SUBLANE - t.shape[-1]), (0, 0)))


def _split_heads(t, h):
    return t.reshape(t.shape[:-1] + (h, t.shape[-1] // h))


def _group_norm(o, heads, centre):
    o4 = _split_heads(o, heads)
    return head_norm(o4, centre).reshape(o.shape)


def _rotate_heads(t, heads, tab):
    cos, sin = tab
    return rotate(_split_heads(t, heads), (cos[:, None, :], sin[:, None, :])).reshape(t.shape)


def _segment_conv(x, w, b):
    def conv(t):
        z = jnp.zeros_like(t[:, :1])
        return (jnp.concatenate([z, t[:, :-1]], 1) * w[0] + t * w[1] + jnp.concatenate([t[:, 1:], z], 1) * w[2]) + b

    return jnp.concatenate([conv(x[:, :CTX_LEN]), conv(x[:, CTX_LEN:])], 1)


def gla_tokens(p, w_gf, b_gf, w_gb, b_gb, g_out):
    q, k, v, r, lr_f, lr_b = p
    no_rows = jnp.zeros((q.shape[0], SUBLANE, q.shape[1]), F32)

    def log_gate(lr, w, b):
        return jax.nn.log_sigmoid(_mm(lr, w) + b) / GLA_GATE_NORM

    o = scan_bidir(q * GLA_DK ** -0.5, k, k, v, log_gate(lr_f, w_gf, b_gf), log_gate(lr_b, w_gb, b_gb),
                   no_rows, no_rows, heads=GLA_HEADS, per_dim=True, chunk=CHUNK)
    return _group_norm(o, GLA_HEADS, False) * g_out * jax.nn.silu(r)


def ssd_tokens(p, conv_w, conv_b, dtb_f, dtb_b, alog_f, alog_b, d_skip, g_out):
    z, xbc, dt_f, dt_b = p
    xbc = jax.nn.silu(_segment_conv(xbc, conv_w, conv_b))
    xs, bm, cm = split_cols(xbc, (SSD_HEADS * SSD_HEADDIM, SSD_BC, SSD_BC))
    rep = SSD_HEADS // SSD_GROUPS

    def per_head(t):
        return jnp.repeat(_split_heads(t, SSD_GROUPS), rep, axis=2).reshape(t.shape[:-1] + (SSD_HEADS * SSD_STATE,))

    b_all = per_head(bm)

    def direction(dt_raw, dt_bias, a_log):
        dt = jax.nn.softplus(dt_raw + dt_bias)
        return b_all * jnp.repeat(dt, SSD_STATE, axis=-1), dt * -jnp.exp(a_log)

    k_f, la_f = direction(dt_f, dtb_f, alog_f)
    k_b, la_b = direction(dt_b, dtb_b, alog_b)
    o = scan_bidir(per_head(cm), k_f, k_b, xs, _pad_lanes(la_f), _pad_lanes(la_b), _head_rows(la_f), _head_rows(la_b),
                   heads=SSD_HEADS, per_dim=False, chunk=SCAN_TL)
    y = o + jnp.repeat(d_skip, SSD_HEADDIM) * xs
    return rms_norm(y * jax.nn.silu(z), g_out)


def ret_tokens(p, rope):
    q, k, v, g = p
    q = _rotate_heads(q, RET_HEADS, rope)
    k = _rotate_heads(k * RET_DK ** -0.5, RET_HEADS, rope)
    la = jnp.broadcast_to(ret_log_decay(), q.shape[:2] + (RET_HEADS,))
    o = scan_bidir(q, k, k, v, _pad_lanes(la), _pad_lanes(la), _head_rows(la), _head_rows(la),
                   heads=RET_HEADS, per_dim=False, chunk=SCAN_TL)
    return _group_norm(o, RET_HEADS, True) * jax.nn.silu(g)


def mla_tokens(p, g_q, w_uq, g_kv, w_uk, w_uv, axial):
    c_q, c_kv, k_r = p
    kv = rms_norm(c_kv, g_kv)
    k_nope = to_heads(_mm(kv, w_uk), MLA_HEADS)
    v = to_heads(_mm(kv, w_uv), MLA_HEADS)
    k_rope = axial_rope(k_r, axial)
    k = jnp.concatenate([k_nope, jnp.broadcast_to(k_rope[:, None], k_nope.shape[:3] + (MLA_ROPE,))], -1)
    q = to_heads(_mm(rms_norm(c_q, g_q), w_uq), MLA_HEADS)
    q = jnp.concatenate([q[..., :MLA_NOPE], axial_rope(q[..., MLA_NOPE:], axial)], -1)
    return attention_tokens(q.astype(BF16), k.astype(BF16), v.astype(BF16))


ATTN_TQ = 256


def _attn_body(q_ref, k_ref, v_ref, o_ref):
    heads, l, dv = v_ref.shape[1], v_ref.shape[2], v_ref.shape[3]
    col = lax.broadcasted_iota(jnp.int32, (1, l), 1)
    visible = (col < CTX_LEN) | (pl.program_id(1) > 0)
    scale = q_ref.shape[-1] ** -0.5
    for h in range(heads):
        s = lax.dot_general(q_ref[0, h], k_ref[0, h], _NT, preferred_element_type=F32) * scale
        s = jnp.where(visible, s, NEG_INF)
        p = jnp.exp(s - jnp.max(s, axis=1, keepdims=True))
        o = jnp.dot(p.astype(BF16), v_ref[0, h], preferred_element_type=F32)
        o_ref[0, :, h * dv:(h + 1) * dv] = o / jnp.sum(p, axis=1, keepdims=True)


def attention_tokens(q, k, v):
    b, h, l, d = q.shape
    dv = v.shape[-1]
    return pl.pallas_call(
        _attn_body,
        grid=(b, l // ATTN_TQ),
        in_specs=[pl.BlockSpec((1, h, ATTN_TQ, d), lambda bi, qi: (bi, 0, qi, 0)),
                  pl.BlockSpec((1, h, l, d), lambda bi, qi: (bi, 0, 0, 0)),
                  pl.BlockSpec((1, h, l, dv), lambda bi, qi: (bi, 0, 0, 0))],
        out_specs=pl.BlockSpec((1, ATTN_TQ, h * dv), lambda bi, qi: (bi, qi, 0)),
        out_shape=jax.ShapeDtypeStruct((b, l, h * dv), F32),
        compiler_params=pltpu.CompilerParams(vmem_limit_bytes=48 * 2 ** 20),
        name='mla_attention',
    )(q, k, v)


def kernel(x, c, ctx, c_ctx, w_ada, b_ada, w_in, mla_g_q, mla_w_uq, mla_g_kv, mla_w_uk, mla_w_uv, gla_w_gf, gla_b_gf, gla_w_gb, gla_b_gb, gla_g, ssd_conv_w, ssd_conv_b, ssd_dt_bias_f, ssd_dt_bias_b, ssd_a_log_f, ssd_a_log_b, ssd_d, ssd_g, w_out, ln1_g, ln1_b, peer_w_q, peer_keys, peer_u, peer_v, ln2_g, ln2_b):
    seq = x.shape[1]
    rows = seq // GRID_W
    row = jnp.repeat(jnp.arange(rows), GRID_W)
    col = jnp.tile(jnp.arange(GRID_W), rows)
    axial = (_ctx_identity(rope_tables(row, MLA_ROPE // 2)), _ctx_identity(rope_tables(col, MLA_ROPE // 2)))
    ret_rope = _ctx_identity(rope_tables(jnp.arange(seq), RET_DK))
    s_lat = jax.nn.silu(c)
    s_ctx = jax.nn.silu(c_ctx)
    xc = ctx
    for i in range(DEPTH):
        need_ctx = i < DEPTH - 1
        mod = [t[:, None, :] for t in jnp.split(s_lat @ w_ada[i] + b_ada[i], 6, axis=-1)]
        mod_c = jnp.split(s_ctx @ w_ada[i] + b_ada[i], 6, axis=-1)
        xin = jnp.concatenate([modulate(xc, mod_c[0], mod_c[1]), modulate(x, mod[0], mod[1])], 1)
        p_mla, p_gla, p_ssd, p_ret = partition(split_cols(_mm(xin, w_in[i]), IN_SIZES))
        outs = (
            mla_tokens(p_mla, mla_g_q[i], mla_w_uq[i], mla_g_kv[i], mla_w_uk[i], mla_w_uv[i], axial),
            gla_tokens(p_gla, gla_w_gf[i], gla_b_gf[i], gla_w_gb[i], gla_b_gb[i], gla_g[i]),
            ssd_tokens(p_ssd, ssd_conv_w[i], ssd_conv_b[i], ssd_dt_bias_f[i], ssd_dt_bias_b[i],
                       ssd_a_log_f[i], ssd_a_log_b[i], ssd_d[i], ssd_g[i]),
            ret_tokens(p_ret, ret_rope),
        )
        mixed = jnp.concatenate(outs, -1)
        y = _mm(mixed[:, CTX_LEN:], w_out[i])
        x = layer_norm(DEEPNORM_ALPHA * x + mod[2] * y, ln1_g[i], ln1_b[i])
        toks = [modulate(x, mod[3], mod[4]).reshape(-1, D_MODEL)]
        if need_ctx:
            yc = _mm(mixed[:, :CTX_LEN], w_out[i])
            xc = layer_norm(DEEPNORM_ALPHA * xc + mod_c[2] * yc, ln1_g[i], ln1_b[i])
            toks.append(modulate(xc, mod_c[3], mod_c[4]).reshape(-1, D_MODEL))
        n_lat = toks[0].shape[0]
        f_all = peer_dense(jnp.concatenate(toks, 0), peer_w_q[i].T.astype(BF16), peer_keys[i, 0].astype(BF16),
                           peer_keys[i, 1].astype(BF16), peer_u[i].astype(BF16), peer_v[i].T.astype(BF16))
        x = layer_norm(DEEPNORM_ALPHA * x + mod[5] * f_all[:n_lat].reshape(x.shape), ln2_g[i], ln2_b[i])
        if need_ctx:
            xc = layer_norm(DEEPNORM_ALPHA * xc + mod_c[5] * f_all[n_lat:].reshape(xc.shape), ln2_g[i], ln2_b[i])
    return x
```

```python
import functools
import math

import jax
import jax.numpy as jnp
import numpy as np
from jax import lax
from jax.experimental import pallas as pl
from jax.experimental.pallas import tpu as pltpu

D_MODEL = 1024
DEPTH = 4
GRID_W = 64
CHUNK = 64
Q_BLOCK = 128
ROPE_BASE = 10000.0
MLA_HEADS, MLA_NOPE, MLA_ROPE, MLA_V, MLA_Q_LORA, MLA_KV_LORA = 4, 64, 32, 64, 256, 128
GLA_HEADS, GLA_DK, GLA_DV, GLA_GATE_RANK, GLA_GATE_NORM = 4, 32, 64, 16, 16.0
SSD_HEADS, SSD_HEADDIM, SSD_GROUPS, SSD_STATE = 4, 64, 2, 128
RET_HEADS, RET_DK, RET_DV = 4, 32, 64
PEER_HEADS, PEER_NKEYS, PEER_DKEY, PEER_TOPK, PEER_TOKEN_BLOCK = 8, 128, 256, 16, 128
DEEPNORM_ALPHA = (2 * DEPTH) ** 0.25

SSD_BC = SSD_GROUPS * SSD_STATE
SSD_XBC = SSD_HEADS * SSD_HEADDIM + 2 * SSD_BC
MLA_SIZES = (MLA_Q_LORA, MLA_KV_LORA, MLA_ROPE)
GLA_SIZES = (GLA_HEADS * GLA_DK, GLA_HEADS * GLA_DK, GLA_HEADS * GLA_DV, GLA_HEADS * GLA_DV, GLA_GATE_RANK, GLA_GATE_RANK)
SSD_SIZES = (SSD_HEADS * SSD_HEADDIM, SSD_XBC, SSD_HEADS, SSD_HEADS)
RET_SIZES = (RET_HEADS * RET_DK, RET_HEADS * RET_DK, RET_HEADS * RET_DV, RET_HEADS * RET_DV)
IN_SIZES = MLA_SIZES + GLA_SIZES + SSD_SIZES + RET_SIZES

F32 = jnp.float32
BF16 = jnp.bfloat16
LANE = 128


def _mm_body(a_ref, b_ref, o_ref):
    o_ref[...] = jnp.dot(a_ref[...].astype(BF16), b_ref[...].astype(BF16), preferred_element_type=F32)


def _mm(a, b, tm=512, tn=512):
    lead = a.shape[:-1]
    k = a.shape[-1]
    n = b.shape[-1]
    a2 = a.reshape(-1, k)
    m = a2.shape[0]
    tm = math.gcd(tm, m)
    n_pad = -(-n // LANE) * LANE
    tn = max(t for t in range(LANE, tn + 1, LANE) if n_pad % t == 0)
    if n_pad != n:
        b = jnp.pad(b, ((0, 0), (0, n_pad - n)))
    out = pl.pallas_call(
        _mm_body,
        grid=(m // tm, n_pad // tn),
        in_specs=[pl.BlockSpec((tm, k), lambda i, j: (i, 0)), pl.BlockSpec((k, tn), lambda i, j: (0, j))],
        out_specs=pl.BlockSpec((tm, tn), lambda i, j: (i, j)),
        out_shape=jax.ShapeDtypeStruct((m, n_pad), F32),
    )(a2, b)
    return out[:, :n].reshape(lead + (n,))


def layer_norm(x, g, b, eps=1e-5):
    mu = x.mean(-1, keepdims=True)
    xc = x - mu
    var = jnp.mean(xc * xc, -1, keepdims=True)
    return xc * lax.rsqrt(var + eps) * g + b


def rms_norm(x, g, eps=1e-6):
    return x * lax.rsqrt(jnp.mean(x * x, -1, keepdims=True) + eps) * g


def head_norm(o, centre):
    if centre:
        o = o - o.mean(-1, keepdims=True)
    return o * lax.rsqrt(jnp.mean(o * o, -1, keepdims=True) + 1e-6)


def modulate(x, shift, scale):
    return x * (1 + scale) + shift


def split_cols(p, sizes):
    return jnp.split(p, np.cumsum(sizes)[:-1].tolist(), axis=-1)


def partition(pieces):
    a = len(MLA_SIZES)
    b = a + len(GLA_SIZES)
    s = b + len(SSD_SIZES)
    return pieces[:a], pieces[a:b], pieces[b:s], pieces[s:]


def to_heads(t, h):
    b, l, _ = t.shape
    return t.reshape(b, l, h, -1).transpose(0, 2, 1, 3)


def merge_heads(t):
    b, h, l, d = t.shape
    return t.transpose(0, 2, 1, 3).reshape(b, l, h * d)


def rope_tables(pos, dim):
    half = dim // 2
    freqs = ROPE_BASE ** (-jnp.arange(half, dtype=F32) / half)
    ang = pos.astype(F32)[:, None] * freqs
    return jnp.cos(ang), jnp.sin(ang)


def rotate(x, tab):
    cos, sin = tab
    h = x.shape[-1] // 2
    x1, x2 = x[..., :h], x[..., h:]
    return jnp.concatenate([x1 * cos - x2 * sin, x1 * sin + x2 * cos], -1)


def axial_rope(x, axial):
    h = x.shape[-1] // 2
    return jnp.concatenate([rotate(x[..., :h], axial[0]), rotate(x[..., h:], axial[1])], -1)


def attention(q, k, v):
    s = jnp.einsum('bhqd,bhkd->bhqk', q, k) * (q.shape[-1] ** -0.5)
    return jnp.einsum('bhqk,bhkv->bhqv', jax.nn.softmax(s, axis=-1), v)


def blocked_attention(q, k, v):
    b, h, l, d = q.shape
    nb = l // Q_BLOCK
    qb = q.reshape(b, h, nb, Q_BLOCK, d).transpose(2, 0, 1, 3, 4)
    ob = lax.map(lambda qq: attention(qq, k, v), qb)
    return ob.transpose(1, 2, 0, 3, 4).reshape(b, h, l, -1)


def chunk_scan(q, k, v, lg, s0):
    b, h, l, _ = k.shape
    n = l // CHUNK
    mask = jnp.tril(jnp.ones((CHUNK, CHUNK), bool))[:, :, None]

    def chunks(t):
        return t.reshape(b, h, n, CHUNK, t.shape[-1]).transpose(2, 0, 1, 3, 4)

    def step(s, inp):
        qc, kc, vc, gc = inp
        cum = jnp.cumsum(gc, axis=2)
        last = cum[:, :, -1:]
        inter = jnp.einsum('bhik,bhkv->bhiv', qc * jnp.exp(cum), s)
        decay = jnp.exp(jnp.where(mask, cum[:, :, :, None] - cum[:, :, None], -jnp.inf))
        if gc.shape[-1] == 1:
            att = jnp.einsum('bhik,bhjk->bhij', qc, kc) * decay[..., 0]
        else:
            att = jnp.einsum('bhik,bhjk,bhijk->bhij', qc, kc, decay)
        o = inter + jnp.einsum('bhij,bhjv->bhiv', att, vc)
        s = jnp.exp(last[:, :, 0])[..., None] * s + jnp.einsum('bhjk,bhjv->bhkv', kc * jnp.exp(last - cum), vc)
        return s, o

    s_fin, o = lax.scan(step, s0, (chunks(q), chunks(k), chunks(v), chunks(lg)))
    return o.transpose(1, 2, 0, 3, 4).reshape(b, h, l, -1), s_fin


def final_state(k, v, lg):
    cum = jnp.cumsum(lg, axis=2)
    return jnp.einsum('bhjk,bhjv->bhkv', k * jnp.exp(cum[:, :, -1:] - cum), v)


def bidir_scan(lat_dirs, ctx_dirs, need_ctx):
    out_lat, out_ctx = [], []
    for d, (lat, ctxd) in enumerate(zip(lat_dirs, ctx_dirs)):
        flip = (lambda t: jnp.flip(t, axis=2)) if d == 1 else (lambda t: t)
        qc, kc, vc, gc = (None if t is None else flip(t) for t in ctxd)
        if need_ctx:
            s0 = jnp.zeros(kc.shape[:2] + (kc.shape[-1], vc.shape[-1]), vc.dtype)
            oc, state = chunk_scan(qc, kc, vc, gc, s0)
            out_ctx.append(flip(oc))
        else:
            state = final_state(kc, vc, gc)
        ql, kl, vl, gl = (flip(t) for t in lat)
        ol, _ = chunk_scan(ql, kl, vl, gl, state)
        out_lat.append(flip(ol))
    return out_lat[0] + out_lat[1], (out_ctx[0] + out_ctx[1] if need_ctx else None)


def mla_q(c_q, g_q, w_uq, axial):
    q = to_heads(rms_norm(c_q, g_q) @ w_uq, MLA_HEADS)
    if axial is None:
        return q
    return jnp.concatenate([q[..., :MLA_NOPE], axial_rope(q[..., MLA_NOPE:], axial)], -1)


def mla_kv(c_kv, k_rope, g_kv, w_uk, w_uv, axial):
    kv = rms_norm(c_kv, g_kv)
    k_nope = to_heads(kv @ w_uk, MLA_HEADS)
    v = to_heads(kv @ w_uv, MLA_HEADS)
    if axial is not None:
        k_rope = axial_rope(k_rope, axial)
    k_rope = jnp.broadcast_to(k_rope[:, None], k_nope.shape[:3] + (MLA_ROPE,))
    return jnp.concatenate([k_nope, k_rope], -1), v


def mla_mixer(pl_, pc, g_q, w_uq, g_kv, w_uk, w_uv, axial, need_ctx):
    cq_l, ckv_l, kr_l = pl_
    cq_c, ckv_c, kr_c = pc
    k_c, v_c = mla_kv(ckv_c, kr_c, g_kv, w_uk, w_uv, None)
    k_l, v_l = mla_kv(ckv_l, kr_l, g_kv, w_uk, w_uv, axial)
    q_l = mla_q(cq_l, g_q, w_uq, axial)
    o_l = blocked_attention(q_l, jnp.concatenate([k_c, k_l], 2), jnp.concatenate([v_c, v_l], 2))
    o_c = merge_heads(attention(mla_q(cq_c, g_q, w_uq, None), k_c, v_c)) if need_ctx else None
    return merge_heads(o_l), o_c


def gla_prep(p, w_gf, b_gf, w_gb, b_gb, with_q):
    q, k, v, r, lr_f, lr_b = p
    q = (to_heads(q, GLA_HEADS) * GLA_DK ** -0.5) if with_q else None
    k = to_heads(k, GLA_HEADS)
    v = to_heads(v, GLA_HEADS)

    def log_gate(lr, w, b):
        return to_heads(jax.nn.log_sigmoid(lr @ w + b) / GLA_GATE_NORM, GLA_HEADS)

    return (q, k, v, log_gate(lr_f, w_gf, b_gf)), (q, k, v, log_gate(lr_b, w_gb, b_gb)), r


def gla_mixer(pl_, pc, w_gf, b_gf, w_gb, b_gb, g_out, need_ctx):
    lat_f, lat_b, r_l = gla_prep(pl_, w_gf, b_gf, w_gb, b_gb, True)
    ctx_f, ctx_b, r_c = gla_prep(pc, w_gf, b_gf, w_gb, b_gb, need_ctx)
    o_l, o_c = bidir_scan((lat_f, lat_b), (ctx_f, ctx_b), need_ctx)

    def out(o, r):
        return merge_heads(head_norm(o, False)) * g_out * jax.nn.silu(r)

    return out(o_l, r_l), (out(o_c, r_c) if need_ctx else None)


def depthwise_conv(x, w, b):
    pad = w.shape[0] // 2
    y = lax.conv_general_dilated(x, w[:, None, :], (1,), [(pad, pad)],
                                 dimension_numbers=('NWC', 'WIO', 'NWC'), feature_group_count=x.shape[-1])
    return y + b


def ssd_prep(p, conv_w, conv_b, dtb_f, dtb_b, alog_f, alog_b, with_q):
    z, xbc, dt_f, dt_b = p
    xbc = jax.nn.silu(depthwise_conv(xbc, conv_w, conv_b))
    xs, bm, cm = split_cols(xbc, (SSD_HEADS * SSD_HEADDIM, SSD_BC, SSD_BC))
    rep = SSD_HEADS // SSD_GROUPS
    xh = to_heads(xs, SSD_HEADS)
    bh = jnp.repeat(to_heads(bm, SSD_GROUPS), rep, axis=1)
    ch = jnp.repeat(to_heads(cm, SSD_GROUPS), rep, axis=1) if with_q else None

    def direction(dt_raw, dt_bias, a_log):
        dt = jax.nn.softplus(dt_raw + dt_bias).transpose(0, 2, 1)[..., None]
        la = dt * -jnp.exp(a_log)[:, None, None]
        return (ch, bh * dt, xh, la)

    return direction(dt_f, dtb_f, alog_f), direction(dt_b, dtb_b, alog_b), xh, z


def ssd_mixer(pl_, pc, conv_w, conv_b, dtb_f, dtb_b, alog_f, alog_b, d_skip, g_out, need_ctx):
    lat_f, lat_b, x_l, z_l = ssd_prep(pl_, conv_w, conv_b, dtb_f, dtb_b, alog_f, alog_b, True)
    ctx_f, ctx_b, x_c, z_c = ssd_prep(pc, conv_w, conv_b, dtb_f, dtb_b, alog_f, alog_b, need_ctx)
    o_l, o_c = bidir_scan((lat_f, lat_b), (ctx_f, ctx_b), need_ctx)

    def out(o, xh, z):
        y = merge_heads(o + d_skip[:, None, None] * xh)
        return rms_norm(y * jax.nn.silu(z), g_out)

    return out(o_l, x_l, z_l), (out(o_c, x_c, z_c) if need_ctx else None)


def ret_log_decay():
    return jnp.log1p(-jnp.exp2(-5.0 - jnp.arange(RET_HEADS, dtype=F32)))


def ret_prep(p, rope, with_q):
    q, k, v, g = p
    q = to_heads(q, RET_HEADS) if with_q else None
    k = to_heads(k, RET_HEADS) * RET_DK ** -0.5
    if rope is not None:
        q = rotate(q, rope)
        k = rotate(k, rope)
    v = to_heads(v, RET_HEADS)
    la = jnp.broadcast_to(ret_log_decay()[:, None, None], (k.shape[0], RET_HEADS, k.shape[2], 1))
    d = (q, k, v, la)
    return (d, d), g


def ret_mixer(pl_, pc, rope, need_ctx):
    lat, g_l = ret_prep(pl_, rope, True)
    ctx_d, g_c = ret_prep(pc, None, need_ctx)
    o_l, o_c = bidir_scan(lat, ctx_d, need_ctx)

    def out(o, g):
        return merge_heads(head_norm(o, True)) * jax.nn.silu(g)

    return out(o_l, g_l), (out(o_c, g_c) if need_ctx else None)


PEER_TT = 512
PEER_EB = 512
SUBLANE = 8
_NT = (((1,), (1,)), ((), ()))
NEG_INF = float('-inf')
PEER_NCAND = sum(PEER_TOPK // (b + 1) for b in range(PEER_TOPK))


def _merge_exchanges(lo, hi, r):
    step = r * 2
    if step < hi - lo:
        yield from _merge_exchanges(lo, hi, step)
        yield from _merge_exchanges(lo + r, hi, step)
        yield from ((i, i + r) for i in range(lo + r, hi - r, step))
    else:
        yield (lo, lo + r)


def _sort_exchanges(lo, hi):
    if hi > lo:
        mid = lo + (hi - lo) // 2
        yield from _sort_exchanges(lo, mid)
        yield from _sort_exchanges(mid + 1, hi)
        yield from _merge_exchanges(lo, hi, 1)


def _top_values(s, rows_ref):
    n = s.shape[0] // SUBLANE
    v = [s[t * SUBLANE:(t + 1) * SUBLANE, :] for t in range(n)]
    for i, j in _sort_exchanges(0, n - 1):
        v[i], v[j] = jnp.maximum(v[i], v[j]), jnp.minimum(v[i], v[j])
    for k in range(PEER_TOPK):
        m = jnp.max(v[0], axis=0, keepdims=True)
        rows_ref[k:k + 1, :] = m
        pop = v[0] == m
        live = min(n, PEER_TOPK - k) - 1
        for r in range(live):
            v[r] = jnp.where(pop, v[r + 1] if r + 1 < n else NEG_INF, v[r])


def _peer_score_body(x_ref, wqt_ref, k0_ref, k1_ref, s1_ref, e1_ref, s2_ref, e2_ref,
                     qt_scr, v1_scr, v2_scr, cand_scr):
    half = PEER_DKEY // 2
    qt_scr[...] = lax.dot_general(wqt_ref[...], x_ref[...], _NT, preferred_element_type=F32)

    def strip(h, tg):
        s1 = s1_ref[h, tg]
        s2 = s2_ref[h, tg]
        _top_values(s1, v1_scr)
        _top_values(s2, v2_scr)
        r = 0
        for b in range(PEER_TOPK):
            for a in range(PEER_TOPK // (b + 1)):
                cand_scr[r:r + 1, :] = v1_scr[a:a + 1, :] + v2_scr[b:b + 1, :]
                r += 1
        cand_scr[r:, :] = jnp.full((cand_scr.shape[0] - r, LANE), NEG_INF, F32)
        cand = [cand_scr[t0:t0 + SUBLANE, :] for t0 in range(0, cand_scr.shape[0], SUBLANE)]
        prev = None
        cnt = jnp.zeros((1, LANE), F32)
        tau = jnp.full((1, LANE), NEG_INF, F32)
        for _ in range(PEER_TOPK):
            best = None
            for c in cand:
                cur = c if prev is None else jnp.where(c < prev, c, NEG_INF)
                best = cur if best is None else jnp.maximum(best, cur)
            m = jnp.max(best, axis=0, keepdims=True)
            hits = None
            for c in cand:
                eq = jnp.where(c == m, 1.0, 0.0)
                hits = eq if hits is None else hits + eq
            tau = jnp.where(cnt < PEER_TOPK, m, tau)
            cnt = cnt + jnp.sum(hits, axis=0, keepdims=True)
            prev = m
        m1 = v1_scr[0:1, :]
        m2 = v2_scr[0:1, :]
        zs = None
        for c in cand:
            z = jnp.where(c >= tau, jnp.exp(c - (m1 + m2)), 0.0)
            zs = z if zs is None else zs + z
        z_tot = jnp.sum(zs, axis=0, keepdims=True)
        e1_ref[h, tg] = jnp.exp(s1 - m1) / z_tot
        e2_ref[h, tg] = jnp.exp(s2 - m2)
        c1 = jnp.full(s1.shape, jnp.inf, F32)
        for b in range(PEER_TOPK):
            v2b = v2_scr[b:b + 1, :]
            c1 = jnp.where(s1 + v2b >= tau, v2b, c1)
        s1_ref[h, tg] = c1

    def head(h, carry):
        base = pl.multiple_of(h * PEER_DKEY, PEER_DKEY)
        s1 = jnp.dot(k0_ref[...], qt_scr[pl.ds(base, half), :].astype(BF16), preferred_element_type=F32)
        s2 = jnp.dot(k1_ref[...], qt_scr[pl.ds(base + half, half), :].astype(BF16), preferred_element_type=F32)
        for tg in range(x_ref.shape[0] // LANE):
            s1_ref[h, tg] = s1[:, tg * LANE:(tg + 1) * LANE]
            s2_ref[h, tg] = s2[:, tg * LANE:(tg + 1) * LANE]
        for tg in range(x_ref.shape[0] // LANE):
            strip(h, tg)
        return carry

    lax.fori_loop(0, PEER_HEADS, head, 0)


PEER_JQ = 4
PEER_IQ = 4


def _peer_expert_body(x_ref, u_ref, vt_ref, c1_ref, e1_ref, s2_ref, e2_ref, o_ref,
                      h_scr, g_scr, act_scr, acc_scr, bc1_scr, be1_scr):
    e = pl.program_id(1)
    n_blocks = pl.num_programs(1) - 1
    tt = x_ref.shape[0]
    n_i = u_ref.shape[0] // PEER_NKEYS
    span = PEER_JQ * SUBLANE

    @pl.when(e == 0)
    def _():
        acc_scr[...] = jnp.zeros_like(acc_scr)
        act_scr[...] = jnp.zeros_like(act_scr)

    n_tg = tt // LANE
    m_rows = acc_scr.shape[0] // n_tg
    i_base = jnp.minimum(e, n_blocks - 1) * n_i
    for i_loc in range(n_i):
        for tg in range(n_tg):
            for h in range(PEER_HEADS):
                bc1_scr[i_loc, tg, h] = jnp.broadcast_to(c1_ref[h, tg, pl.ds(i_base + i_loc, 1), :], (SUBLANE, LANE))
                be1_scr[i_loc, tg, h] = jnp.broadcast_to(e1_ref[h, tg, pl.ds(i_base + i_loc, 1), :], (SUBLANE, LANE))
    h_scr[...] = lax.dot_general(u_ref[...], x_ref[...], _NT, preferred_element_type=F32)

    def token_strip(tg, carry):
        m0 = pl.multiple_of(tg * m_rows, m_rows)
        acc_scr[pl.ds(m0, m_rows), :] += jnp.dot(vt_ref[pl.ds(m0, m_rows), :], act_scr[...],
                                                  preferred_element_type=F32)
        for i0 in range(0, n_i, PEER_IQ):
            for j0 in range(0, PEER_NKEYS, span):
                g = [[None] * PEER_JQ for _ in range(PEER_IQ)]
                for h in range(PEER_HEADS):
                    cs = [bc1_scr[i0 + ii, tg, h] for ii in range(PEER_IQ)]
                    bs = [be1_scr[i0 + ii, tg, h] for ii in range(PEER_IQ)]
                    for k in range(PEER_JQ):
                        rows = slice(j0 + k * SUBLANE, j0 + (k + 1) * SUBLANE)
                        s2t = s2_ref[h, tg, rows, :]
                        e2t = e2_ref[h, tg, rows, :]
                        for ii in range(PEER_IQ):
                            gh = jnp.where(s2t >= cs[ii], bs[ii] * e2t, 0.0)
                            g[ii][k] = gh if g[ii][k] is None else g[ii][k] + gh
                for ii in range(PEER_IQ):
                    for k in range(PEER_JQ):
                        r0 = (i0 + ii) * PEER_NKEYS + j0 + k * SUBLANE
                        g_scr[tg, r0:r0 + SUBLANE, :] = g[ii][k]
        return carry

    lax.fori_loop(0, n_tg, token_strip, 0)
    for tg in range(n_tg):
        hv = h_scr[:, tg * LANE:(tg + 1) * LANE]
        act_scr[:, tg * LANE:(tg + 1) * LANE] = (
            0.5 * hv * (1.0 + lax.erf(hv * (2.0 ** -0.5))) * g_scr[tg]).astype(BF16)

    @pl.when(e == n_blocks)
    def _():
        o_ref[...] = acc_scr[...].T


def peer_dense(xm, wqt, k0, k1, u_bf, vt_bf, tt=PEER_TT, eb=PEER_EB):
    t, d = xm.shape
    n_blocks = u_bf.shape[0] // eb
    xb = xm.astype(BF16)
    hk = (PEER_HEADS, t // LANE, PEER_NKEYS, LANE)
    sblk = pl.BlockSpec((PEER_HEADS, tt // LANE, PEER_NKEYS, LANE), lambda i: (0, i, 0, 0))
    c1, e1, s2, e2 = pl.pallas_call(
        _peer_score_body,
        grid=(t // tt,),
        in_specs=[pl.BlockSpec((tt, d), lambda i: (i, 0)),
                  pl.BlockSpec(wqt.shape, lambda i: (0, 0)),
                  pl.BlockSpec(k0.shape, lambda i: (0, 0)),
                  pl.BlockSpec(k1.shape, lambda i: (0, 0))],
        out_specs=[sblk, sblk, sblk, sblk],
        out_shape=[jax.ShapeDtypeStruct(hk, F32)] * 4,
        scratch_shapes=[pltpu.VMEM((PEER_HEADS * PEER_DKEY, tt), F32),
                        pltpu.VMEM((PEER_TOPK, LANE), F32), pltpu.VMEM((PEER_TOPK, LANE), F32),
                        pltpu.VMEM((-(-PEER_NCAND // SUBLANE) * SUBLANE, LANE), F32)],
        compiler_params=pltpu.CompilerParams(vmem_limit_bytes=48 * 2 ** 20),
        name='peer_score',
    )(xb, wqt, k0, k1)
    sblk2 = pl.BlockSpec((PEER_HEADS, tt // LANE, PEER_NKEYS, LANE), lambda i, j: (0, i, 0, 0))
    return pl.pallas_call(
        _peer_expert_body,
        grid=(t // tt, n_blocks + 1),
        in_specs=[pl.BlockSpec((tt, d), lambda i, j: (i, 0)),
                  pl.BlockSpec((eb, d), lambda i, j: (jnp.minimum(j, n_blocks - 1), 0)),
                  pl.BlockSpec((d, eb), lambda i, j: (0, jnp.maximum(j - 1, 0))),
                  sblk2, sblk2, sblk2, sblk2],
        out_specs=pl.BlockSpec((tt, d), lambda i, j: (i, 0)),
        out_shape=jax.ShapeDtypeStruct((t, d), F32),
        scratch_shapes=[pltpu.VMEM((eb, tt), F32), pltpu.VMEM((tt // LANE, eb, LANE), F32), pltpu.VMEM((eb, tt), BF16),
                        pltpu.VMEM((d, tt), F32)]
        + [pltpu.VMEM((eb // PEER_NKEYS, tt // LANE, PEER_HEADS, SUBLANE, LANE), F32)] * 2,
        compiler_params=pltpu.CompilerParams(vmem_limit_bytes=48 * 2 ** 20,
                                             dimension_semantics=("arbitrary", "arbitrary")),
        name='peer_expert',
    )(xb, u_bf, vt_bf, c1, e1, s2, e2)


SCAN_TL = 256
CTX_LEN = 256
_HI = lax.Precision.HIGHEST
_TN = (((0,), (0,)), ((), ()))


def _scan_body(qf_ref, kf_ref, vf_ref, lgf_ref, ltf_ref, qb_ref, kb_ref, vb_ref, lgb_ref, ltb_ref,
               of_ref, ob_ref, sf_scr, sb_scr, *, chunk, heads, per_dim):
    tl, kd = qf_ref.shape[1], qf_ref.shape[2]
    vd = vf_ref.shape[2]
    dk, dv = kd // heads, vd // heads

    @pl.when(pl.program_id(1) == 0)
    def _():
        sf_scr[...] = jnp.zeros_like(sf_scr)
        sb_scr[...] = jnp.zeros_like(sb_scr)

    ri = lax.broadcasted_iota(jnp.int32, (chunk, chunk), 0)
    ci = lax.broadcasted_iota(jnp.int32, (chunk, chunk), 1)
    khead = lax.broadcasted_iota(jnp.int32, (1, kd), 1) // dk
    vhead = lax.broadcasted_iota(jnp.int32, (1, vd), 1) // dv
    same_head = (lax.broadcasted_iota(jnp.int32, (vd, kd), 0) // dv
                 == lax.broadcasted_iota(jnp.int32, (vd, kd), 1) // dk)

    def per_key(x8):
        out = jnp.zeros((x8.shape[0], kd), F32)
        for h in range(heads):
            out = jnp.where(khead == h, x8[:, h:h + 1], out)
        return out

    def sweep(q_ref, k_ref, v_ref, lg_ref, lt_ref, o_ref, s_scr, fwd):
        keep = (ri >= ci) if fwd else (ri <= ci)
        tri = keep.astype(F32)
        order = range(tl // chunk) if fwd else reversed(range(tl // chunk))
        for c in order:
            rows = slice(c * chunk, (c + 1) * chunk)
            q, k, v = q_ref[0, rows, :], k_ref[0, rows, :], v_ref[0, rows, :]
            cum = jnp.dot(tri, lg_ref[0, rows, :], precision=_HI, preferred_element_type=F32)
            tot = cum[chunk - 1:chunk, :] if fwd else cum[0:1, :]
            if per_dim:
                e_in, e_out, e_tot = jnp.exp(cum), jnp.exp(tot - cum), jnp.exp(tot)
                mid = cum[chunk // 2:chunk // 2 + 1, :]
                q_att = (q * jnp.exp(cum - mid))
                k_att = (k * jnp.exp(mid - cum)).astype(BF16)
            else:
                e_in, e_out, e_tot = per_key(jnp.exp(cum)), per_key(jnp.exp(tot - cum)), per_key(jnp.exp(tot))
                cum_t = lax.dot_general(lt_ref[0, :, rows], tri, _NT, precision=_HI, preferred_element_type=F32)
                q_att, k_att = q, k.astype(BF16)
            s = s_scr[...]
            o = lax.dot_general((q * e_in).astype(BF16), s.astype(BF16), _NT, preferred_element_type=F32)
            for h in range(heads):
                sc = lax.dot_general(jnp.where(khead == h, q_att, 0.0).astype(BF16), k_att, _NT,
                                     preferred_element_type=F32)
                if per_dim:
                    att = jnp.where(keep, sc, 0.0)
                else:
                    att = sc * jnp.exp(jnp.where(keep, cum[:, h:h + 1] - cum_t[h:h + 1, :], NEG_INF))
                o = o + jnp.dot(att.astype(BF16), jnp.where(vhead == h, v, 0.0).astype(BF16),
                                preferred_element_type=F32)
            o_ref[0, rows, :] = o
            kv = lax.dot_general(v.astype(BF16), (k * e_out).astype(BF16), _TN, preferred_element_type=F32)
            s_scr[...] = s * e_tot + jnp.where(same_head, kv, 0.0)

    sweep(qf_ref, kf_ref, vf_ref, lgf_ref, ltf_ref, of_ref, sf_scr, True)
    sweep(qb_ref, kb_ref, vb_ref, lgb_ref, ltb_ref, ob_ref, sb_scr, False)


def scan_bidir(q, k_f, k_b, v, lg_f, lg_b, lt_f, lt_b, *, heads, per_dim, chunk):
    b, l, kd = q.shape
    vd = v.shape[-1]
    n, n_ctx = l // SCAN_TL, CTX_LEN // SCAN_TL

    def bwd_group(s):
        return jnp.where(s < n_ctx, n_ctx - 1 - s, n - 1 - (s - n_ctx))

    def specs(group):
        tok = lambda w: pl.BlockSpec((1, SCAN_TL, w), lambda bi, s: (bi, group(s), 0))
        return [tok(kd), tok(kd), tok(vd), tok(LANE),
                pl.BlockSpec((1, SUBLANE, SCAN_TL), lambda bi, s: (bi, 0, group(s)))]

    fwd_group = lambda s: s
    o_f, o_b = pl.pallas_call(
        functools.partial(_scan_body, chunk=chunk, heads=heads, per_dim=per_dim),
        grid=(b, n),
        in_specs=specs(fwd_group) + specs(bwd_group),
        out_specs=[pl.BlockSpec((1, SCAN_TL, vd), lambda bi, s: (bi, s, 0)),
                   pl.BlockSpec((1, SCAN_TL, vd), lambda bi, s: (bi, bwd_group(s), 0))],
        out_shape=[jax.ShapeDtypeStruct((b, l, vd), F32)] * 2,
        scratch_shapes=[pltpu.VMEM((vd, kd), F32)] * 2,
        compiler_params=pltpu.CompilerParams(dimension_semantics=("arbitrary", "arbitrary")),
        name='scan_bidir',
    )(q, k_f, v, lg_f, lt_f, q, k_b, v, lg_b, lt_b)
    return o_f + o_b


def _ctx_identity(tab):
    cos, sin = tab
    pad = (CTX_LEN, cos.shape[1])
    return jnp.concatenate([jnp.ones(pad, F32), cos], 0), jnp.concatenate([jnp.zeros(pad, F32), sin], 0)


def _pad_lanes(t):
    return jnp.pad(t, ((0, 0), (0, 0), (0, LANE - t.shape[-1])))


def _head_rows(t):
    return jnp.pad(t.transpose(0, 2, 1), ((0, 0), (0, SUBLANE - t.shape[-1]), (0, 0)))


def _split_heads(t, h):
    return t.reshape(t.shape[:-1] + (h, t.shape[-1] // h))


def _group_norm(o, heads, centre):
    o4 = _split_heads(o, heads)
    return head_norm(o4, centre).reshape(o.shape)


def _rotate_heads(t, heads, tab):
    cos, sin = tab
    return rotate(_split_heads(t, heads), (cos[:, None, :], sin[:, None, :])).reshape(t.shape)


def _segment_conv(x, w, b):
    def conv(t):
        z = jnp.zeros_like(t[:, :1])
        return (jnp.concatenate([z, t[:, :-1]], 1) * w[0] + t * w[1] + jnp.concatenate([t[:, 1:], z], 1) * w[2]) + b

    return jnp.concatenate([conv(x[:, :CTX_LEN]), conv(x[:, CTX_LEN:])], 1)


def gla_tokens(p, w_gf, b_gf, w_gb, b_gb, g_out):
    q, k, v, r, lr_f, lr_b = p
    no_rows = jnp.zeros((q.shape[0], SUBLANE, q.shape[1]), F32)

    def log_gate(lr, w, b):
        return jax.nn.log_sigmoid(_mm(lr, w) + b) / GLA_GATE_NORM

    o = scan_bidir(q * GLA_DK ** -0.5, k, k, v, log_gate(lr_f, w_gf, b_gf), log_gate(lr_b, w_gb, b_gb),
                   no_rows, no_rows, heads=GLA_HEADS, per_dim=True, chunk=CHUNK)
    return _group_norm(o, GLA_HEADS, False) * g_out * jax.nn.silu(r)


def ssd_tokens(p, conv_w, conv_b, dtb_f, dtb_b, alog_f, alog_b, d_skip, g_out):
    z, xbc, dt_f, dt_b = p
    xbc = jax.nn.silu(_segment_conv(xbc, conv_w, conv_b))
    xs, bm, cm = split_cols(xbc, (SSD_HEADS * SSD_HEADDIM, SSD_BC, SSD_BC))
    rep = SSD_HEADS // SSD_GROUPS

    def per_head(t):
        return jnp.repeat(_split_heads(t, SSD_GROUPS), rep, axis=2).reshape(t.shape[:-1] + (SSD_HEADS * SSD_STATE,))

    b_all = per_head(bm)

    def direction(dt_raw, dt_bias, a_log):
        dt = jax.nn.softplus(dt_raw + dt_bias)
        return b_all * jnp.repeat(dt, SSD_STATE, axis=-1), dt * -jnp.exp(a_log)

    k_f, la_f = direction(dt_f, dtb_f, alog_f)
    k_b, la_b = direction(dt_b, dtb_b, alog_b)
    o = scan_bidir(per_head(cm), k_f, k_b, xs, _pad_lanes(la_f), _pad_lanes(la_b), _head_rows(la_f), _head_rows(la_b),
                   heads=SSD_HEADS, per_dim=False, chunk=SCAN_TL)
    y = o + jnp.repeat(d_skip, SSD_HEADDIM) * xs
    return rms_norm(y * jax.nn.silu(z), g_out)


def ret_tokens(p, rope):
    q, k, v, g = p
    q = _rotate_heads(q, RET_HEADS, rope)
    k = _rotate_heads(k * RET_DK ** -0.5, RET_HEADS, rope)
    la = jnp.broadcast_to(ret_log_decay(), q.shape[:2] + (RET_HEADS,))
    o = scan_bidir(q, k, k, v, _pad_lanes(la), _pad_lanes(la), _head_rows(la), _head_rows(la),
                   heads=RET_HEADS, per_dim=False, chunk=SCAN_TL)
    return _group_norm(o, RET_HEADS, True) * jax.nn.silu(g)


def mla_tokens(p, g_q, w_uq, g_kv, w_uk, w_uv, axial):
    c_q, c_kv, k_r = p

    def pad_heads(w):
        w = _split_heads(w, MLA_HEADS)
        return jnp.pad(w, ((0, 0), (0, 0), (0, LANE - w.shape[-1]))).reshape(w.shape[0], MLA_HEADS * LANE)

    kv = rms_norm(c_kv, g_kv)
    kv_all = _mm(kv, jnp.concatenate([pad_heads(w_uk), pad_heads(w_uv)], -1))
    k = _split_heads(kv_all[..., :MLA_HEADS * LANE], MLA_HEADS)
    v = kv_all[..., MLA_HEADS * LANE:]
    k_rope = jnp.broadcast_to(axial_rope(k_r, axial)[:, :, None, :], k.shape[:3] + (MLA_ROPE,))
    k = jnp.concatenate([k[..., :MLA_NOPE], k_rope, k[..., MLA_NOPE + MLA_ROPE:]], -1)
    q = _split_heads(_mm(rms_norm(c_q, g_q), pad_heads(w_uq)), MLA_HEADS)
    cos_sin = [(t[0][:, None, :], t[1][:, None, :]) for t in axial]
    q = jnp.concatenate([q[..., :MLA_NOPE], axial_rope(q[..., MLA_NOPE:MLA_NOPE + MLA_ROPE], cos_sin),
                         q[..., MLA_NOPE + MLA_ROPE:]], -1)
    flat = lambda t: t.reshape(t.shape[:2] + (MLA_HEADS * LANE,)).astype(BF16)
    return attention_tokens(flat(q), flat(k), v.astype(BF16))


ATTN_TQ = 256


def _attn_body(q_ref, k_ref, v_ref, o_ref):
    l = k_ref.shape[1]
    col = lax.broadcasted_iota(jnp.int32, (1, l), 1)
    visible = (col < CTX_LEN) | (pl.program_id(1) > 0)
    scale = (MLA_NOPE + MLA_ROPE) ** -0.5
    for h in range(MLA_HEADS):
        lanes = slice(h * LANE, (h + 1) * LANE)
        s = lax.dot_general(q_ref[0, :, lanes], k_ref[0, :, lanes], _NT, preferred_element_type=F32) * scale
        s = jnp.where(visible, s, NEG_INF)
        p = jnp.exp(s - jnp.max(s, axis=1, keepdims=True))
        o = jnp.dot(p.astype(BF16), v_ref[0, :, lanes], preferred_element_type=F32)
        o_ref[0, :, h * MLA_V:(h + 1) * MLA_V] = o[:, :MLA_V] / jnp.sum(p, axis=1, keepdims=True)


def attention_tokens(q, k, v):
    b, l, w = q.shape
    return pl.pallas_call(
        _attn_body,
        grid=(b, l // ATTN_TQ),
        in_specs=[pl.BlockSpec((1, ATTN_TQ, w), lambda bi, qi: (bi, qi, 0)),
                  pl.BlockSpec((1, l, w), lambda bi, qi: (bi, 0, 0)),
                  pl.BlockSpec((1, l, w), lambda bi, qi: (bi, 0, 0))],
        out_specs=pl.BlockSpec((1, ATTN_TQ, MLA_HEADS * MLA_V), lambda bi, qi: (bi, qi, 0)),
        out_shape=jax.ShapeDtypeStruct((b, l, MLA_HEADS * MLA_V), F32),
        compiler_params=pltpu.CompilerParams(vmem_limit_bytes=48 * 2 ** 20),
        name='mla_attention',
    )(q, k, v)


def kernel(x, c, ctx, c_ctx, w_ada, b_ada, w_in, mla_g_q, mla_w_uq, mla_g_kv, mla_w_uk, mla_w_uv, gla_w_gf, gla_b_gf, gla_w_gb, gla_b_gb, gla_g, ssd_conv_w, ssd_conv_b, ssd_dt_bias_f, ssd_dt_bias_b, ssd_a_log_f, ssd_a_log_b, ssd_d, ssd_g, w_out, ln1_g, ln1_b, peer_w_q, peer_keys, peer_u, peer_v, ln2_g, ln2_b):
    seq = x.shape[1]
    rows = seq // GRID_W
    row = jnp.repeat(jnp.arange(rows), GRID_W)
    col = jnp.tile(jnp.arange(GRID_W), rows)
    axial = (_ctx_identity(rope_tables(row, MLA_ROPE // 2)), _ctx_identity(rope_tables(col, MLA_ROPE // 2)))
    ret_rope = _ctx_identity(rope_tables(jnp.arange(seq), RET_DK))
    s_lat = jax.nn.silu(c)
    s_ctx = jax.nn.silu(c_ctx)
    xc = ctx
    for i in range(DEPTH):
        need_ctx = i < DEPTH - 1
        mod = [t[:, None, :] for t in jnp.split(s_lat @ w_ada[i] + b_ada[i], 6, axis=-1)]
        mod_c = jnp.split(s_ctx @ w_ada[i] + b_ada[i], 6, axis=-1)
        xin = jnp.concatenate([modulate(xc, mod_c[0], mod_c[1]), modulate(x, mod[0], mod[1])], 1)
        groups = (MLA_SIZES, GLA_SIZES, SSD_SIZES, RET_SIZES)
        w_groups = split_cols(w_in[i], [sum(g) for g in groups])
        p_all = _mm(xin, jnp.concatenate([jnp.pad(w, ((0, 0), (0, -w.shape[1] % LANE))) for w in w_groups], 1))
        starts = np.cumsum([0] + [w.shape[1] + -w.shape[1] % LANE for w in w_groups])
        p_mla, p_gla, p_ssd, p_ret = (split_cols(p_all[..., s0:s0 + sum(g)], g) for s0, g in zip(starts, groups))
        outs = (
            mla_tokens(p_mla, mla_g_q[i], mla_w_uq[i], mla_g_kv[i], mla_w_uk[i], mla_w_uv[i], axial),
            gla_tokens(p_gla, gla_w_gf[i], gla_b_gf[i], gla_w_gb[i], gla_b_gb[i], gla_g[i]),
            ssd_tokens(p_ssd, ssd_conv_w[i], ssd_conv_b[i], ssd_dt_bias_f[i], ssd_dt_bias_b[i],
                       ssd_a_log_f[i], ssd_a_log_b[i], ssd_d[i], ssd_g[i]),
            ret_tokens(p_ret, ret_rope),
        )
        y_all = _mm(jnp.concatenate(outs, -1), w_out[i])
        y = y_all[:, CTX_LEN:]
        x = layer_norm(DEEPNORM_ALPHA * x + mod[2] * y, ln1_g[i], ln1_b[i])
        toks = [modulate(x, mod[3], mod[4]).reshape(-1, D_MODEL)]
        if need_ctx:
            yc = y_all[:, :CTX_LEN]
            xc = layer_norm(DEEPNORM_ALPHA * xc + mod_c[2] * yc, ln1_g[i], ln1_b[i])
            toks.append(modulate(xc, mod_c[3], mod_c[4]).reshape(-1, D_MODEL))
        n_lat = toks[0].shape[0]
        f_all = peer_dense(jnp.concatenate(toks, 0), peer_w_q[i].T.astype(BF16), peer_keys[i, 0].astype(BF16),
                           peer_keys[i, 1].astype(BF16), peer_u[i].astype(BF16), peer_v[i].T.astype(BF16))
        x = layer_norm(DEEPNORM_ALPHA * x + mod[5] * f_all[:n_lat].reshape(x.shape), ln2_g[i], ln2_b[i])
        if need_ctx:
            xc = layer_norm(DEEPNORM_ALPHA * xc + mod_c[5] * f_all[n_lat:].reshape(xc.shape), ln2_g[i], ln2_b[i])
    return x
```

```python
import functools
import math

import jax
import jax.numpy as jnp
import numpy as np
from jax import lax
from jax.experimental import pallas as pl
from jax.experimental.pallas import tpu as pltpu

D_MODEL = 1024
DEPTH = 4
GRID_W = 64
CHUNK = 64
Q_BLOCK = 128
ROPE_BASE = 10000.0
MLA_HEADS, MLA_NOPE, MLA_ROPE, MLA_V, MLA_Q_LORA, MLA_KV_LORA = 4, 64, 32, 64, 256, 128
GLA_HEADS, GLA_DK, GLA_DV, GLA_GATE_RANK, GLA_GATE_NORM = 4, 32, 64, 16, 16.0
SSD_HEADS, SSD_HEADDIM, SSD_GROUPS, SSD_STATE = 4, 64, 2, 128
RET_HEADS, RET_DK, RET_DV = 4, 32, 64
PEER_HEADS, PEER_NKEYS, PEER_DKEY, PEER_TOPK, PEER_TOKEN_BLOCK = 8, 128, 256, 16, 128
DEEPNORM_ALPHA = (2 * DEPTH) ** 0.25

SSD_BC = SSD_GROUPS * SSD_STATE
SSD_XBC = SSD_HEADS * SSD_HEADDIM + 2 * SSD_BC
MLA_SIZES = (MLA_Q_LORA, MLA_KV_LORA, MLA_ROPE)
GLA_SIZES = (GLA_HEADS * GLA_DK, GLA_HEADS * GLA_DK, GLA_HEADS * GLA_DV, GLA_HEADS * GLA_DV, GLA_GATE_RANK, GLA_GATE_RANK)
SSD_SIZES = (SSD_HEADS * SSD_HEADDIM, SSD_XBC, SSD_HEADS, SSD_HEADS)
RET_SIZES = (RET_HEADS * RET_DK, RET_HEADS * RET_DK, RET_HEADS * RET_DV, RET_HEADS * RET_DV)
IN_SIZES = MLA_SIZES + GLA_SIZES + SSD_SIZES + RET_SIZES

F32 = jnp.float32
BF16 = jnp.bfloat16
LANE = 128


def _mm_body(a_ref, b_ref, o_ref):
    o_ref[...] = jnp.dot(a_ref[...].astype(BF16), b_ref[...].astype(BF16), preferred_element_type=F32)


def _mm(a, b, tm=512, tn=512):
    lead = a.shape[:-1]
    k = a.shape[-1]
    n = b.shape[-1]
    a2 = a.reshape(-1, k)
    m = a2.shape[0]
    tm = math.gcd(tm, m)
    n_pad = -(-n // LANE) * LANE
    tn = max(t for t in range(LANE, tn + 1, LANE) if n_pad % t == 0)
    if n_pad != n:
        b = jnp.pad(b, ((0, 0), (0, n_pad - n)))
    out = pl.pallas_call(
        _mm_body,
        grid=(m // tm, n_pad // tn),
        in_specs=[pl.BlockSpec((tm, k), lambda i, j: (i, 0)), pl.BlockSpec((k, tn), lambda i, j: (0, j))],
        out_specs=pl.BlockSpec((tm, tn), lambda i, j: (i, j)),
        out_shape=jax.ShapeDtypeStruct((m, n_pad), F32),
    )(a2, b)
    return out[:, :n].reshape(lead + (n,))


def layer_norm(x, g, b, eps=1e-5):
    mu = x.mean(-1, keepdims=True)
    xc = x - mu
    var = jnp.mean(xc * xc, -1, keepdims=True)
    return xc * lax.rsqrt(var + eps) * g + b


def rms_norm(x, g, eps=1e-6):
    return x * lax.rsqrt(jnp.mean(x * x, -1, keepdims=True) + eps) * g


def head_norm(o, centre):
    if centre:
        o = o - o.mean(-1, keepdims=True)
    return o * lax.rsqrt(jnp.mean(o * o, -1, keepdims=True) + 1e-6)


def modulate(x, shift, scale):
    return x * (1 + scale) + shift


def split_cols(p, sizes):
    return jnp.split(p, np.cumsum(sizes)[:-1].tolist(), axis=-1)


def partition(pieces):
    a = len(MLA_SIZES)
    b = a + len(GLA_SIZES)
    s = b + len(SSD_SIZES)
    return pieces[:a], pieces[a:b], pieces[b:s], pieces[s:]


def to_heads(t, h):
    b, l, _ = t.shape
    return t.reshape(b, l, h, -1).transpose(0, 2, 1, 3)


def merge_heads(t):
    b, h, l, d = t.shape
    return t.transpose(0, 2, 1, 3).reshape(b, l, h * d)


def rope_tables(pos, dim):
    half = dim // 2
    freqs = ROPE_BASE ** (-jnp.arange(half, dtype=F32) / half)
    ang = pos.astype(F32)[:, None] * freqs
    return jnp.cos(ang), jnp.sin(ang)


def rotate(x, tab):
    cos, sin = tab
    h = x.shape[-1] // 2
    x1, x2 = x[..., :h], x[..., h:]
    return jnp.concatenate([x1 * cos - x2 * sin, x1 * sin + x2 * cos], -1)


def axial_rope(x, axial):
    h = x.shape[-1] // 2
    return jnp.concatenate([rotate(x[..., :h], axial[0]), rotate(x[..., h:], axial[1])], -1)


def attention(q, k, v):
    s = jnp.einsum('bhqd,bhkd->bhqk', q, k) * (q.shape[-1] ** -0.5)
    return jnp.einsum('bhqk,bhkv->bhqv', jax.nn.softmax(s, axis=-1), v)


def blocked_attention(q, k, v):
    b, h, l, d = q.shape
    nb = l // Q_BLOCK
    qb = q.reshape(b, h, nb, Q_BLOCK, d).transpose(2, 0, 1, 3, 4)
    ob = lax.map(lambda qq: attention(qq, k, v), qb)
    return ob.transpose(1, 2, 0, 3, 4).reshape(b, h, l, -1)


def chunk_scan(q, k, v, lg, s0):
    b, h, l, _ = k.shape
    n = l // CHUNK
    mask = jnp.tril(jnp.ones((CHUNK, CHUNK), bool))[:, :, None]

    def chunks(t):
        return t.reshape(b, h, n, CHUNK, t.shape[-1]).transpose(2, 0, 1, 3, 4)

    def step(s, inp):
        qc, kc, vc, gc = inp
        cum = jnp.cumsum(gc, axis=2)
        last = cum[:, :, -1:]
        inter = jnp.einsum('bhik,bhkv->bhiv', qc * jnp.exp(cum), s)
        decay = jnp.exp(jnp.where(mask, cum[:, :, :, None] - cum[:, :, None], -jnp.inf))
        if gc.shape[-1] == 1:
            att = jnp.einsum('bhik,bhjk->bhij', qc, kc) * decay[..., 0]
        else:
            att = jnp.einsum('bhik,bhjk,bhijk->bhij', qc, kc, decay)
        o = inter + jnp.einsum('bhij,bhjv->bhiv', att, vc)
        s = jnp.exp(last[:, :, 0])[..., None] * s + jnp.einsum('bhjk,bhjv->bhkv', kc * jnp.exp(last - cum), vc)
        return s, o

    s_fin, o = lax.scan(step, s0, (chunks(q), chunks(k), chunks(v), chunks(lg)))
    return o.transpose(1, 2, 0, 3, 4).reshape(b, h, l, -1), s_fin


def final_state(k, v, lg):
    cum = jnp.cumsum(lg, axis=2)
    return jnp.einsum('bhjk,bhjv->bhkv', k * jnp.exp(cum[:, :, -1:] - cum), v)


def bidir_scan(lat_dirs, ctx_dirs, need_ctx):
    out_lat, out_ctx = [], []
    for d, (lat, ctxd) in enumerate(zip(lat_dirs, ctx_dirs)):
        flip = (lambda t: jnp.flip(t, axis=2)) if d == 1 else (lambda t: t)
        qc, kc, vc, gc = (None if t is None else flip(t) for t in ctxd)
        if need_ctx:
            s0 = jnp.zeros(kc.shape[:2] + (kc.shape[-1], vc.shape[-1]), vc.dtype)
            oc, state = chunk_scan(qc, kc, vc, gc, s0)
            out_ctx.append(flip(oc))
        else:
            state = final_state(kc, vc, gc)
        ql, kl, vl, gl = (flip(t) for t in lat)
        ol, _ = chunk_scan(ql, kl, vl, gl, state)
        out_lat.append(flip(ol))
    return out_lat[0] + out_lat[1], (out_ctx[0] + out_ctx[1] if need_ctx else None)


def mla_q(c_q, g_q, w_uq, axial):
    q = to_heads(rms_norm(c_q, g_q) @ w_uq, MLA_HEADS)
    if axial is None:
        return q
    return jnp.concatenate([q[..., :MLA_NOPE], axial_rope(q[..., MLA_NOPE:], axial)], -1)


def mla_kv(c_kv, k_rope, g_kv, w_uk, w_uv, axial):
    kv = rms_norm(c_kv, g_kv)
    k_nope = to_heads(kv @ w_uk, MLA_HEADS)
    v = to_heads(kv @ w_uv, MLA_HEADS)
    if axial is not None:
        k_rope = axial_rope(k_rope, axial)
    k_rope = jnp.broadcast_to(k_rope[:, None], k_nope.shape[:3] + (MLA_ROPE,))
    return jnp.concatenate([k_nope, k_rope], -1), v


def mla_mixer(pl_, pc, g_q, w_uq, g_kv, w_uk, w_uv, axial, need_ctx):
    cq_l, ckv_l, kr_l = pl_
    cq_c, ckv_c, kr_c = pc
    k_c, v_c = mla_kv(ckv_c, kr_c, g_kv, w_uk, w_uv, None)
    k_l, v_l = mla_kv(ckv_l, kr_l, g_kv, w_uk, w_uv, axial)
    q_l = mla_q(cq_l, g_q, w_uq, axial)
    o_l = blocked_attention(q_l, jnp.concatenate([k_c, k_l], 2), jnp.concatenate([v_c, v_l], 2))
    o_c = merge_heads(attention(mla_q(cq_c, g_q, w_uq, None), k_c, v_c)) if need_ctx else None
    return merge_heads(o_l), o_c


def gla_prep(p, w_gf, b_gf, w_gb, b_gb, with_q):
    q, k, v, r, lr_f, lr_b = p
    q = (to_heads(q, GLA_HEADS) * GLA_DK ** -0.5) if with_q else None
    k = to_heads(k, GLA_HEADS)
    v = to_heads(v, GLA_HEADS)

    def log_gate(lr, w, b):
        return to_heads(jax.nn.log_sigmoid(lr @ w + b) / GLA_GATE_NORM, GLA_HEADS)

    return (q, k, v, log_gate(lr_f, w_gf, b_gf)), (q, k, v, log_gate(lr_b, w_gb, b_gb)), r


def gla_mixer(pl_, pc, w_gf, b_gf, w_gb, b_gb, g_out, need_ctx):
    lat_f, lat_b, r_l = gla_prep(pl_, w_gf, b_gf, w_gb, b_gb, True)
    ctx_f, ctx_b, r_c = gla_prep(pc, w_gf, b_gf, w_gb, b_gb, need_ctx)
    o_l, o_c = bidir_scan((lat_f, lat_b), (ctx_f, ctx_b), need_ctx)

    def out(o, r):
        return merge_heads(head_norm(o, False)) * g_out * jax.nn.silu(r)

    return out(o_l, r_l), (out(o_c, r_c) if need_ctx else None)


def depthwise_conv(x, w, b):
    pad = w.shape[0] // 2
    y = lax.conv_general_dilated(x, w[:, None, :], (1,), [(pad, pad)],
                                 dimension_numbers=('NWC', 'WIO', 'NWC'), feature_group_count=x.shape[-1])
    return y + b


def ssd_prep(p, conv_w, conv_b, dtb_f, dtb_b, alog_f, alog_b, with_q):
    z, xbc, dt_f, dt_b = p
    xbc = jax.nn.silu(depthwise_conv(xbc, conv_w, conv_b))
    xs, bm, cm = split_cols(xbc, (SSD_HEADS * SSD_HEADDIM, SSD_BC, SSD_BC))
    rep = SSD_HEADS // SSD_GROUPS
    xh = to_heads(xs, SSD_HEADS)
    bh = jnp.repeat(to_heads(bm, SSD_GROUPS), rep, axis=1)
    ch = jnp.repeat(to_heads(cm, SSD_GROUPS), rep, axis=1) if with_q else None

    def direction(dt_raw, dt_bias, a_log):
        dt = jax.nn.softplus(dt_raw + dt_bias).transpose(0, 2, 1)[..., None]
        la = dt * -jnp.exp(a_log)[:, None, None]
        return (ch, bh * dt, xh, la)

    return direction(dt_f, dtb_f, alog_f), direction(dt_b, dtb_b, alog_b), xh, z


def ssd_mixer(pl_, pc, conv_w, conv_b, dtb_f, dtb_b, alog_f, alog_b, d_skip, g_out, need_ctx):
    lat_f, lat_b, x_l, z_l = ssd_prep(pl_, conv_w, conv_b, dtb_f, dtb_b, alog_f, alog_b, True)
    ctx_f, ctx_b, x_c, z_c = ssd_prep(pc, conv_w, conv_b, dtb_f, dtb_b, alog_f, alog_b, need_ctx)
    o_l, o_c = bidir_scan((lat_f, lat_b), (ctx_f, ctx_b), need_ctx)

    def out(o, xh, z):
        y = merge_heads(o + d_skip[:, None, None] * xh)
        return rms_norm(y * jax.nn.silu(z), g_out)

    return out(o_l, x_l, z_l), (out(o_c, x_c, z_c) if need_ctx else None)


def ret_log_decay():
    return jnp.log1p(-jnp.exp2(-5.0 - jnp.arange(RET_HEADS, dtype=F32)))


def ret_prep(p, rope, with_q):
    q, k, v, g = p
    q = to_heads(q, RET_HEADS) if with_q else None
    k = to_heads(k, RET_HEADS) * RET_DK ** -0.5
    if rope is not None:
        q = rotate(q, rope)
        k = rotate(k, rope)
    v = to_heads(v, RET_HEADS)
    la = jnp.broadcast_to(ret_log_decay()[:, None, None], (k.shape[0], RET_HEADS, k.shape[2], 1))
    d = (q, k, v, la)
    return (d, d), g


def ret_mixer(pl_, pc, rope, need_ctx):
    lat, g_l = ret_prep(pl_, rope, True)
    ctx_d, g_c = ret_prep(pc, None, need_ctx)
    o_l, o_c = bidir_scan(lat, ctx_d, need_ctx)

    def out(o, g):
        return merge_heads(head_norm(o, True)) * jax.nn.silu(g)

    return out(o_l, g_l), (out(o_c, g_c) if need_ctx else None)


PEER_TT = 512
PEER_EB = 512
SUBLANE = 8
_NT = (((1,), (1,)), ((), ()))
NEG_INF = float('-inf')
PEER_NCAND = sum(PEER_TOPK // (b + 1) for b in range(PEER_TOPK))


def _merge_exchanges(lo, hi, r):
    step = r * 2
    if step < hi - lo:
        yield from _merge_exchanges(lo, hi, step)
        yield from _merge_exchanges(lo + r, hi, step)
        yield from ((i, i + r) for i in range(lo + r, hi - r, step))
    else:
        yield (lo, lo + r)


def _sort_exchanges(lo, hi):
    if hi > lo:
        mid = lo + (hi - lo) // 2
        yield from _sort_exchanges(lo, mid)
        yield from _sort_exchanges(mid + 1, hi)
        yield from _merge_exchanges(lo, hi, 1)


def _top_values(s, rows_ref):
    n = s.shape[0] // SUBLANE
    v = [s[t * SUBLANE:(t + 1) * SUBLANE, :] for t in range(n)]
    for i, j in _sort_exchanges(0, n - 1):
        v[i], v[j] = jnp.maximum(v[i], v[j]), jnp.minimum(v[i], v[j])
    for k in range(PEER_TOPK):
        m = jnp.max(v[0], axis=0, keepdims=True)
        rows_ref[k:k + 1, :] = m
        pop = v[0] == m
        live = min(n, PEER_TOPK - k) - 1
        for r in range(live):
            v[r] = jnp.where(pop, v[r + 1] if r + 1 < n else NEG_INF, v[r])


def _peer_score_body(x_ref, wqt_ref, k0_ref, k1_ref, s1_ref, e1_ref, s2_ref, e2_ref,
                     qt_scr, v1_scr, v2_scr, cand_scr):
    half = PEER_DKEY // 2
    qt_scr[...] = lax.dot_general(wqt_ref[...], x_ref[...], _NT, preferred_element_type=F32)

    def strip(h, tg):
        s1 = s1_ref[h, tg]
        s2 = s2_ref[h, tg]
        _top_values(s1, v1_scr)
        _top_values(s2, v2_scr)
        r = 0
        for b in range(PEER_TOPK):
            for a in range(PEER_TOPK // (b + 1)):
                cand_scr[r:r + 1, :] = v1_scr[a:a + 1, :] + v2_scr[b:b + 1, :]
                r += 1
        cand_scr[r:, :] = jnp.full((cand_scr.shape[0] - r, LANE), NEG_INF, F32)
        cand = [cand_scr[t0:t0 + SUBLANE, :] for t0 in range(0, cand_scr.shape[0], SUBLANE)]
        prev = None
        cnt = jnp.zeros((1, LANE), F32)
        tau = jnp.full((1, LANE), NEG_INF, F32)
        for _ in range(PEER_TOPK):
            best = None
            for c in cand:
                cur = c if prev is None else jnp.where(c < prev, c, NEG_INF)
                best = cur if best is None else jnp.maximum(best, cur)
            m = jnp.max(best, axis=0, keepdims=True)
            hits = None
            for c in cand:
                eq = jnp.where(c == m, 1.0, 0.0)
                hits = eq if hits is None else hits + eq
            tau = jnp.where(cnt < PEER_TOPK, m, tau)
            cnt = cnt + jnp.sum(hits, axis=0, keepdims=True)
            prev = m
        m1 = v1_scr[0:1, :]
        m2 = v2_scr[0:1, :]
        zs = None
        for c in cand:
            z = jnp.where(c >= tau, jnp.exp(c - (m1 + m2)), 0.0)
            zs = z if zs is None else zs + z
        z_tot = jnp.sum(zs, axis=0, keepdims=True)
        e1_ref[h, tg] = jnp.exp(s1 - m1) / z_tot
        e2_ref[h, tg] = jnp.exp(s2 - m2)
        c1 = jnp.full(s1.shape, jnp.inf, F32)
        for b in range(PEER_TOPK):
            v2b = v2_scr[b:b + 1, :]
            c1 = jnp.where(s1 + v2b >= tau, v2b, c1)
        s1_ref[h, tg] = c1

    def head(h, carry):
        base = pl.multiple_of(h * PEER_DKEY, PEER_DKEY)
        s1 = jnp.dot(k0_ref[...], qt_scr[pl.ds(base, half), :].astype(BF16), preferred_element_type=F32)
        s2 = jnp.dot(k1_ref[...], qt_scr[pl.ds(base + half, half), :].astype(BF16), preferred_element_type=F32)
        for tg in range(x_ref.shape[0] // LANE):
            s1_ref[h, tg] = s1[:, tg * LANE:(tg + 1) * LANE]
            s2_ref[h, tg] = s2[:, tg * LANE:(tg + 1) * LANE]
        for tg in range(x_ref.shape[0] // LANE):
            strip(h, tg)
        return carry

    lax.fori_loop(0, PEER_HEADS, head, 0)


PEER_JQ = 4
PEER_IQ = 4


def _peer_expert_body(x_ref, u_ref, vt_ref, c1_ref, e1_ref, s2_ref, e2_ref, o_ref,
                      h_scr, g_scr, act_scr, acc_scr, bc1_scr, be1_scr):
    e = pl.program_id(1)
    n_blocks = pl.num_programs(1) - 1
    tt = x_ref.shape[0]
    n_i = u_ref.shape[0] // PEER_NKEYS
    span = PEER_JQ * SUBLANE

    @pl.when(e == 0)
    def _():
        acc_scr[...] = jnp.zeros_like(acc_scr)
        act_scr[...] = jnp.zeros_like(act_scr)

    n_tg = tt // LANE
    m_rows = acc_scr.shape[0] // n_tg
    i_base = jnp.minimum(e, n_blocks - 1) * n_i
    for i_loc in range(n_i):
        for tg in range(n_tg):
            for h in range(PEER_HEADS):
                bc1_scr[i_loc, tg, h] = jnp.broadcast_to(c1_ref[h, tg, pl.ds(i_base + i_loc, 1), :], (SUBLANE, LANE))
                be1_scr[i_loc, tg, h] = jnp.broadcast_to(e1_ref[h, tg, pl.ds(i_base + i_loc, 1), :], (SUBLANE, LANE))
    h_scr[...] = lax.dot_general(u_ref[...], x_ref[...], _NT, preferred_element_type=F32)

    def token_strip(tg, carry):
        m0 = pl.multiple_of(tg * m_rows, m_rows)
        acc_scr[pl.ds(m0, m_rows), :] += jnp.dot(vt_ref[pl.ds(m0, m_rows), :], act_scr[...],
                                                  preferred_element_type=F32)
        for i0 in range(0, n_i, PEER_IQ):
            for j0 in range(0, PEER_NKEYS, span):
                g = [[None] * PEER_JQ for _ in range(PEER_IQ)]
                for h in range(PEER_HEADS):
                    cs = [bc1_scr[i0 + ii, tg, h] for ii in range(PEER_IQ)]
                    bs = [be1_scr[i0 + ii, tg, h] for ii in range(PEER_IQ)]
                    for k in range(PEER_JQ):
                        rows = slice(j0 + k * SUBLANE, j0 + (k + 1) * SUBLANE)
                        s2t = s2_ref[h, tg, rows, :]
                        e2t = e2_ref[h, tg, rows, :]
                        for ii in range(PEER_IQ):
                            gh = jnp.where(s2t >= cs[ii], bs[ii] * e2t, 0.0)
                            g[ii][k] = gh if g[ii][k] is None else g[ii][k] + gh
                for ii in range(PEER_IQ):
                    for k in range(PEER_JQ):
                        r0 = (i0 + ii) * PEER_NKEYS + j0 + k * SUBLANE
                        g_scr[tg, r0:r0 + SUBLANE, :] = g[ii][k]
        return carry

    lax.fori_loop(0, n_tg, token_strip, 0)
    for tg in range(n_tg):
        hv = h_scr[:, tg * LANE:(tg + 1) * LANE]
        act_scr[:, tg * LANE:(tg + 1) * LANE] = (
            0.5 * hv * (1.0 + lax.erf(hv * (2.0 ** -0.5))) * g_scr[tg]).astype(BF16)

    @pl.when(e == n_blocks)
    def _():
        o_ref[...] = acc_scr[...].T


def peer_dense(xm, wqt, k0, k1, u_bf, vt_bf, tt=PEER_TT, eb=PEER_EB):
    t, d = xm.shape
    n_blocks = u_bf.shape[0] // eb
    xb = xm.astype(BF16)
    hk = (PEER_HEADS, t // LANE, PEER_NKEYS, LANE)
    sblk = pl.BlockSpec((PEER_HEADS, tt // LANE, PEER_NKEYS, LANE), lambda i: (0, i, 0, 0))
    c1, e1, s2, e2 = pl.pallas_call(
        _peer_score_body,
        grid=(t // tt,),
        in_specs=[pl.BlockSpec((tt, d), lambda i: (i, 0)),
                  pl.BlockSpec(wqt.shape, lambda i: (0, 0)),
                  pl.BlockSpec(k0.shape, lambda i: (0, 0)),
                  pl.BlockSpec(k1.shape, lambda i: (0, 0))],
        out_specs=[sblk, sblk, sblk, sblk],
        out_shape=[jax.ShapeDtypeStruct(hk, F32)] * 4,
        scratch_shapes=[pltpu.VMEM((PEER_HEADS * PEER_DKEY, tt), F32),
                        pltpu.VMEM((PEER_TOPK, LANE), F32), pltpu.VMEM((PEER_TOPK, LANE), F32),
                        pltpu.VMEM((-(-PEER_NCAND // SUBLANE) * SUBLANE, LANE), F32)],
        compiler_params=pltpu.CompilerParams(vmem_limit_bytes=48 * 2 ** 20),
        name='peer_score',
    )(xb, wqt, k0, k1)
    sblk2 = pl.BlockSpec((PEER_HEADS, tt // LANE, PEER_NKEYS, LANE), lambda i, j: (0, i, 0, 0))
    return pl.pallas_call(
        _peer_expert_body,
        grid=(t // tt, n_blocks + 1),
        in_specs=[pl.BlockSpec((tt, d), lambda i, j: (i, 0)),
                  pl.BlockSpec((eb, d), lambda i, j: (jnp.minimum(j, n_blocks - 1), 0)),
                  pl.BlockSpec((d, eb), lambda i, j: (0, jnp.maximum(j - 1, 0))),
                  sblk2, sblk2, sblk2, sblk2],
        out_specs=pl.BlockSpec((tt, d), lambda i, j: (i, 0)),
        out_shape=jax.ShapeDtypeStruct((t, d), F32),
        scratch_shapes=[pltpu.VMEM((eb, tt), F32), pltpu.VMEM((tt // LANE, eb, LANE), F32), pltpu.VMEM((eb, tt), BF16),
                        pltpu.VMEM((d, tt), F32)]
        + [pltpu.VMEM((eb // PEER_NKEYS, tt // LANE, PEER_HEADS, SUBLANE, LANE), F32)] * 2,
        compiler_params=pltpu.CompilerParams(vmem_limit_bytes=48 * 2 ** 20,
                                             dimension_semantics=("arbitrary", "arbitrary")),
        name='peer_expert',
    )(xb, u_bf, vt_bf, c1, e1, s2, e2)


SCAN_TL = 256
CTX_LEN = 256
_HI = lax.Precision.HIGHEST
_TN = (((0,), (0,)), ((), ()))


def _scan_body(*refs, chunk, heads, groups, per_dim, scaled):
    n_in = 6 if scaled else 5
    of_ref, ob_ref, sf_scr, sb_scr = refs[2 * n_in:]
    tl, dk = refs[0].shape[1], refs[0].shape[2] // groups
    vd = refs[2].shape[2]
    kd, dv = heads * dk, vd // heads

    @pl.when(pl.program_id(1) == 0)
    def _():
        sf_scr[...] = jnp.zeros_like(sf_scr)
        sb_scr[...] = jnp.zeros_like(sb_scr)

    ri = lax.broadcasted_iota(jnp.int32, (chunk, chunk), 0)
    ci = lax.broadcasted_iota(jnp.int32, (chunk, chunk), 1)
    khead = lax.broadcasted_iota(jnp.int32, (1, kd), 1) // dk
    vhead = lax.broadcasted_iota(jnp.int32, (1, vd), 1) // dv
    same_head = (lax.broadcasted_iota(jnp.int32, (vd, kd), 0) // dv
                 == lax.broadcasted_iota(jnp.int32, (vd, kd), 1) // dk)

    def per_key(x8):
        out = jnp.zeros((x8.shape[0], kd), F32)
        for h in range(heads):
            out = jnp.where(khead == h, x8[:, h:h + 1], out)
        return out

    def per_head(t):
        if groups == heads:
            return t
        rep = heads // groups
        return jnp.concatenate([t[:, (h // rep) * dk:(h // rep + 1) * dk] for h in range(heads)], axis=1)

    def sweep(in_refs, o_ref, s_scr, fwd):
        q_ref, k_ref, v_ref, lg_ref, lt_ref = in_refs[:5]
        keep = (ri >= ci) if fwd else (ri <= ci)
        tri = keep.astype(F32)
        order = range(tl // chunk) if fwd else reversed(range(tl // chunk))
        for c in order:
            rows = slice(c * chunk, (c + 1) * chunk)
            q, k, v = per_head(q_ref[0, rows, :]), per_head(k_ref[0, rows, :]), v_ref[0, rows, :]
            if scaled:
                k = k * per_key(in_refs[5][0, rows, :])
            cum = jnp.dot(tri, lg_ref[0, rows, :], precision=_HI, preferred_element_type=F32)
            tot = cum[chunk - 1:chunk, :] if fwd else cum[0:1, :]
            if per_dim:
                e_in, e_out, e_tot = jnp.exp(cum), jnp.exp(tot - cum), jnp.exp(tot)
                mid = cum[chunk // 2:chunk // 2 + 1, :]
                q_att = (q * jnp.exp(cum - mid))
                k_att = (k * jnp.exp(mid - cum)).astype(BF16)
            else:
                e_in, e_out, e_tot = per_key(jnp.exp(cum)), per_key(jnp.exp(tot - cum)), per_key(jnp.exp(tot))
                cum_t = lax.dot_general(lt_ref[0, :, rows], tri, _NT, precision=_HI, preferred_element_type=F32)
                q_att, k_att = q, k.astype(BF16)
            s = s_scr[...]
            o = lax.dot_general((q * e_in).astype(BF16), s.astype(BF16), _NT, preferred_element_type=F32)
            for h in range(heads):
                sc = lax.dot_general(jnp.where(khead == h, q_att, 0.0).astype(BF16), k_att, _NT,
                                     preferred_element_type=F32)
                if per_dim:
                    att = jnp.where(keep, sc, 0.0)
                else:
                    att = sc * jnp.exp(jnp.where(keep, cum[:, h:h + 1] - cum_t[h:h + 1, :], NEG_INF))
                o = o + jnp.dot(att.astype(BF16), jnp.where(vhead == h, v, 0.0).astype(BF16),
                                preferred_element_type=F32)
            o_ref[0, rows, :] = o
            kv = lax.dot_general(v.astype(BF16), (k * e_out).astype(BF16), _TN, preferred_element_type=F32)
            s_scr[...] = s * e_tot + jnp.where(same_head, kv, 0.0)

    sweep(refs[:n_in], of_ref, sf_scr, True)
    sweep(refs[n_in:2 * n_in], ob_ref, sb_scr, False)


def scan_bidir(q, k_f, k_b, v, lg_f, lg_b, lt_f, lt_b, ks_f=None, ks_b=None, *, heads, per_dim, chunk, groups=None):
    b, l, qw = q.shape
    groups = groups or heads
    kd = qw // groups * heads
    vd = v.shape[-1]
    scaled = ks_f is not None
    n, n_ctx = l // SCAN_TL, CTX_LEN // SCAN_TL

    def bwd_group(s):
        return jnp.where(s < n_ctx, n_ctx - 1 - s, n - 1 - (s - n_ctx))

    def specs(group):
        tok = lambda w: pl.BlockSpec((1, SCAN_TL, w), lambda bi, s: (bi, group(s), 0))
        return [tok(qw), tok(qw), tok(vd), tok(LANE),
                pl.BlockSpec((1, SUBLANE, SCAN_TL), lambda bi, s: (bi, 0, group(s)))] + ([tok(LANE)] if scaled else [])

    fwd_group = lambda s: s
    o_f, o_b = pl.pallas_call(
        functools.partial(_scan_body, chunk=chunk, heads=heads, groups=groups, per_dim=per_dim, scaled=scaled),
        grid=(b, n),
        in_specs=specs(fwd_group) + specs(bwd_group),
        out_specs=[pl.BlockSpec((1, SCAN_TL, vd), lambda bi, s: (bi, s, 0)),
                   pl.BlockSpec((1, SCAN_TL, vd), lambda bi, s: (bi, bwd_group(s), 0))],
        out_shape=[jax.ShapeDtypeStruct((b, l, vd), F32)] * 2,
        scratch_shapes=[pltpu.VMEM((vd, kd), F32)] * 2,
        compiler_params=pltpu.CompilerParams(dimension_semantics=("arbitrary", "arbitrary")),
        name='scan_bidir',
    )(*((q, k_f, v, lg_f, lt_f) + ((ks_f,) if scaled else ()) + (q, k_b, v, lg_b, lt_b) + ((ks_b,) if scaled else ())))
    return o_f + o_b


def _ctx_identity(tab):
    cos, sin = tab
    pad = (CTX_LEN, cos.shape[1])
    return jnp.concatenate([jnp.ones(pad, F32), cos], 0), jnp.concatenate([jnp.zeros(pad, F32), sin], 0)


def _pad_lanes(t):
    return jnp.pad(t, ((0, 0), (0, 0), (0, LANE - t.shape[-1])))


def _head_rows(t):
    return jnp.pad(t.transpose(0, 2, 1), ((0, 0), (0, SUBLANE - t.shape[-1]), (0, 0)))


def _split_heads(t, h):
    return t.reshape(t.shape[:-1] + (h, t.shape[-1] // h))


def _group_norm(o, heads, centre):
    o4 = _split_heads(o, heads)
    return head_norm(o4, centre).reshape(o.shape)


def _rotate_heads(t, heads, tab):
    cos, sin = tab
    return rotate(_split_heads(t, heads), (cos[:, None, :], sin[:, None, :])).reshape(t.shape)


def _segment_conv(x, w, b):
    def conv(t):
        z = jnp.zeros_like(t[:, :1])
        return (jnp.concatenate([z, t[:, :-1]], 1) * w[0] + t * w[1] + jnp.concatenate([t[:, 1:], z], 1) * w[2]) + b

    return jnp.concatenate([conv(x[:, :CTX_LEN]), conv(x[:, CTX_LEN:])], 1)


def gla_tokens(p, w_gf, b_gf, w_gb, b_gb, g_out):
    q, k, v, r, lr_f, lr_b = p
    no_rows = jnp.zeros((q.shape[0], SUBLANE, q.shape[1]), F32)

    def log_gate(lr, w, b):
        return jax.nn.log_sigmoid(_mm(lr, w) + b) / GLA_GATE_NORM

    o = scan_bidir(q * GLA_DK ** -0.5, k, k, v, log_gate(lr_f, w_gf, b_gf), log_gate(lr_b, w_gb, b_gb),
                   no_rows, no_rows, heads=GLA_HEADS, per_dim=True, chunk=CHUNK)
    return _group_norm(o, GLA_HEADS, False) * g_out * jax.nn.silu(r)


def ssd_tokens(p, conv_w, conv_b, dtb_f, dtb_b, alog_f, alog_b, d_skip, g_out):
    z, xbc, dt_f, dt_b = p
    xbc = jax.nn.silu(_segment_conv(xbc, conv_w, conv_b))
    xs, bm, cm = split_cols(xbc, (SSD_HEADS * SSD_HEADDIM, SSD_BC, SSD_BC))

    def direction(dt_raw, dt_bias, a_log):
        dt = jax.nn.softplus(dt_raw + dt_bias)
        return dt, dt * -jnp.exp(a_log)

    dt_f, la_f = direction(dt_f, dtb_f, alog_f)
    dt_b, la_b = direction(dt_b, dtb_b, alog_b)
    o = scan_bidir(cm, bm, bm, xs, _pad_lanes(la_f), _pad_lanes(la_b), _head_rows(la_f), _head_rows(la_b),
                   _pad_lanes(dt_f), _pad_lanes(dt_b), heads=SSD_HEADS, groups=SSD_GROUPS, per_dim=False, chunk=SCAN_TL)
    y = o + jnp.repeat(d_skip, SSD_HEADDIM) * xs
    return rms_norm(y * jax.nn.silu(z), g_out)


def ret_tokens(p, rope):
    q, k, v, g = p
    q = _rotate_heads(q, RET_HEADS, rope)
    k = _rotate_heads(k * RET_DK ** -0.5, RET_HEADS, rope)
    la = jnp.broadcast_to(ret_log_decay(), q.shape[:2] + (RET_HEADS,))
    o = scan_bidir(q, k, k, v, _pad_lanes(la), _pad_lanes(la), _head_rows(la), _head_rows(la),
                   heads=RET_HEADS, per_dim=False, chunk=SCAN_TL)
    return _group_norm(o, RET_HEADS, True) * jax.nn.silu(g)


def mla_tokens(p, g_q, w_uq, g_kv, w_uk, w_uv, axial):
    c_q, c_kv, k_r = p
    kv = rms_norm(c_kv, g_kv)
    k_nope = to_heads(_mm(kv, w_uk), MLA_HEADS)
    v = to_heads(_mm(kv, w_uv), MLA_HEADS)
    k_rope = axial_rope(k_r, axial)
    k = jnp.concatenate([k_nope, jnp.broadcast_to(k_rope[:, None], k_nope.shape[:3] + (MLA_ROPE,))], -1)
    q = to_heads(_mm(rms_norm(c_q, g_q), w_uq), MLA_HEADS)
    q = jnp.concatenate([q[..., :MLA_NOPE], axial_rope(q[..., MLA_NOPE:], axial)], -1)
    return attention_tokens(q.astype(BF16), k.astype(BF16), v.astype(BF16))


ATTN_TQ = 256


def _attn_body(q_ref, k_ref, v_ref, o_ref):
    heads, l, dv = v_ref.shape[1], v_ref.shape[2], v_ref.shape[3]
    col = lax.broadcasted_iota(jnp.int32, (1, l), 1)
    visible = (col < CTX_LEN) | (pl.program_id(1) > 0)
    scale = q_ref.shape[-1] ** -0.5
    for h in range(heads):
        s = lax.dot_general(q_ref[0, h], k_ref[0, h], _NT, preferred_element_type=F32) * scale
        s = jnp.where(visible, s, NEG_INF)
        p = jnp.exp(s - jnp.max(s, axis=1, keepdims=True))
        o = jnp.dot(p.astype(BF16), v_ref[0, h], preferred_element_type=F32)
        o_ref[0, :, h * dv:(h + 1) * dv] = o / jnp.sum(p, axis=1, keepdims=True)


def attention_tokens(q, k, v):
    b, h, l, d = q.shape
    dv = v.shape[-1]
    return pl.pallas_call(
        _attn_body,
        grid=(b, l // ATTN_TQ),
        in_specs=[pl.BlockSpec((1, h, ATTN_TQ, d), lambda bi, qi: (bi, 0, qi, 0)),
                  pl.BlockSpec((1, h, l, d), lambda bi, qi: (bi, 0, 0, 0)),
                  pl.BlockSpec((1, h, l, dv), lambda bi, qi: (bi, 0, 0, 0))],
        out_specs=pl.BlockSpec((1, ATTN_TQ, h * dv), lambda bi, qi: (bi, qi, 0)),
        out_shape=jax.ShapeDtypeStruct((b, l, h * dv), F32),
        compiler_params=pltpu.CompilerParams(vmem_limit_bytes=48 * 2 ** 20),
        name='mla_attention',
    )(q, k, v)


def kernel(x, c, ctx, c_ctx, w_ada, b_ada, w_in, mla_g_q, mla_w_uq, mla_g_kv, mla_w_uk, mla_w_uv, gla_w_gf, gla_b_gf, gla_w_gb, gla_b_gb, gla_g, ssd_conv_w, ssd_conv_b, ssd_dt_bias_f, ssd_dt_bias_b, ssd_a_log_f, ssd_a_log_b, ssd_d, ssd_g, w_out, ln1_g, ln1_b, peer_w_q, peer_keys, peer_u, peer_v, ln2_g, ln2_b):
    seq = x.shape[1]
    rows = seq // GRID_W
    row = jnp.repeat(jnp.arange(rows), GRID_W)
    col = jnp.tile(jnp.arange(GRID_W), rows)
    axial = (_ctx_identity(rope_tables(row, MLA_ROPE // 2)), _ctx_identity(rope_tables(col, MLA_ROPE // 2)))
    ret_rope = _ctx_identity(rope_tables(jnp.arange(seq), RET_DK))
    s_lat = jax.nn.silu(c)
    s_ctx = jax.nn.silu(c_ctx)
    xc = ctx
    for i in range(DEPTH):
        need_ctx = i < DEPTH - 1
        mod = [t[:, None, :] for t in jnp.split(s_lat @ w_ada[i] + b_ada[i], 6, axis=-1)]
        mod_c = jnp.split(s_ctx @ w_ada[i] + b_ada[i], 6, axis=-1)
        xin = jnp.concatenate([modulate(xc, mod_c[0], mod_c[1]), modulate(x, mod[0], mod[1])], 1)
        p_mla, p_gla, p_ssd, p_ret = partition(split_cols(_mm(xin, w_in[i]), IN_SIZES))
        outs = (
            mla_tokens(p_mla, mla_g_q[i], mla_w_uq[i], mla_g_kv[i], mla_w_uk[i], mla_w_uv[i], axial),
            gla_tokens(p_gla, gla_w_gf[i], gla_b_gf[i], gla_w_gb[i], gla_b_gb[i], gla_g[i]),
            ssd_tokens(p_ssd, ssd_conv_w[i], ssd_conv_b[i], ssd_dt_bias_f[i], ssd_dt_bias_b[i],
                       ssd_a_log_f[i], ssd_a_log_b[i], ssd_d[i], ssd_g[i]),
            ret_tokens(p_ret, ret_rope),
        )
        y_all = _mm(jnp.concatenate(outs, -1), w_out[i])
        y = y_all[:, CTX_LEN:]
        x = layer_norm(DEEPNORM_ALPHA * x + mod[2] * y, ln1_g[i], ln1_b[i])
        toks = [modulate(x, mod[3], mod[4]).reshape(-1, D_MODEL)]
        if need_ctx:
            yc = y_all[:, :CTX_LEN]
            xc = layer_norm(DEEPNORM_ALPHA * xc + mod_c[2] * yc, ln1_g[i], ln1_b[i])
            toks.append(modulate(xc, mod_c[3], mod_c[4]).reshape(-1, D_MODEL))
        n_lat = toks[0].shape[0]
        f_all = peer_dense(jnp.concatenate(toks, 0), peer_w_q[i].T.astype(BF16), peer_keys[i, 0].astype(BF16),
                           peer_keys[i, 1].astype(BF16), peer_u[i].astype(BF16), peer_v[i].T.astype(BF16))
        x = layer_norm(DEEPNORM_ALPHA * x + mod[5] * f_all[:n_lat].reshape(x.shape), ln2_g[i], ln2_b[i])
        if need_ctx:
            xc = layer_norm(DEEPNORM_ALPHA * xc + mod_c[5] * f_all[n_lat:].reshape(xc.shape), ln2_g[i], ln2_b[i])
    return x
```

```python
import functools
import math

import jax
import jax.numpy as jnp
import numpy as np
from jax import lax
from jax.experimental import pallas as pl
from jax.experimental.pallas import tpu as pltpu

D_MODEL = 1024
DEPTH = 4
GRID_W = 64
CHUNK = 64
ROPE_BASE = 10000.0
MLA_HEADS, MLA_NOPE, MLA_ROPE, MLA_V, MLA_Q_LORA, MLA_KV_LORA = 4, 64, 32, 64, 256, 128
GLA_HEADS, GLA_DK, GLA_DV, GLA_GATE_RANK, GLA_GATE_NORM = 4, 32, 64, 16, 16.0
SSD_HEADS, SSD_HEADDIM, SSD_GROUPS, SSD_STATE = 4, 64, 2, 128
RET_HEADS, RET_DK, RET_DV = 4, 32, 64
PEER_HEADS, PEER_NKEYS, PEER_DKEY, PEER_TOPK = 8, 128, 256, 16
DEEPNORM_ALPHA = (2 * DEPTH) ** 0.25

SSD_BC = SSD_GROUPS * SSD_STATE
SSD_XBC = SSD_HEADS * SSD_HEADDIM + 2 * SSD_BC
MLA_SIZES = (MLA_Q_LORA, MLA_KV_LORA, MLA_ROPE)
GLA_SIZES = (GLA_HEADS * GLA_DK, GLA_HEADS * GLA_DK, GLA_HEADS * GLA_DV, GLA_HEADS * GLA_DV, GLA_GATE_RANK, GLA_GATE_RANK)
SSD_SIZES = (SSD_HEADS * SSD_HEADDIM, SSD_XBC, SSD_HEADS, SSD_HEADS)
RET_SIZES = (RET_HEADS * RET_DK, RET_HEADS * RET_DK, RET_HEADS * RET_DV, RET_HEADS * RET_DV)
IN_SIZES = MLA_SIZES + GLA_SIZES + SSD_SIZES + RET_SIZES

F32 = jnp.float32
BF16 = jnp.bfloat16
LANE = 128
VMEM_LIMIT = 48 * 2 ** 20


def _mm_body(a_ref, b_ref, o_ref):
    o_ref[...] = jnp.dot(a_ref[...].astype(BF16), b_ref[...].astype(BF16), preferred_element_type=F32)


def _mm(a, b, tm=512, tn=512):
    lead = a.shape[:-1]
    k = a.shape[-1]
    n = b.shape[-1]
    a2 = a.reshape(-1, k)
    m = a2.shape[0]
    tm = math.gcd(tm, m)
    n_pad = -(-n // LANE) * LANE
    tn = max(t for t in range(LANE, tn + 1, LANE) if n_pad % t == 0)
    if n_pad != n:
        b = jnp.pad(b, ((0, 0), (0, n_pad - n)))
    out = pl.pallas_call(
        _mm_body,
        grid=(m // tm, n_pad // tn),
        in_specs=[pl.BlockSpec((tm, k), lambda i, j: (i, 0)), pl.BlockSpec((k, tn), lambda i, j: (0, j))],
        out_specs=pl.BlockSpec((tm, tn), lambda i, j: (i, j)),
        out_shape=jax.ShapeDtypeStruct((m, n_pad), F32),
    )(a2, b)
    return out[:, :n].reshape(lead + (n,))


def layer_norm(x, g, b, eps=1e-5):
    mu = x.mean(-1, keepdims=True)
    xc = x - mu
    var = jnp.mean(xc * xc, -1, keepdims=True)
    return xc * lax.rsqrt(var + eps) * g + b


def rms_norm(x, g, eps=1e-6):
    return x * lax.rsqrt(jnp.mean(x * x, -1, keepdims=True) + eps) * g


def modulate(x, shift, scale):
    return x * (1 + scale) + shift


def split_cols(p, sizes):
    return jnp.split(p, np.cumsum(sizes)[:-1].tolist(), axis=-1)


def partition(pieces):
    a = len(MLA_SIZES)
    b = a + len(GLA_SIZES)
    s = b + len(SSD_SIZES)
    return pieces[:a], pieces[a:b], pieces[b:s], pieces[s:]


def to_heads(t, h):
    b, l, _ = t.shape
    return t.reshape(b, l, h, -1).transpose(0, 2, 1, 3)


def rope_tables(pos, dim):
    half = dim // 2
    freqs = ROPE_BASE ** (-jnp.arange(half, dtype=F32) / half)
    ang = pos.astype(F32)[:, None] * freqs
    return jnp.cos(ang), jnp.sin(ang)


def rotate(x, tab):
    cos, sin = tab
    h = x.shape[-1] // 2
    x1, x2 = x[..., :h], x[..., h:]
    return jnp.concatenate([x1 * cos - x2 * sin, x1 * sin + x2 * cos], -1)


def axial_rope(x, axial):
    h = x.shape[-1] // 2
    return jnp.concatenate([rotate(x[..., :h], axial[0]), rotate(x[..., h:], axial[1])], -1)


def ret_log_decay():
    return jnp.log1p(-jnp.exp2(-5.0 - jnp.arange(RET_HEADS, dtype=F32)))


PEER_TT = 512
PEER_EB = 512
SUBLANE = 8
_NT = (((1,), (1,)), ((), ()))
NEG_INF = float('-inf')
PEER_NCAND = sum(PEER_TOPK // (b + 1) for b in range(PEER_TOPK))


def _merge_exchanges(lo, hi, r):
    step = r * 2
    if step < hi - lo:
        yield from _merge_exchanges(lo, hi, step)
        yield from _merge_exchanges(lo + r, hi, step)
        yield from ((i, i + r) for i in range(lo + r, hi - r, step))
    else:
        yield (lo, lo + r)


def _sort_exchanges(lo, hi):
    if hi > lo:
        mid = lo + (hi - lo) // 2
        yield from _sort_exchanges(lo, mid)
        yield from _sort_exchanges(mid + 1, hi)
        yield from _merge_exchanges(lo, hi, 1)


def _top_values(s, rows_ref):
    n = s.shape[0] // SUBLANE
    v = [s[t * SUBLANE:(t + 1) * SUBLANE, :] for t in range(n)]
    for i, j in _sort_exchanges(0, n - 1):
        v[i], v[j] = jnp.maximum(v[i], v[j]), jnp.minimum(v[i], v[j])
    for k in range(PEER_TOPK):
        m = jnp.max(v[0], axis=0, keepdims=True)
        rows_ref[k:k + 1, :] = m
        pop = v[0] == m
        live = min(n, PEER_TOPK - k) - 1
        for r in range(live):
            v[r] = jnp.where(pop, v[r + 1] if r + 1 < n else NEG_INF, v[r])


def _peer_score_body(x_ref, wqt_ref, k0_ref, k1_ref, s1_ref, e1_ref, s2_ref, e2_ref,
                     qt_scr, v1_scr, v2_scr, cand_scr):
    half = PEER_DKEY // 2
    qt_scr[...] = lax.dot_general(wqt_ref[...], x_ref[...], _NT, preferred_element_type=F32)

    def strip(h, tg):
        s1 = s1_ref[h, tg]
        s2 = s2_ref[h, tg]
        _top_values(s1, v1_scr)
        _top_values(s2, v2_scr)
        r = 0
        for b in range(PEER_TOPK):
            for a in range(PEER_TOPK // (b + 1)):
                cand_scr[r:r + 1, :] = v1_scr[a:a + 1, :] + v2_scr[b:b + 1, :]
                r += 1
        cand_scr[r:, :] = jnp.full((cand_scr.shape[0] - r, LANE), NEG_INF, F32)
        cand = [cand_scr[t0:t0 + SUBLANE, :] for t0 in range(0, cand_scr.shape[0], SUBLANE)]
        prev = None
        cnt = jnp.zeros((1, LANE), F32)
        tau = jnp.full((1, LANE), NEG_INF, F32)
        for _ in range(PEER_TOPK):
            best = None
            for c in cand:
                cur = c if prev is None else jnp.where(c < prev, c, NEG_INF)
                best = cur if best is None else jnp.maximum(best, cur)
            m = jnp.max(best, axis=0, keepdims=True)
            hits = None
            for c in cand:
                eq = jnp.where(c == m, 1.0, 0.0)
                hits = eq if hits is None else hits + eq
            tau = jnp.where(cnt < PEER_TOPK, m, tau)
            cnt = cnt + jnp.sum(hits, axis=0, keepdims=True)
            prev = m
        m1 = v1_scr[0:1, :]
        m2 = v2_scr[0:1, :]
        zs = None
        for c in cand:
            z = jnp.where(c >= tau, jnp.exp(c - (m1 + m2)), 0.0)
            zs = z if zs is None else zs + z
        z_tot = jnp.sum(zs, axis=0, keepdims=True)
        e1_ref[h, tg] = jnp.exp(s1 - m1) / z_tot
        e2_ref[h, tg] = jnp.exp(s2 - m2)
        c1 = jnp.full(s1.shape, jnp.inf, F32)
        for b in range(PEER_TOPK):
            v2b = v2_scr[b:b + 1, :]
            c1 = jnp.where(s1 + v2b >= tau, v2b, c1)
        s1_ref[h, tg] = c1

    def head(h, carry):
        base = pl.multiple_of(h * PEER_DKEY, PEER_DKEY)
        s1 = jnp.dot(k0_ref[...], qt_scr[pl.ds(base, half), :].astype(BF16), preferred_element_type=F32)
        s2 = jnp.dot(k1_ref[...], qt_scr[pl.ds(base + half, half), :].astype(BF16), preferred_element_type=F32)
        for tg in range(x_ref.shape[0] // LANE):
            s1_ref[h, tg] = s1[:, tg * LANE:(tg + 1) * LANE]
            s2_ref[h, tg] = s2[:, tg * LANE:(tg + 1) * LANE]
        for tg in range(x_ref.shape[0] // LANE):
            strip(h, tg)
        return carry

    lax.fori_loop(0, PEER_HEADS, head, 0)


PEER_JQ = 4
PEER_IQ = 4


def _peer_expert_body(x_ref, u_ref, vt_ref, c1_ref, e1_ref, s2_ref, e2_ref, o_ref,
                      h_scr, g_scr, act_scr, acc_scr, bc1_scr, be1_scr):
    e = pl.program_id(1)
    n_blocks = pl.num_programs(1) - 1
    tt = x_ref.shape[0]
    n_i = u_ref.shape[0] // PEER_NKEYS
    span = PEER_JQ * SUBLANE

    @pl.when(e == 0)
    def _():
        acc_scr[...] = jnp.zeros_like(acc_scr)
        act_scr[...] = jnp.zeros_like(act_scr)

    n_tg = tt // LANE
    m_rows = acc_scr.shape[0] // n_tg
    i_base = jnp.minimum(e, n_blocks - 1) * n_i
    for i_loc in range(n_i):
        for tg in range(n_tg):
            for h in range(PEER_HEADS):
                bc1_scr[i_loc, tg, h] = jnp.broadcast_to(c1_ref[h, tg, pl.ds(i_base + i_loc, 1), :], (SUBLANE, LANE))
                be1_scr[i_loc, tg, h] = jnp.broadcast_to(e1_ref[h, tg, pl.ds(i_base + i_loc, 1), :], (SUBLANE, LANE))
    h_scr[...] = lax.dot_general(u_ref[...], x_ref[...], _NT, preferred_element_type=F32)

    def token_strip(tg, carry):
        m0 = pl.multiple_of(tg * m_rows, m_rows)
        acc_scr[pl.ds(m0, m_rows), :] += jnp.dot(vt_ref[pl.ds(m0, m_rows), :], act_scr[...],
                                                  preferred_element_type=F32)
        for i0 in range(0, n_i, PEER_IQ):
            for j0 in range(0, PEER_NKEYS, span):
                g = [[None] * PEER_JQ for _ in range(PEER_IQ)]
                for h in range(PEER_HEADS):
                    cs = [bc1_scr[i0 + ii, tg, h] for ii in range(PEER_IQ)]
                    bs = [be1_scr[i0 + ii, tg, h] for ii in range(PEER_IQ)]
                    for k in range(PEER_JQ):
                        rows = slice(j0 + k * SUBLANE, j0 + (k + 1) * SUBLANE)
                        s2t = s2_ref[h, tg, rows, :]
                        e2t = e2_ref[h, tg, rows, :]
                        for ii in range(PEER_IQ):
                            gh = jnp.where(s2t >= cs[ii], bs[ii] * e2t, 0.0)
                            g[ii][k] = gh if g[ii][k] is None else g[ii][k] + gh
                for ii in range(PEER_IQ):
                    for k in range(PEER_JQ):
                        r0 = (i0 + ii) * PEER_NKEYS + j0 + k * SUBLANE
                        g_scr[tg, r0:r0 + SUBLANE, :] = g[ii][k]
        return carry

    lax.fori_loop(0, n_tg, token_strip, 0)
    for tg in range(n_tg):
        hv = h_scr[:, tg * LANE:(tg + 1) * LANE]
        act_scr[:, tg * LANE:(tg + 1) * LANE] = (
            0.5 * hv * (1.0 + lax.erf(hv * (2.0 ** -0.5))) * g_scr[tg]).astype(BF16)

    @pl.when(e == n_blocks)
    def _():
        o_ref[...] = acc_scr[...].T


def peer_dense(xm, wqt, k0, k1, u_bf, vt_bf, tt=PEER_TT, eb=PEER_EB):
    t, d = xm.shape
    n_blocks = u_bf.shape[0] // eb
    xb = xm.astype(BF16)
    hk = (PEER_HEADS, t // LANE, PEER_NKEYS, LANE)
    sblk = pl.BlockSpec((PEER_HEADS, tt // LANE, PEER_NKEYS, LANE), lambda i: (0, i, 0, 0))
    c1, e1, s2, e2 = pl.pallas_call(
        _peer_score_body,
        grid=(t // tt,),
        in_specs=[pl.BlockSpec((tt, d), lambda i: (i, 0)),
                  pl.BlockSpec(wqt.shape, lambda i: (0, 0)),
                  pl.BlockSpec(k0.shape, lambda i: (0, 0)),
                  pl.BlockSpec(k1.shape, lambda i: (0, 0))],
        out_specs=[sblk, sblk, sblk, sblk],
        out_shape=[jax.ShapeDtypeStruct(hk, F32)] * 4,
        scratch_shapes=[pltpu.VMEM((PEER_HEADS * PEER_DKEY, tt), F32),
                        pltpu.VMEM((PEER_TOPK, LANE), F32), pltpu.VMEM((PEER_TOPK, LANE), F32),
                        pltpu.VMEM((-(-PEER_NCAND // SUBLANE) * SUBLANE, LANE), F32)],
        compiler_params=pltpu.CompilerParams(vmem_limit_bytes=VMEM_LIMIT),
        name='peer_score',
    )(xb, wqt, k0, k1)
    sblk2 = pl.BlockSpec((PEER_HEADS, tt // LANE, PEER_NKEYS, LANE), lambda i, j: (0, i, 0, 0))
    return pl.pallas_call(
        _peer_expert_body,
        grid=(t // tt, n_blocks + 1),
        in_specs=[pl.BlockSpec((tt, d), lambda i, j: (i, 0)),
                  pl.BlockSpec((eb, d), lambda i, j: (jnp.minimum(j, n_blocks - 1), 0)),
                  pl.BlockSpec((d, eb), lambda i, j: (0, jnp.maximum(j - 1, 0))),
                  sblk2, sblk2, sblk2, sblk2],
        out_specs=pl.BlockSpec((tt, d), lambda i, j: (i, 0)),
        out_shape=jax.ShapeDtypeStruct((t, d), F32),
        scratch_shapes=[pltpu.VMEM((eb, tt), F32), pltpu.VMEM((tt // LANE, eb, LANE), F32), pltpu.VMEM((eb, tt), BF16),
                        pltpu.VMEM((d, tt), F32)]
        + [pltpu.VMEM((eb // PEER_NKEYS, tt // LANE, PEER_HEADS, SUBLANE, LANE), F32)] * 2,
        compiler_params=pltpu.CompilerParams(vmem_limit_bytes=VMEM_LIMIT,
                                             dimension_semantics=("arbitrary", "arbitrary")),
        name='peer_expert',
    )(xb, u_bf, vt_bf, c1, e1, s2, e2)


SCAN_TL = 256
CTX_LEN = 256
_HI = lax.Precision.HIGHEST
_TN = (((0,), (0,)), ((), ()))


def _scan_body(*refs, chunk, heads, groups, per_dim, scaled):
    n_in = 6 if scaled else 5
    of_ref, ob_ref, sf_scr, sb_scr = refs[2 * n_in:]
    tl, dk = refs[0].shape[1], refs[0].shape[2] // groups
    vd = refs[2].shape[2]
    kd, dv = heads * dk, vd // heads

    @pl.when(pl.program_id(1) == 0)
    def _():
        sf_scr[...] = jnp.zeros_like(sf_scr)
        sb_scr[...] = jnp.zeros_like(sb_scr)

    ri = lax.broadcasted_iota(jnp.int32, (chunk, chunk), 0)
    ci = lax.broadcasted_iota(jnp.int32, (chunk, chunk), 1)
    khead = lax.broadcasted_iota(jnp.int32, (1, kd), 1) // dk
    vhead = lax.broadcasted_iota(jnp.int32, (1, vd), 1) // dv
    same_head = (lax.broadcasted_iota(jnp.int32, (vd, kd), 0) // dv
                 == lax.broadcasted_iota(jnp.int32, (vd, kd), 1) // dk)

    def per_key(x8):
        out = jnp.zeros((x8.shape[0], kd), F32)
        for h in range(heads):
            out = jnp.where(khead == h, x8[:, h:h + 1], out)
        return out

    def per_head(t):
        if groups == heads:
            return t
        rep = heads // groups
        return jnp.concatenate([t[:, (h // rep) * dk:(h // rep + 1) * dk] for h in range(heads)], axis=1)

    def sweep(in_refs, o_ref, s_scr, fwd):
        q_ref, k_ref, v_ref, lg_ref, lt_ref = in_refs[:5]
        keep = (ri >= ci) if fwd else (ri <= ci)
        tri = keep.astype(F32)
        order = range(tl // chunk) if fwd else reversed(range(tl // chunk))
        for c in order:
            rows = slice(c * chunk, (c + 1) * chunk)
            q, k, v = per_head(q_ref[0, rows, :]), per_head(k_ref[0, rows, :]), v_ref[0, rows, :]
            if scaled:
                k = k * per_key(in_refs[5][0, rows, :])
            cum = jnp.dot(tri, lg_ref[0, rows, :], precision=_HI, preferred_element_type=F32)
            tot = cum[chunk - 1:chunk, :] if fwd else cum[0:1, :]
            if per_dim:
                e_in, e_out, e_tot = jnp.exp(cum), jnp.exp(tot - cum), jnp.exp(tot)
                mid = cum[chunk // 2:chunk // 2 + 1, :]
                q_att = (q * jnp.exp(cum - mid))
                k_att = (k * jnp.exp(mid - cum)).astype(BF16)
            else:
                e_in, e_out, e_tot = per_key(jnp.exp(cum)), per_key(jnp.exp(tot - cum)), per_key(jnp.exp(tot))
                cum_t = lax.dot_general(lt_ref[0, :, rows], tri, _NT, precision=_HI, preferred_element_type=F32)
                q_att, k_att = q, k.astype(BF16)
            s = s_scr[...]
            o = lax.dot_general((q * e_in).astype(BF16), s.astype(BF16), _NT, preferred_element_type=F32)
            for h in range(heads):
                sc = lax.dot_general(jnp.where(khead == h, q_att, 0.0).astype(BF16), k_att, _NT,
                                     preferred_element_type=F32)
                if per_dim:
                    att = jnp.where(keep, sc, 0.0)
                else:
                    att = sc * jnp.exp(jnp.where(keep, cum[:, h:h + 1] - cum_t[h:h + 1, :], NEG_INF))
                o = o + jnp.dot(att.astype(BF16), jnp.where(vhead == h, v, 0.0).astype(BF16),
                                preferred_element_type=F32)
            o_ref[0, rows, :] = o
            kv = lax.dot_general(v.astype(BF16), (k * e_out).astype(BF16), _TN, preferred_element_type=F32)
            s_scr[...] = s * e_tot + jnp.where(same_head, kv, 0.0)

    sweep(refs[:n_in], of_ref, sf_scr, True)
    sweep(refs[n_in:2 * n_in], ob_ref, sb_scr, False)


def scan_bidir(q, k_f, k_b, v, lg_f, lg_b, lt_f, lt_b, ks_f=None, ks_b=None, *, heads, per_dim, chunk, groups=None):
    b, l, qw = q.shape
    groups = groups or heads
    kd = qw // groups * heads
    vd = v.shape[-1]
    scaled = ks_f is not None
    n, n_ctx = l // SCAN_TL, CTX_LEN // SCAN_TL

    def bwd_group(s):
        return jnp.where(s < n_ctx, n_ctx - 1 - s, n - 1 - (s - n_ctx))

    def specs(group):
        tok = lambda w: pl.BlockSpec((1, SCAN_TL, w), lambda bi, s: (bi, group(s), 0))
        return [tok(qw), tok(qw), tok(vd), tok(LANE),
                pl.BlockSpec((1, SUBLANE, SCAN_TL), lambda bi, s: (bi, 0, group(s)))] + ([tok(LANE)] if scaled else [])

    fwd_group = lambda s: s
    o_f, o_b = pl.pallas_call(
        functools.partial(_scan_body, chunk=chunk, heads=heads, groups=groups, per_dim=per_dim, scaled=scaled),
        grid=(b, n),
        in_specs=specs(fwd_group) + specs(bwd_group),
        out_specs=[pl.BlockSpec((1, SCAN_TL, vd), lambda bi, s: (bi, s, 0)),
                   pl.BlockSpec((1, SCAN_TL, vd), lambda bi, s: (bi, bwd_group(s), 0))],
        out_shape=[jax.ShapeDtypeStruct((b, l, vd), F32)] * 2,
        scratch_shapes=[pltpu.VMEM((vd, kd), F32)] * 2,
        compiler_params=pltpu.CompilerParams(dimension_semantics=("arbitrary", "arbitrary")),
        name='scan_bidir',
    )(*((q, k_f, v, lg_f, lt_f) + ((ks_f,) if scaled else ()) + (q, k_b, v, lg_b, lt_b) + ((ks_b,) if scaled else ())))
    return o_f, o_b


def _ctx_identity(tab):
    cos, sin = tab
    pad = (CTX_LEN, cos.shape[1])
    return jnp.concatenate([jnp.ones(pad, F32), cos], 0), jnp.concatenate([jnp.zeros(pad, F32), sin], 0)


def _pad_lanes(t):
    return jnp.pad(t, ((0, 0), (0, 0), (0, LANE - t.shape[-1])))


def _head_rows(t):
    return jnp.pad(t.transpose(0, 2, 1), ((0, 0), (0, SUBLANE - t.shape[-1]), (0, 0)))


def _split_heads(t, h):
    return t.reshape(t.shape[:-1] + (h, t.shape[-1] // h))


def _rotate_heads(t, heads, tab):
    cos, sin = tab
    return rotate(_split_heads(t, heads), (cos[:, None, :], sin[:, None, :])).reshape(t.shape)


def _segment_conv(x, w, b):
    def conv(t):
        z = jnp.zeros_like(t[:, :1])
        return (jnp.concatenate([z, t[:, :-1]], 1) * w[0] + t * w[1] + jnp.concatenate([t[:, 1:], z], 1) * w[2]) + b

    return jnp.concatenate([conv(x[:, :CTX_LEN]), conv(x[:, CTX_LEN:])], 1)


def gla_tokens(p, w_gf, b_gf, w_gb, b_gb):
    q, k, v, r, lr_f, lr_b = p
    no_rows = jnp.zeros((q.shape[0], SUBLANE, q.shape[1]), F32)

    def log_gate(lr, w, b):
        return jax.nn.log_sigmoid(_mm(lr, w) + b) / GLA_GATE_NORM

    return scan_bidir(q * GLA_DK ** -0.5, k, k, v, log_gate(lr_f, w_gf, b_gf), log_gate(lr_b, w_gb, b_gb),
                      no_rows, no_rows, heads=GLA_HEADS, per_dim=True, chunk=CHUNK) + (r,)


def ssd_tokens(p, conv_w, conv_b, dtb_f, dtb_b, alog_f, alog_b):
    z, xbc, dt_f, dt_b = p
    xbc = jax.nn.silu(_segment_conv(xbc, conv_w, conv_b))
    xs, bm, cm = split_cols(xbc, (SSD_HEADS * SSD_HEADDIM, SSD_BC, SSD_BC))

    def direction(dt_raw, dt_bias, a_log):
        dt = jax.nn.softplus(dt_raw + dt_bias)
        return dt, dt * -jnp.exp(a_log)

    dt_f, la_f = direction(dt_f, dtb_f, alog_f)
    dt_b, la_b = direction(dt_b, dtb_b, alog_b)
    return scan_bidir(cm, bm, bm, xs, _pad_lanes(la_f), _pad_lanes(la_b), _head_rows(la_f), _head_rows(la_b),
                      _pad_lanes(dt_f), _pad_lanes(dt_b), heads=SSD_HEADS, groups=SSD_GROUPS, per_dim=False,
                      chunk=SCAN_TL) + (xs, z)


MIX_TM = 512


def _silu(t):
    return t * jax.nn.sigmoid(t)


def _mix_out_body(mla_ref, gf_ref, gb_ref, gr_ref, sf_ref, sb_ref, sx_ref, sz_ref, rf_ref, rb_ref, rg_ref,
                  gla_g_ref, ssd_d_ref, ssd_g_ref, o_ref):
    w = mla_ref.shape[1]
    dv = w // GLA_HEADS
    avg = jnp.where(lax.broadcasted_iota(jnp.int32, (w, w), 0) // dv == lax.broadcasted_iota(jnp.int32, (w, w), 1) // dv,
                    1.0 / dv, 0.0)

    def head_mean(t):
        return jnp.dot(t, avg, precision=_HI, preferred_element_type=F32)

    o_ref[:, 0:w] = mla_ref[...].astype(o_ref.dtype)
    gla = gf_ref[...] + gb_ref[...]
    gla = gla * lax.rsqrt(head_mean(gla * gla) + 1e-6) * gla_g_ref[...] * _silu(gr_ref[...])
    o_ref[:, w:2 * w] = gla.astype(o_ref.dtype)
    ssd = (sf_ref[...] + sb_ref[...] + ssd_d_ref[...] * sx_ref[...]) * _silu(sz_ref[...])
    ssd = ssd * lax.rsqrt(jnp.mean(ssd * ssd, axis=-1, keepdims=True) + 1e-6) * ssd_g_ref[...]
    o_ref[:, 2 * w:3 * w] = ssd.astype(o_ref.dtype)
    ret = rf_ref[...] + rb_ref[...]
    ret = ret - head_mean(ret)
    ret = ret * lax.rsqrt(head_mean(ret * ret) + 1e-6) * _silu(rg_ref[...])
    o_ref[:, 3 * w:4 * w] = ret.astype(o_ref.dtype)


def mix_out(o_mla, gla, ssd, ret, gla_g, ssd_d, ssd_g):
    b, l, w = o_mla.shape
    rows = b * l
    acts = [t.reshape(rows, w) for t in (o_mla,) + tuple(gla) + tuple(ssd) + tuple(ret)]
    params = [gla_g.reshape(1, w), jnp.repeat(ssd_d, SSD_HEADDIM).reshape(1, w), ssd_g.reshape(1, w)]
    tm = math.gcd(MIX_TM, rows)
    out = pl.pallas_call(
        _mix_out_body,
        grid=(rows // tm,),
        in_specs=[pl.BlockSpec((tm, w), lambda i: (i, 0))] * len(acts)
        + [pl.BlockSpec((1, w), lambda i: (0, 0))] * len(params),
        out_specs=pl.BlockSpec((tm, 4 * w), lambda i: (i, 0)),
        out_shape=jax.ShapeDtypeStruct((rows, 4 * w), BF16),
        name='mix_out',
    )(*acts, *params)
    return out.reshape(b, l, 4 * w)


def ret_tokens(p, rope):
    q, k, v, g = p
    q = _rotate_heads(q, RET_HEADS, rope)
    k = _rotate_heads(k * RET_DK ** -0.5, RET_HEADS, rope)
    la = jnp.broadcast_to(ret_log_decay(), q.shape[:2] + (RET_HEADS,))
    return scan_bidir(q, k, k, v, _pad_lanes(la), _pad_lanes(la), _head_rows(la), _head_rows(la),
                      heads=RET_HEADS, per_dim=False, chunk=SCAN_TL) + (g,)


def mla_tokens(p, g_q, w_uq, g_kv, w_uk, w_uv, axial):
    c_q, c_kv, k_r = p
    kv = rms_norm(c_kv, g_kv)
    k_nope = to_heads(_mm(kv, w_uk), MLA_HEADS)
    v = to_heads(_mm(kv, w_uv), MLA_HEADS)
    k_rope = axial_rope(k_r, axial)
    k = jnp.concatenate([k_nope, jnp.broadcast_to(k_rope[:, None], k_nope.shape[:3] + (MLA_ROPE,))], -1)
    q = to_heads(_mm(rms_norm(c_q, g_q), w_uq), MLA_HEADS)
    q = jnp.concatenate([q[..., :MLA_NOPE], axial_rope(q[..., MLA_NOPE:], axial)], -1)
    return attention_tokens(q.astype(BF16), k.astype(BF16), v.astype(BF16))


ATTN_TQ = 256


def _attn_body(q_ref, k_ref, v_ref, o_ref):
    heads, l, dv = v_ref.shape[1], v_ref.shape[2], v_ref.shape[3]
    col = lax.broadcasted_iota(jnp.int32, (1, l), 1)
    visible = (col < CTX_LEN) | (pl.program_id(1) > 0)
    scale = q_ref.shape[-1] ** -0.5
    for h in range(heads):
        s = lax.dot_general(q_ref[0, h], k_ref[0, h], _NT, preferred_element_type=F32) * scale
        s = jnp.where(visible, s, NEG_INF)
        p = jnp.exp(s - jnp.max(s, axis=1, keepdims=True))
        o = jnp.dot(p.astype(BF16), v_ref[0, h], preferred_element_type=F32)
        o_ref[0, :, h * dv:(h + 1) * dv] = o / jnp.sum(p, axis=1, keepdims=True)


def attention_tokens(q, k, v):
    b, h, l, d = q.shape
    dv = v.shape[-1]
    return pl.pallas_call(
        _attn_body,
        grid=(b, l // ATTN_TQ),
        in_specs=[pl.BlockSpec((1, h, ATTN_TQ, d), lambda bi, qi: (bi, 0, qi, 0)),
                  pl.BlockSpec((1, h, l, d), lambda bi, qi: (bi, 0, 0, 0)),
                  pl.BlockSpec((1, h, l, dv), lambda bi, qi: (bi, 0, 0, 0))],
        out_specs=pl.BlockSpec((1, ATTN_TQ, h * dv), lambda bi, qi: (bi, qi, 0)),
        out_shape=jax.ShapeDtypeStruct((b, l, h * dv), F32),
        compiler_params=pltpu.CompilerParams(vmem_limit_bytes=VMEM_LIMIT),
        name='mla_attention',
    )(q, k, v)


def kernel(x, c, ctx, c_ctx, w_ada, b_ada, w_in, mla_g_q, mla_w_uq, mla_g_kv, mla_w_uk, mla_w_uv, gla_w_gf, gla_b_gf, gla_w_gb, gla_b_gb, gla_g, ssd_conv_w, ssd_conv_b, ssd_dt_bias_f, ssd_dt_bias_b, ssd_a_log_f, ssd_a_log_b, ssd_d, ssd_g, w_out, ln1_g, ln1_b, peer_w_q, peer_keys, peer_u, peer_v, ln2_g, ln2_b):
    seq = x.shape[1]
    rows = seq // GRID_W
    row = jnp.repeat(jnp.arange(rows), GRID_W)
    col = jnp.tile(jnp.arange(GRID_W), rows)
    axial = (_ctx_identity(rope_tables(row, MLA_ROPE // 2)), _ctx_identity(rope_tables(col, MLA_ROPE // 2)))
    ret_rope = _ctx_identity(rope_tables(jnp.arange(seq), RET_DK))
    s_lat = jax.nn.silu(c)
    s_ctx = jax.nn.silu(c_ctx)
    xc = ctx
    for i in range(DEPTH):
        need_ctx = i < DEPTH - 1
        mod = [t[:, None, :] for t in jnp.split(s_lat @ w_ada[i] + b_ada[i], 6, axis=-1)]
        mod_c = jnp.split(s_ctx @ w_ada[i] + b_ada[i], 6, axis=-1)
        xin = jnp.concatenate([modulate(xc, mod_c[0], mod_c[1]), modulate(x, mod[0], mod[1])], 1)
        p_mla, p_gla, p_ssd, p_ret = partition(split_cols(_mm(xin, w_in[i]), IN_SIZES))
        mixed = mix_out(
            mla_tokens(p_mla, mla_g_q[i], mla_w_uq[i], mla_g_kv[i], mla_w_uk[i], mla_w_uv[i], axial),
            gla_tokens(p_gla, gla_w_gf[i], gla_b_gf[i], gla_w_gb[i], gla_b_gb[i]),
            ssd_tokens(p_ssd, ssd_conv_w[i], ssd_conv_b[i], ssd_dt_bias_f[i], ssd_dt_bias_b[i],
                       ssd_a_log_f[i], ssd_a_log_b[i]),
            ret_tokens(p_ret, ret_rope), gla_g[i], ssd_d[i], ssd_g[i])
        y_all = _mm(mixed, w_out[i])
        y = y_all[:, CTX_LEN:]
        x = layer_norm(DEEPNORM_ALPHA * x + mod[2] * y, ln1_g[i], ln1_b[i])
        toks = [modulate(x, mod[3], mod[4]).reshape(-1, D_MODEL)]
        if need_ctx:
            yc = y_all[:, :CTX_LEN]
            xc = layer_norm(DEEPNORM_ALPHA * xc + mod_c[2] * yc, ln1_g[i], ln1_b[i])
            toks.append(modulate(xc, mod_c[3], mod_c[4]).reshape(-1, D_MODEL))
        n_lat = toks[0].shape[0]
        f_all = peer_dense(jnp.concatenate(toks, 0), peer_w_q[i].T.astype(BF16), peer_keys[i, 0].astype(BF16),
                           peer_keys[i, 1].astype(BF16), peer_u[i].astype(BF16), peer_v[i].T.astype(BF16))
        x = layer_norm(DEEPNORM_ALPHA * x + mod[5] * f_all[:n_lat].reshape(x.shape), ln2_g[i], ln2_b[i])
        if need_ctx:
            xc = layer_norm(DEEPNORM_ALPHA * xc + mod_c[5] * f_all[n_lat:].reshape(xc.shape), ln2_g[i], ln2_b[i])
    return x
```

```python
import functools
import math

import jax
import jax.numpy as jnp
import numpy as np
from jax import lax
from jax.experimental import pallas as pl
from jax.experimental.pallas import tpu as pltpu

D_MODEL = 1024
DEPTH = 4
GRID_W = 64
CHUNK = 64
ROPE_BASE = 10000.0
MLA_HEADS, MLA_NOPE, MLA_ROPE, MLA_V, MLA_Q_LORA, MLA_KV_LORA = 4, 64, 32, 64, 256, 128
GLA_HEADS, GLA_DK, GLA_DV, GLA_GATE_RANK, GLA_GATE_NORM = 4, 32, 64, 16, 16.0
SSD_HEADS, SSD_HEADDIM, SSD_GROUPS, SSD_STATE = 4, 64, 2, 128
RET_HEADS, RET_DK, RET_DV = 4, 32, 64
PEER_HEADS, PEER_NKEYS, PEER_DKEY, PEER_TOPK = 8, 128, 256, 16
DEEPNORM_ALPHA = (2 * DEPTH) ** 0.25

SSD_BC = SSD_GROUPS * SSD_STATE
SSD_XBC = SSD_HEADS * SSD_HEADDIM + 2 * SSD_BC
MLA_SIZES = (MLA_Q_LORA, MLA_KV_LORA, MLA_ROPE)
GLA_SIZES = (GLA_HEADS * GLA_DK, GLA_HEADS * GLA_DK, GLA_HEADS * GLA_DV, GLA_HEADS * GLA_DV, GLA_GATE_RANK, GLA_GATE_RANK)
SSD_SIZES = (SSD_HEADS * SSD_HEADDIM, SSD_XBC, SSD_HEADS, SSD_HEADS)
RET_SIZES = (RET_HEADS * RET_DK, RET_HEADS * RET_DK, RET_HEADS * RET_DV, RET_HEADS * RET_DV)
IN_SIZES = MLA_SIZES + GLA_SIZES + SSD_SIZES + RET_SIZES

F32 = jnp.float32
BF16 = jnp.bfloat16
LANE = 128
VMEM_LIMIT = 48 * 2 ** 20


def _mm_body(a_ref, b_ref, o_ref):
    o_ref[...] = jnp.dot(a_ref[...].astype(BF16), b_ref[...].astype(BF16), preferred_element_type=F32)


def _mm(a, b, tm=512, tn=512):
    lead = a.shape[:-1]
    k = a.shape[-1]
    n = b.shape[-1]
    a2 = a.reshape(-1, k)
    m = a2.shape[0]
    tm = math.gcd(tm, m)
    n_pad = -(-n // LANE) * LANE
    tn = max(t for t in range(LANE, tn + 1, LANE) if n_pad % t == 0)
    if n_pad != n:
        b = jnp.pad(b, ((0, 0), (0, n_pad - n)))
    out = pl.pallas_call(
        _mm_body,
        grid=(m // tm, n_pad // tn),
        in_specs=[pl.BlockSpec((tm, k), lambda i, j: (i, 0)), pl.BlockSpec((k, tn), lambda i, j: (0, j))],
        out_specs=pl.BlockSpec((tm, tn), lambda i, j: (i, j)),
        out_shape=jax.ShapeDtypeStruct((m, n_pad), F32),
    )(a2, b)
    return out[:, :n].reshape(lead + (n,))


def layer_norm(x, g, b, eps=1e-5):
    mu = x.mean(-1, keepdims=True)
    xc = x - mu
    var = jnp.mean(xc * xc, -1, keepdims=True)
    return xc * lax.rsqrt(var + eps) * g + b


def rms_norm(x, g, eps=1e-6):
    return x * lax.rsqrt(jnp.mean(x * x, -1, keepdims=True) + eps) * g


def modulate(x, shift, scale):
    return x * (1 + scale) + shift


def split_cols(p, sizes):
    return jnp.split(p, np.cumsum(sizes)[:-1].tolist(), axis=-1)


def partition(pieces):
    a = len(MLA_SIZES)
    b = a + len(GLA_SIZES)
    s = b + len(SSD_SIZES)
    return pieces[:a], pieces[a:b], pieces[b:s], pieces[s:]


def to_heads(t, h):
    b, l, _ = t.shape
    return t.reshape(b, l, h, -1).transpose(0, 2, 1, 3)


def rope_tables(pos, dim):
    half = dim // 2
    freqs = ROPE_BASE ** (-jnp.arange(half, dtype=F32) / half)
    ang = pos.astype(F32)[:, None] * freqs
    return jnp.cos(ang), jnp.sin(ang)


def rotate(x, tab):
    cos, sin = tab
    h = x.shape[-1] // 2
    x1, x2 = x[..., :h], x[..., h:]
    return jnp.concatenate([x1 * cos - x2 * sin, x1 * sin + x2 * cos], -1)


def axial_rope(x, axial):
    h = x.shape[-1] // 2
    return jnp.concatenate([rotate(x[..., :h], axial[0]), rotate(x[..., h:], axial[1])], -1)


def ret_log_decay():
    return jnp.log1p(-jnp.exp2(-5.0 - jnp.arange(RET_HEADS, dtype=F32)))


PEER_TT = 512
PEER_EB = 512
SUBLANE = 8
_NT = (((1,), (1,)), ((), ()))
NEG_INF = float('-inf')
PEER_NCAND = sum(PEER_TOPK // (b + 1) for b in range(PEER_TOPK))


def _merge_exchanges(lo, hi, r):
    step = r * 2
    if step < hi - lo:
        yield from _merge_exchanges(lo, hi, step)
        yield from _merge_exchanges(lo + r, hi, step)
        yield from ((i, i + r) for i in range(lo + r, hi - r, step))
    else:
        yield (lo, lo + r)


def _sort_exchanges(lo, hi):
    if hi > lo:
        mid = lo + (hi - lo) // 2
        yield from _sort_exchanges(lo, mid)
        yield from _sort_exchanges(mid + 1, hi)
        yield from _merge_exchanges(lo, hi, 1)


def _top_values(s, rows_ref):
    n = s.shape[0] // SUBLANE
    v = [s[t * SUBLANE:(t + 1) * SUBLANE, :] for t in range(n)]
    for i, j in _sort_exchanges(0, n - 1):
        v[i], v[j] = jnp.maximum(v[i], v[j]), jnp.minimum(v[i], v[j])
    for k in range(PEER_TOPK):
        m = jnp.max(v[0], axis=0, keepdims=True)
        rows_ref[k:k + 1, :] = m
        pop = v[0] == m
        live = min(n, PEER_TOPK - k) - 1
        for r in range(live):
            v[r] = jnp.where(pop, v[r + 1] if r + 1 < n else NEG_INF, v[r])


def _peer_score_body(x_ref, wqt_ref, k0_ref, k1_ref, s1_ref, e1_ref, s2_ref, e2_ref,
                     qt_scr, v1_scr, v2_scr, cand_scr):
    half = PEER_DKEY // 2
    qt_scr[...] = lax.dot_general(wqt_ref[...], x_ref[...], _NT, preferred_element_type=F32)

    def strip(h, tg):
        s1 = s1_ref[h, tg]
        s2 = s2_ref[h, tg]
        _top_values(s1, v1_scr)
        _top_values(s2, v2_scr)
        r = 0
        for b in range(PEER_TOPK):
            for a in range(PEER_TOPK // (b + 1)):
                cand_scr[r:r + 1, :] = v1_scr[a:a + 1, :] + v2_scr[b:b + 1, :]
                r += 1
        cand_scr[r:, :] = jnp.full((cand_scr.shape[0] - r, LANE), NEG_INF, F32)
        cand = [cand_scr[t0:t0 + SUBLANE, :] for t0 in range(0, cand_scr.shape[0], SUBLANE)]
        prev = None
        cnt = jnp.zeros((1, LANE), F32)
        tau = jnp.full((1, LANE), NEG_INF, F32)
        for _ in range(PEER_TOPK):
            best = None
            for c in cand:
                cur = c if prev is None else jnp.where(c < prev, c, NEG_INF)
                best = cur if best is None else jnp.maximum(best, cur)
            m = jnp.max(best, axis=0, keepdims=True)
            hits = None
            for c in cand:
                eq = jnp.where(c == m, 1.0, 0.0)
                hits = eq if hits is None else hits + eq
            tau = jnp.where(cnt < PEER_TOPK, m, tau)
            cnt = cnt + jnp.sum(hits, axis=0, keepdims=True)
            prev = m
        m1 = v1_scr[0:1, :]
        m2 = v2_scr[0:1, :]
        zs = None
        for c in cand:
            z = jnp.where(c >= tau, jnp.exp(c - (m1 + m2)), 0.0)
            zs = z if zs is None else zs + z
        z_tot = jnp.sum(zs, axis=0, keepdims=True)
        e1_ref[h, tg] = jnp.exp(s1 - m1) / z_tot
        e2_ref[h, tg] = jnp.exp(s2 - m2)
        c1 = jnp.full(s1.shape, jnp.inf, F32)
        for b in range(PEER_TOPK):
            v2b = v2_scr[b:b + 1, :]
            c1 = jnp.where(s1 + v2b >= tau, v2b, c1)
        s1_ref[h, tg] = c1

    def head(h, carry):
        base = pl.multiple_of(h * PEER_DKEY, PEER_DKEY)
        s1 = jnp.dot(k0_ref[...], qt_scr[pl.ds(base, half), :].astype(BF16), preferred_element_type=F32)
        s2 = jnp.dot(k1_ref[...], qt_scr[pl.ds(base + half, half), :].astype(BF16), preferred_element_type=F32)
        for tg in range(x_ref.shape[0] // LANE):
            s1_ref[h, tg] = s1[:, tg * LANE:(tg + 1) * LANE]
            s2_ref[h, tg] = s2[:, tg * LANE:(tg + 1) * LANE]
        for tg in range(x_ref.shape[0] // LANE):
            strip(h, tg)
        return carry

    lax.fori_loop(0, PEER_HEADS, head, 0)


PEER_JQ = 4
PEER_IQ = 4


def _peer_expert_body(x_ref, u_ref, vt_ref, c1_ref, e1_ref, s2_ref, e2_ref, o_ref,
                      h_scr, g_scr, act_scr, acc_scr, bc1_scr, be1_scr):
    e = pl.program_id(1)
    n_blocks = pl.num_programs(1) - 1
    tt = x_ref.shape[0]
    n_i = u_ref.shape[0] // PEER_NKEYS
    span = PEER_JQ * SUBLANE

    @pl.when(e == 0)
    def _():
        acc_scr[...] = jnp.zeros_like(acc_scr)
        act_scr[...] = jnp.zeros_like(act_scr)

    n_tg = tt // LANE
    m_rows = acc_scr.shape[0] // n_tg
    i_base = jnp.minimum(e, n_blocks - 1) * n_i
    for i_loc in range(n_i):
        for tg in range(n_tg):
            for h in range(PEER_HEADS):
                bc1_scr[i_loc, tg, h] = jnp.broadcast_to(c1_ref[h, tg, pl.ds(i_base + i_loc, 1), :], (SUBLANE, LANE))
                be1_scr[i_loc, tg, h] = jnp.broadcast_to(e1_ref[h, tg, pl.ds(i_base + i_loc, 1), :], (SUBLANE, LANE))
    h_scr[...] = lax.dot_general(u_ref[...], x_ref[...], _NT, preferred_element_type=F32)

    def token_strip(tg, carry):
        m0 = pl.multiple_of(tg * m_rows, m_rows)
        acc_scr[pl.ds(m0, m_rows), :] += jnp.dot(vt_ref[pl.ds(m0, m_rows), :], act_scr[...],
                                                  preferred_element_type=F32)
        for i0 in range(0, n_i, PEER_IQ):
            for j0 in range(0, PEER_NKEYS, span):
                g = [[None] * PEER_JQ for _ in range(PEER_IQ)]
                for h in range(PEER_HEADS):
                    cs = [bc1_scr[i0 + ii, tg, h] for ii in range(PEER_IQ)]
                    bs = [be1_scr[i0 + ii, tg, h] for ii in range(PEER_IQ)]
                    for k in range(PEER_JQ):
                        rows = slice(j0 + k * SUBLANE, j0 + (k + 1) * SUBLANE)
                        s2t = s2_ref[h, tg, rows, :]
                        e2t = e2_ref[h, tg, rows, :]
                        for ii in range(PEER_IQ):
                            gh = jnp.where(s2t >= cs[ii], bs[ii] * e2t, 0.0)
                            g[ii][k] = gh if g[ii][k] is None else g[ii][k] + gh
                for ii in range(PEER_IQ):
                    for k in range(PEER_JQ):
                        r0 = (i0 + ii) * PEER_NKEYS + j0 + k * SUBLANE
                        g_scr[tg, r0:r0 + SUBLANE, :] = g[ii][k]
        return carry

    lax.fori_loop(0, n_tg, token_strip, 0)
    for tg in range(n_tg):
        hv = h_scr[:, tg * LANE:(tg + 1) * LANE]
        act_scr[:, tg * LANE:(tg + 1) * LANE] = (
            0.5 * hv * (1.0 + lax.erf(hv * (2.0 ** -0.5))) * g_scr[tg]).astype(BF16)

    @pl.when(e == n_blocks)
    def _():
        o_ref[...] = acc_scr[...].T


def peer_dense(xm, wqt, k0, k1, u_bf, vt_bf, tt=PEER_TT, eb=PEER_EB):
    t, d = xm.shape
    n_blocks = u_bf.shape[0] // eb
    xb = xm.astype(BF16)
    hk = (PEER_HEADS, t // LANE, PEER_NKEYS, LANE)
    sblk = pl.BlockSpec((PEER_HEADS, tt // LANE, PEER_NKEYS, LANE), lambda i: (0, i, 0, 0))
    c1, e1, s2, e2 = pl.pallas_call(
        _peer_score_body,
        grid=(t // tt,),
        in_specs=[pl.BlockSpec((tt, d), lambda i: (i, 0)),
                  pl.BlockSpec(wqt.shape, lambda i: (0, 0)),
                  pl.BlockSpec(k0.shape, lambda i: (0, 0)),
                  pl.BlockSpec(k1.shape, lambda i: (0, 0))],
        out_specs=[sblk, sblk, sblk, sblk],
        out_shape=[jax.ShapeDtypeStruct(hk, F32)] * 4,
        scratch_shapes=[pltpu.VMEM((PEER_HEADS * PEER_DKEY, tt), F32),
                        pltpu.VMEM((PEER_TOPK, LANE), F32), pltpu.VMEM((PEER_TOPK, LANE), F32),
                        pltpu.VMEM((-(-PEER_NCAND // SUBLANE) * SUBLANE, LANE), F32)],
        compiler_params=pltpu.CompilerParams(vmem_limit_bytes=VMEM_LIMIT),
        name='peer_score',
    )(xb, wqt, k0, k1)
    sblk2 = pl.BlockSpec((PEER_HEADS, tt // LANE, PEER_NKEYS, LANE), lambda i, j: (0, i, 0, 0))
    return pl.pallas_call(
        _peer_expert_body,
        grid=(t // tt, n_blocks + 1),
        in_specs=[pl.BlockSpec((tt, d), lambda i, j: (i, 0)),
                  pl.BlockSpec((eb, d), lambda i, j: (jnp.minimum(j, n_blocks - 1), 0)),
                  pl.BlockSpec((d, eb), lambda i, j: (0, jnp.maximum(j - 1, 0))),
                  sblk2, sblk2, sblk2, sblk2],
        out_specs=pl.BlockSpec((tt, d), lambda i, j: (i, 0)),
        out_shape=jax.ShapeDtypeStruct((t, d), F32),
        scratch_shapes=[pltpu.VMEM((eb, tt), F32), pltpu.VMEM((tt // LANE, eb, LANE), F32), pltpu.VMEM((eb, tt), BF16),
                        pltpu.VMEM((d, tt), F32)]
        + [pltpu.VMEM((eb // PEER_NKEYS, tt // LANE, PEER_HEADS, SUBLANE, LANE), F32)] * 2,
        compiler_params=pltpu.CompilerParams(vmem_limit_bytes=VMEM_LIMIT,
                                             dimension_semantics=("arbitrary", "arbitrary")),
        name='peer_expert',
    )(xb, u_bf, vt_bf, c1, e1, s2, e2)


SCAN_TL = 256
CTX_LEN = 256
_HI = lax.Precision.HIGHEST
_TN = (((0,), (0,)), ((), ()))


def _scan_body(*refs, chunk, heads, groups, per_dim, scaled):
    n_in = 6 if scaled else 5
    of_ref, ob_ref, sf_scr, sb_scr = refs[2 * n_in:]
    tl, dk = refs[0].shape[1], refs[0].shape[2] // groups
    vd = refs[2].shape[2]
    kd, dv = heads * dk, vd // heads

    @pl.when(pl.program_id(1) == 0)
    def _():
        sf_scr[...] = jnp.zeros_like(sf_scr)
        sb_scr[...] = jnp.zeros_like(sb_scr)

    ri = lax.broadcasted_iota(jnp.int32, (chunk, chunk), 0)
    ci = lax.broadcasted_iota(jnp.int32, (chunk, chunk), 1)
    khead = lax.broadcasted_iota(jnp.int32, (1, kd), 1) // dk
    vhead = lax.broadcasted_iota(jnp.int32, (1, vd), 1) // dv
    same_head = (lax.broadcasted_iota(jnp.int32, (vd, kd), 0) // dv
                 == lax.broadcasted_iota(jnp.int32, (vd, kd), 1) // dk)

    def per_key(x8):
        out = jnp.zeros((x8.shape[0], kd), F32)
        for h in range(heads):
            out = jnp.where(khead == h, x8[:, h:h + 1], out)
        return out

    def per_head(t):
        if groups == heads:
            return t
        rep = heads // groups
        return jnp.concatenate([t[:, (h // rep) * dk:(h // rep + 1) * dk] for h in range(heads)], axis=1)

    def sweep(in_refs, o_ref, s_scr, fwd):
        q_ref, k_ref, v_ref, lg_ref, lt_ref = in_refs[:5]
        keep = (ri >= ci) if fwd else (ri <= ci)
        tri = keep.astype(F32)
        order = range(tl // chunk) if fwd else reversed(range(tl // chunk))
        for c in order:
            rows = slice(c * chunk, (c + 1) * chunk)
            q, k, v = per_head(q_ref[0, rows, :]), per_head(k_ref[0, rows, :]), v_ref[0, rows, :]
            if scaled:
                k = k * per_key(in_refs[5][0, rows, :])
            cum = jnp.dot(tri, lg_ref[0, rows, :], precision=_HI, preferred_element_type=F32)
            tot = cum[chunk - 1:chunk, :] if fwd else cum[0:1, :]
            if per_dim:
                e_in, e_out, e_tot = jnp.exp(cum), jnp.exp(tot - cum), jnp.exp(tot)
                mid = cum[chunk // 2:chunk // 2 + 1, :]
                q_att = (q * jnp.exp(cum - mid))
                k_att = (k * jnp.exp(mid - cum)).astype(BF16)
            else:
                e_in, e_out, e_tot = per_key(jnp.exp(cum)), per_key(jnp.exp(tot - cum)), per_key(jnp.exp(tot))
                cum_t = lax.dot_general(lt_ref[0, :, rows], tri, _NT, precision=_HI, preferred_element_type=F32)
                q_att, k_att = q, k.astype(BF16)
            s = s_scr[...]
            o = lax.dot_general((q * e_in).astype(BF16), s.astype(BF16), _NT, preferred_element_type=F32)
            for h in range(heads):
                sc = lax.dot_general(jnp.where(khead == h, q_att, 0.0).astype(BF16), k_att, _NT,
                                     preferred_element_type=F32)
                if per_dim:
                    att = jnp.where(keep, sc, 0.0)
                else:
                    att = sc * jnp.exp(jnp.where(keep, cum[:, h:h + 1] - cum_t[h:h + 1, :], NEG_INF))
                o = o + jnp.dot(att.astype(BF16), jnp.where(vhead == h, v, 0.0).astype(BF16),
                                preferred_element_type=F32)
            o_ref[0, rows, :] = o
            kv = lax.dot_general(v.astype(BF16), (k * e_out).astype(BF16), _TN, preferred_element_type=F32)
            s_scr[...] = s * e_tot + jnp.where(same_head, kv, 0.0)

    sweep(refs[:n_in], of_ref, sf_scr, True)
    sweep(refs[n_in:2 * n_in], ob_ref, sb_scr, False)


def scan_bidir(q, k_f, k_b, v, lg_f, lg_b, lt_f, lt_b, ks_f=None, ks_b=None, *, heads, per_dim, chunk, groups=None):
    b, l, qw = q.shape
    groups = groups or heads
    kd = qw // groups * heads
    vd = v.shape[-1]
    scaled = ks_f is not None
    n, n_ctx = l // SCAN_TL, CTX_LEN // SCAN_TL

    def bwd_group(s):
        return jnp.where(s < n_ctx, n_ctx - 1 - s, n - 1 - (s - n_ctx))

    def specs(group):
        tok = lambda w: pl.BlockSpec((1, SCAN_TL, w), lambda bi, s: (bi, group(s), 0))
        return [tok(qw), tok(qw), tok(vd), tok(LANE),
                pl.BlockSpec((1, SUBLANE, SCAN_TL), lambda bi, s: (bi, 0, group(s)))] + ([tok(LANE)] if scaled else [])

    fwd_group = lambda s: s
    o_f, o_b = pl.pallas_call(
        functools.partial(_scan_body, chunk=chunk, heads=heads, groups=groups, per_dim=per_dim, scaled=scaled),
        grid=(b, n),
        in_specs=specs(fwd_group) + specs(bwd_group),
        out_specs=[pl.BlockSpec((1, SCAN_TL, vd), lambda bi, s: (bi, s, 0)),
                   pl.BlockSpec((1, SCAN_TL, vd), lambda bi, s: (bi, bwd_group(s), 0))],
        out_shape=[jax.ShapeDtypeStruct((b, l, vd), F32)] * 2,
        scratch_shapes=[pltpu.VMEM((vd, kd), F32)] * 2,
        compiler_params=pltpu.CompilerParams(dimension_semantics=("arbitrary", "arbitrary")),
        name='scan_bidir',
    )(*((q, k_f, v, lg_f, lt_f) + ((ks_f,) if scaled else ()) + (q, k_b, v, lg_b, lt_b) + ((ks_b,) if scaled else ())))
    return o_f, o_b


def _ctx_identity(tab):
    cos, sin = tab
    pad = (CTX_LEN, cos.shape[1])
    return jnp.concatenate([jnp.ones(pad, F32), cos], 0), jnp.concatenate([jnp.zeros(pad, F32), sin], 0)


def _pad_lanes(t):
    return jnp.pad(t, ((0, 0), (0, 0), (0, LANE - t.shape[-1])))


def _head_rows(t):
    return jnp.pad(t.transpose(0, 2, 1), ((0, 0), (0, SUBLANE - t.shape[-1]), (0, 0)))


def _split_heads(t, h):
    return t.reshape(t.shape[:-1] + (h, t.shape[-1] // h))


def _rotate_heads(t, heads, tab):
    cos, sin = tab
    return rotate(_split_heads(t, heads), (cos[:, None, :], sin[:, None, :])).reshape(t.shape)


CONV_ROWS = 64


def _conv_silu_body(x_ref, w_ref, b_ref, o_ref, xp_scr):
    l, c = x_ref.shape[1], x_ref.shape[2]
    zeros = jnp.zeros((SUBLANE, c), F32)
    xp_scr[0:SUBLANE, :] = zeros
    xp_scr[SUBLANE:SUBLANE + l, :] = x_ref[0]
    xp_scr[SUBLANE + l:2 * SUBLANE + l, :] = zeros
    w0, w1, w2, bias = w_ref[0:1, :], w_ref[1:2, :], w_ref[2:3, :], b_ref[...]
    for r0 in range(0, l, CONV_ROWS):
        row = r0 + lax.broadcasted_iota(jnp.int32, (CONV_ROWS, 1), 0)
        prev = xp_scr[SUBLANE - 1 + r0:SUBLANE - 1 + r0 + CONV_ROWS, :]
        nxt = xp_scr[SUBLANE + 1 + r0:SUBLANE + 1 + r0 + CONV_ROWS, :]
        if r0 <= CTX_LEN < r0 + CONV_ROWS:
            prev = jnp.where(row == CTX_LEN, 0.0, prev)
        if r0 <= CTX_LEN - 1 < r0 + CONV_ROWS:
            nxt = jnp.where(row == CTX_LEN - 1, 0.0, nxt)
        y = prev * w0 + xp_scr[SUBLANE + r0:SUBLANE + r0 + CONV_ROWS, :] * w1 + nxt * w2 + bias
        o_ref[0, r0:r0 + CONV_ROWS, :] = y * jax.nn.sigmoid(y)


def _segment_conv_silu(x, w, b):
    bsz, l, c = x.shape
    return pl.pallas_call(
        _conv_silu_body,
        grid=(bsz,),
        in_specs=[pl.BlockSpec((1, l, c), lambda i: (i, 0, 0)),
                  pl.BlockSpec(w.shape, lambda i: (0, 0)),
                  pl.BlockSpec((1, c), lambda i: (0, 0))],
        out_specs=pl.BlockSpec((1, l, c), lambda i: (i, 0, 0)),
        out_shape=jax.ShapeDtypeStruct(x.shape, F32),
        scratch_shapes=[pltpu.VMEM((l + 2 * SUBLANE, c), F32)],
        compiler_params=pltpu.CompilerParams(vmem_limit_bytes=VMEM_LIMIT),
        name='conv_silu',
    )(x, w, b.reshape(1, c))


def gla_tokens(p, w_gf, b_gf, w_gb, b_gb):
    q, k, v, r, lr_f, lr_b = p
    no_rows = jnp.zeros((q.shape[0], SUBLANE, q.shape[1]), F32)

    def log_gate(lr, w, b):
        return jax.nn.log_sigmoid(_mm(lr, w) + b) / GLA_GATE_NORM

    return scan_bidir(q * GLA_DK ** -0.5, k, k, v, log_gate(lr_f, w_gf, b_gf), log_gate(lr_b, w_gb, b_gb),
                      no_rows, no_rows, heads=GLA_HEADS, per_dim=True, chunk=CHUNK) + (r,)


def ssd_tokens(p, conv_w, conv_b, dtb_f, dtb_b, alog_f, alog_b):
    z, xbc, dt_f, dt_b = p
    xbc = _segment_conv_silu(xbc, conv_w, conv_b)
    xs, bm, cm = split_cols(xbc, (SSD_HEADS * SSD_HEADDIM, SSD_BC, SSD_BC))

    def direction(dt_raw, dt_bias, a_log):
        dt = jax.nn.softplus(dt_raw + dt_bias)
        return dt, dt * -jnp.exp(a_log)

    dt_f, la_f = direction(dt_f, dtb_f, alog_f)
    dt_b, la_b = direction(dt_b, dtb_b, alog_b)
    return scan_bidir(cm, bm, bm, xs, _pad_lanes(la_f), _pad_lanes(la_b), _head_rows(la_f), _head_rows(la_b),
                      _pad_lanes(dt_f), _pad_lanes(dt_b), heads=SSD_HEADS, groups=SSD_GROUPS, per_dim=False,
                      chunk=SCAN_TL) + (xs, z)


MIX_TM = 512


def _silu(t):
    return t * jax.nn.sigmoid(t)


def _mix_out_body(mla_ref, gf_ref, gb_ref, gr_ref, sf_ref, sb_ref, sx_ref, sz_ref, rf_ref, rb_ref, rg_ref,
                  gla_g_ref, ssd_d_ref, ssd_g_ref, o_ref):
    w = mla_ref.shape[1]
    dv = w // GLA_HEADS
    avg = jnp.where(lax.broadcasted_iota(jnp.int32, (w, w), 0) // dv == lax.broadcasted_iota(jnp.int32, (w, w), 1) // dv,
                    1.0 / dv, 0.0)

    def head_mean(t):
        return jnp.dot(t, avg, precision=_HI, preferred_element_type=F32)

    o_ref[:, 0:w] = mla_ref[...].astype(o_ref.dtype)
    gla = gf_ref[...] + gb_ref[...]
    gla = gla * lax.rsqrt(head_mean(gla * gla) + 1e-6) * gla_g_ref[...] * _silu(gr_ref[...])
    o_ref[:, w:2 * w] = gla.astype(o_ref.dtype)
    ssd = (sf_ref[...] + sb_ref[...] + ssd_d_ref[...] * sx_ref[...]) * _silu(sz_ref[...])
    ssd = ssd * lax.rsqrt(jnp.mean(ssd * ssd, axis=-1, keepdims=True) + 1e-6) * ssd_g_ref[...]
    o_ref[:, 2 * w:3 * w] = ssd.astype(o_ref.dtype)
    ret = rf_ref[...] + rb_ref[...]
    ret = ret - head_mean(ret)
    ret = ret * lax.rsqrt(head_mean(ret * ret) + 1e-6) * _silu(rg_ref[...])
    o_ref[:, 3 * w:4 * w] = ret.astype(o_ref.dtype)


def mix_out(o_mla, gla, ssd, ret, gla_g, ssd_d, ssd_g):
    b, l, w = o_mla.shape
    rows = b * l
    acts = [t.reshape(rows, w) for t in (o_mla,) + tuple(gla) + tuple(ssd) + tuple(ret)]
    params = [gla_g.reshape(1, w), jnp.repeat(ssd_d, SSD_HEADDIM).reshape(1, w), ssd_g.reshape(1, w)]
    tm = math.gcd(MIX_TM, rows)
    out = pl.pallas_call(
        _mix_out_body,
        grid=(rows // tm,),
        in_specs=[pl.BlockSpec((tm, w), lambda i: (i, 0))] * len(acts)
        + [pl.BlockSpec((1, w), lambda i: (0, 0))] * len(params),
        out_specs=pl.BlockSpec((tm, 4 * w), lambda i: (i, 0)),
        out_shape=jax.ShapeDtypeStruct((rows, 4 * w), BF16),
        name='mix_out',
    )(*acts, *params)
    return out.reshape(b, l, 4 * w)


def ret_tokens(p, rope):
    q, k, v, g = p
    q = _rotate_heads(q, RET_HEADS, rope)
    k = _rotate_heads(k * RET_DK ** -0.5, RET_HEADS, rope)
    la = jnp.broadcast_to(ret_log_decay(), q.shape[:2] + (RET_HEADS,))
    return scan_bidir(q, k, k, v, _pad_lanes(la), _pad_lanes(la), _head_rows(la), _head_rows(la),
                      heads=RET_HEADS, per_dim=False, chunk=SCAN_TL) + (g,)


def mla_tokens(p, g_q, w_uq, g_kv, w_uk, w_uv, axial):
    c_q, c_kv, k_r = p
    kv = rms_norm(c_kv, g_kv)
    k_nope = to_heads(_mm(kv, w_uk), MLA_HEADS)
    v = to_heads(_mm(kv, w_uv), MLA_HEADS)
    k_rope = axial_rope(k_r, axial)
    k = jnp.concatenate([k_nope, jnp.broadcast_to(k_rope[:, None], k_nope.shape[:3] + (MLA_ROPE,))], -1)
    q = to_heads(_mm(rms_norm(c_q, g_q), w_uq), MLA_HEADS)
    q = jnp.concatenate([q[..., :MLA_NOPE], axial_rope(q[..., MLA_NOPE:], axial)], -1)
    return attention_tokens(q.astype(BF16), k.astype(BF16), v.astype(BF16))


ATTN_TQ = 256


def _attn_body(q_ref, k_ref, v_ref, o_ref):
    heads, l, dv = v_ref.shape[1], v_ref.shape[2], v_ref.shape[3]
    col = lax.broadcasted_iota(jnp.int32, (1, l), 1)
    visible = (col < CTX_LEN) | (pl.program_id(1) > 0)
    scale = q_ref.shape[-1] ** -0.5
    for h in range(heads):
        s = lax.dot_general(q_ref[0, h], k_ref[0, h], _NT, preferred_element_type=F32) * scale
        s = jnp.where(visible, s, NEG_INF)
        p = jnp.exp(s - jnp.max(s, axis=1, keepdims=True))
        o = jnp.dot(p.astype(BF16), v_ref[0, h], preferred_element_type=F32)
        o_ref[0, :, h * dv:(h + 1) * dv] = o / jnp.sum(p, axis=1, keepdims=True)


def attention_tokens(q, k, v):
    b, h, l, d = q.shape
    dv = v.shape[-1]
    return pl.pallas_call(
        _attn_body,
        grid=(b, l // ATTN_TQ),
        in_specs=[pl.BlockSpec((1, h, ATTN_TQ, d), lambda bi, qi: (bi, 0, qi, 0)),
                  pl.BlockSpec((1, h, l, d), lambda bi, qi: (bi, 0, 0, 0)),
                  pl.BlockSpec((1, h, l, dv), lambda bi, qi: (bi, 0, 0, 0))],
        out_specs=pl.BlockSpec((1, ATTN_TQ, h * dv), lambda bi, qi: (bi, qi, 0)),
        out_shape=jax.ShapeDtypeStruct((b, l, h * dv), F32),
        compiler_params=pltpu.CompilerParams(vmem_limit_bytes=VMEM_LIMIT),
        name='mla_attention',
    )(q, k, v)


def kernel(x, c, ctx, c_ctx, w_ada, b_ada, w_in, mla_g_q, mla_w_uq, mla_g_kv, mla_w_uk, mla_w_uv, gla_w_gf, gla_b_gf, gla_w_gb, gla_b_gb, gla_g, ssd_conv_w, ssd_conv_b, ssd_dt_bias_f, ssd_dt_bias_b, ssd_a_log_f, ssd_a_log_b, ssd_d, ssd_g, w_out, ln1_g, ln1_b, peer_w_q, peer_keys, peer_u, peer_v, ln2_g, ln2_b):
    seq = x.shape[1]
    rows = seq // GRID_W
    row = jnp.repeat(jnp.arange(rows), GRID_W)
    col = jnp.tile(jnp.arange(GRID_W), rows)
    axial = (_ctx_identity(rope_tables(row, MLA_ROPE // 2)), _ctx_identity(rope_tables(col, MLA_ROPE // 2)))
    ret_rope = _ctx_identity(rope_tables(jnp.arange(seq), RET_DK))
    s_lat = jax.nn.silu(c)
    s_ctx = jax.nn.silu(c_ctx)
    xc = ctx
    for i in range(DEPTH):
        need_ctx = i < DEPTH - 1
        mod = [t[:, None, :] for t in jnp.split(s_lat @ w_ada[i] + b_ada[i], 6, axis=-1)]
        mod_c = jnp.split(s_ctx @ w_ada[i] + b_ada[i], 6, axis=-1)
        xin = jnp.concatenate([modulate(xc, mod_c[0], mod_c[1]), modulate(x, mod[0], mod[1])], 1)
        p_mla, p_gla, p_ssd, p_ret = partition(split_cols(_mm(xin, w_in[i]), IN_SIZES))
        mixed = mix_out(
            mla_tokens(p_mla, mla_g_q[i], mla_w_uq[i], mla_g_kv[i], mla_w_uk[i], mla_w_uv[i], axial),
            gla_tokens(p_gla, gla_w_gf[i], gla_b_gf[i], gla_w_gb[i], gla_b_gb[i]),
            ssd_tokens(p_ssd, ssd_conv_w[i], ssd_conv_b[i], ssd_dt_bias_f[i], ssd_dt_bias_b[i],
                       ssd_a_log_f[i], ssd_a_log_b[i]),
            ret_tokens(p_ret, ret_rope), gla_g[i], ssd_d[i], ssd_g[i])
        y_all = _mm(mixed, w_out[i])
        y = y_all[:, CTX_LEN:]
        x = layer_norm(DEEPNORM_ALPHA * x + mod[2] * y, ln1_g[i], ln1_b[i])
        toks = [modulate(x, mod[3], mod[4]).reshape(-1, D_MODEL)]
        if need_ctx:
            yc = y_all[:, :CTX_LEN]
            xc = layer_norm(DEEPNORM_ALPHA * xc + mod_c[2] * yc, ln1_g[i], ln1_b[i])
            toks.append(modulate(xc, mod_c[3], mod_c[4]).reshape(-1, D_MODEL))
        n_lat = toks[0].shape[0]
        f_all = peer_dense(jnp.concatenate(toks, 0), peer_w_q[i].T.astype(BF16), peer_keys[i, 0].astype(BF16),
                           peer_keys[i, 1].astype(BF16), peer_u[i].astype(BF16), peer_v[i].T.astype(BF16))
        x = layer_norm(DEEPNORM_ALPHA * x + mod[5] * f_all[:n_lat].reshape(x.shape), ln2_g[i], ln2_b[i])
        if need_ctx:
            xc = layer_norm(DEEPNORM_ALPHA * xc + mod_c[5] * f_all[n_lat:].reshape(xc.shape), ln2_g[i], ln2_b[i])
    return x
```

```python
import functools
import math

import jax
import jax.numpy as jnp
import numpy as np
from jax import lax
from jax.experimental import pallas as pl
from jax.experimental.pallas import tpu as pltpu

D_MODEL = 1024
DEPTH = 4
GRID_W = 64
CHUNK = 64
ROPE_BASE = 10000.0
MLA_HEADS, MLA_NOPE, MLA_ROPE, MLA_V, MLA_Q_LORA, MLA_KV_LORA = 4, 64, 32, 64, 256, 128
GLA_HEADS, GLA_DK, GLA_DV, GLA_GATE_RANK, GLA_GATE_NORM = 4, 32, 64, 16, 16.0
SSD_HEADS, SSD_HEADDIM, SSD_GROUPS, SSD_STATE = 4, 64, 2, 128
RET_HEADS, RET_DK, RET_DV = 4, 32, 64
PEER_HEADS, PEER_NKEYS, PEER_DKEY, PEER_TOPK = 8, 128, 256, 16
DEEPNORM_ALPHA = (2 * DEPTH) ** 0.25

SSD_BC = SSD_GROUPS * SSD_STATE
SSD_XBC = SSD_HEADS * SSD_HEADDIM + 2 * SSD_BC
MLA_SIZES = (MLA_Q_LORA, MLA_KV_LORA, MLA_ROPE)
GLA_SIZES = (GLA_HEADS * GLA_DK, GLA_HEADS * GLA_DK, GLA_HEADS * GLA_DV, GLA_HEADS * GLA_DV, GLA_GATE_RANK, GLA_GATE_RANK)
SSD_SIZES = (SSD_HEADS * SSD_HEADDIM, SSD_XBC, SSD_HEADS, SSD_HEADS)
RET_SIZES = (RET_HEADS * RET_DK, RET_HEADS * RET_DK, RET_HEADS * RET_DV, RET_HEADS * RET_DV)
IN_SIZES = MLA_SIZES + GLA_SIZES + SSD_SIZES + RET_SIZES

F32 = jnp.float32
BF16 = jnp.bfloat16
LANE = 128
VMEM_LIMIT = 48 * 2 ** 20


def _mm_body(a_ref, b_ref, o_ref):
    o_ref[...] = jnp.dot(a_ref[...].astype(BF16), b_ref[...].astype(BF16), preferred_element_type=F32)


def _mm(a, b, tm=512, tn=512):
    lead = a.shape[:-1]
    k = a.shape[-1]
    n = b.shape[-1]
    a2 = a.reshape(-1, k)
    m = a2.shape[0]
    tm = math.gcd(tm, m)
    n_pad = -(-n // LANE) * LANE
    tn = max(t for t in range(LANE, tn + 1, LANE) if n_pad % t == 0)
    if n_pad != n:
        b = jnp.pad(b, ((0, 0), (0, n_pad - n)))
    out = pl.pallas_call(
        _mm_body,
        grid=(m // tm, n_pad // tn),
        in_specs=[pl.BlockSpec((tm, k), lambda i, j: (i, 0)), pl.BlockSpec((k, tn), lambda i, j: (0, j))],
        out_specs=pl.BlockSpec((tm, tn), lambda i, j: (i, j)),
        out_shape=jax.ShapeDtypeStruct((m, n_pad), F32),
    )(a2, b)
    return out[:, :n].reshape(lead + (n,))


def rms_norm(x, g, eps=1e-6):
    return x * lax.rsqrt(jnp.mean(x * x, -1, keepdims=True) + eps) * g


def modulate(x, shift, scale):
    return x * (1 + scale) + shift


def split_cols(p, sizes):
    return jnp.split(p, np.cumsum(sizes)[:-1].tolist(), axis=-1)


def partition(pieces):
    a = len(MLA_SIZES)
    b = a + len(GLA_SIZES)
    s = b + len(SSD_SIZES)
    return pieces[:a], pieces[a:b], pieces[b:s], pieces[s:]


def to_heads(t, h):
    b, l, _ = t.shape
    return t.reshape(b, l, h, -1).transpose(0, 2, 1, 3)


def rope_tables(pos, dim):
    half = dim // 2
    freqs = ROPE_BASE ** (-jnp.arange(half, dtype=F32) / half)
    ang = pos.astype(F32)[:, None] * freqs
    return jnp.cos(ang), jnp.sin(ang)


def rotate(x, tab):
    cos, sin = tab
    h = x.shape[-1] // 2
    x1, x2 = x[..., :h], x[..., h:]
    return jnp.concatenate([x1 * cos - x2 * sin, x1 * sin + x2 * cos], -1)


def axial_rope(x, axial):
    h = x.shape[-1] // 2
    return jnp.concatenate([rotate(x[..., :h], axial[0]), rotate(x[..., h:], axial[1])], -1)


def ret_log_decay():
    return jnp.log1p(-jnp.exp2(-5.0 - jnp.arange(RET_HEADS, dtype=F32)))


PEER_TT = 512
PEER_EB = 512
SUBLANE = 8
_NT = (((1,), (1,)), ((), ()))
NEG_INF = float('-inf')
PEER_NCAND = sum(PEER_TOPK // (b + 1) for b in range(PEER_TOPK))


def _merge_exchanges(lo, hi, r):
    step = r * 2
    if step < hi - lo:
        yield from _merge_exchanges(lo, hi, step)
        yield from _merge_exchanges(lo + r, hi, step)
        yield from ((i, i + r) for i in range(lo + r, hi - r, step))
    else:
        yield (lo, lo + r)


def _sort_exchanges(lo, hi):
    if hi > lo:
        mid = lo + (hi - lo) // 2
        yield from _sort_exchanges(lo, mid)
        yield from _sort_exchanges(mid + 1, hi)
        yield from _merge_exchanges(lo, hi, 1)


def _top_values(s, rows_ref):
    n = s.shape[0] // SUBLANE
    v = [s[t * SUBLANE:(t + 1) * SUBLANE, :] for t in range(n)]
    for i, j in _sort_exchanges(0, n - 1):
        v[i], v[j] = jnp.maximum(v[i], v[j]), jnp.minimum(v[i], v[j])
    for k in range(PEER_TOPK):
        m = jnp.max(v[0], axis=0, keepdims=True)
        rows_ref[k:k + 1, :] = m
        pop = v[0] == m
        live = min(n, PEER_TOPK - k) - 1
        for r in range(live):
            v[r] = jnp.where(pop, v[r + 1] if r + 1 < n else NEG_INF, v[r])


def _peer_score_body(x_ref, wqt_ref, k0_ref, k1_ref, s1_ref, e1_ref, s2_ref, e2_ref,
                     qt_scr, v1_scr, v2_scr, cand_scr):
    half = PEER_DKEY // 2
    qt_scr[...] = lax.dot_general(wqt_ref[...], x_ref[...], _NT, preferred_element_type=F32)

    def strip(h, tg):
        s1 = s1_ref[h, tg]
        s2 = s2_ref[h, tg]
        _top_values(s1, v1_scr)
        _top_values(s2, v2_scr)
        r = 0
        for b in range(PEER_TOPK):
            for a in range(PEER_TOPK // (b + 1)):
                cand_scr[r:r + 1, :] = v1_scr[a:a + 1, :] + v2_scr[b:b + 1, :]
                r += 1
        cand_scr[r:, :] = jnp.full((cand_scr.shape[0] - r, LANE), NEG_INF, F32)
        cand = [cand_scr[t0:t0 + SUBLANE, :] for t0 in range(0, cand_scr.shape[0], SUBLANE)]
        prev = None
        cnt = jnp.zeros((1, LANE), F32)
        tau = jnp.full((1, LANE), NEG_INF, F32)
        for _ in range(PEER_TOPK):
            best = None
            for c in cand:
                cur = c if prev is None else jnp.where(c < prev, c, NEG_INF)
                best = cur if best is None else jnp.maximum(best, cur)
            m = jnp.max(best, axis=0, keepdims=True)
            hits = None
            for c in cand:
                eq = jnp.where(c == m, 1.0, 0.0)
                hits = eq if hits is None else hits + eq
            tau = jnp.where(cnt < PEER_TOPK, m, tau)
            cnt = cnt + jnp.sum(hits, axis=0, keepdims=True)
            prev = m
        m1 = v1_scr[0:1, :]
        m2 = v2_scr[0:1, :]
        zs = None
        for c in cand:
            z = jnp.where(c >= tau, jnp.exp(c - (m1 + m2)), 0.0)
            zs = z if zs is None else zs + z
        z_tot = jnp.sum(zs, axis=0, keepdims=True)
        e1_ref[h, tg] = jnp.exp(s1 - m1) / z_tot
        e2_ref[h, tg] = jnp.exp(s2 - m2)
        c1 = jnp.full(s1.shape, jnp.inf, F32)
        for b in range(PEER_TOPK):
            v2b = v2_scr[b:b + 1, :]
            c1 = jnp.where(s1 + v2b >= tau, v2b, c1)
        s1_ref[h, tg] = c1

    def head(h, carry):
        base = pl.multiple_of(h * PEER_DKEY, PEER_DKEY)
        s1 = jnp.dot(k0_ref[...], qt_scr[pl.ds(base, half), :].astype(BF16), preferred_element_type=F32)
        s2 = jnp.dot(k1_ref[...], qt_scr[pl.ds(base + half, half), :].astype(BF16), preferred_element_type=F32)
        for tg in range(x_ref.shape[0] // LANE):
            s1_ref[h, tg] = s1[:, tg * LANE:(tg + 1) * LANE]
            s2_ref[h, tg] = s2[:, tg * LANE:(tg + 1) * LANE]
        for tg in range(x_ref.shape[0] // LANE):
            strip(h, tg)
        return carry

    lax.fori_loop(0, PEER_HEADS, head, 0)


PEER_JQ = 4
PEER_IQ = 4


def _peer_expert_body(x_ref, u_ref, vt_ref, c1_ref, e1_ref, s2_ref, e2_ref, o_ref,
                      h_scr, g_scr, act_scr, acc_scr, bc1_scr, be1_scr):
    e = pl.program_id(1)
    n_blocks = pl.num_programs(1) - 1
    tt = x_ref.shape[0]
    n_i = u_ref.shape[0] // PEER_NKEYS
    span = PEER_JQ * SUBLANE

    @pl.when(e == 0)
    def _():
        acc_scr[...] = jnp.zeros_like(acc_scr)
        act_scr[...] = jnp.zeros_like(act_scr)

    n_tg = tt // LANE
    m_rows = acc_scr.shape[0] // n_tg
    i_base = jnp.minimum(e, n_blocks - 1) * n_i
    for i_loc in range(n_i):
        for tg in range(n_tg):
            for h in range(PEER_HEADS):
                bc1_scr[i_loc, tg, h] = jnp.broadcast_to(c1_ref[h, tg, pl.ds(i_base + i_loc, 1), :], (SUBLANE, LANE))
                be1_scr[i_loc, tg, h] = jnp.broadcast_to(e1_ref[h, tg, pl.ds(i_base + i_loc, 1), :], (SUBLANE, LANE))
    h_scr[...] = lax.dot_general(u_ref[...], x_ref[...], _NT, preferred_element_type=F32)

    def token_strip(tg, carry):
        m0 = pl.multiple_of(tg * m_rows, m_rows)
        acc_scr[pl.ds(m0, m_rows), :] += jnp.dot(vt_ref[pl.ds(m0, m_rows), :], act_scr[...],
                                                  preferred_element_type=F32)
        for i0 in range(0, n_i, PEER_IQ):
            for j0 in range(0, PEER_NKEYS, span):
                g = [[None] * PEER_JQ for _ in range(PEER_IQ)]
                for h in range(PEER_HEADS):
                    cs = [bc1_scr[i0 + ii, tg, h] for ii in range(PEER_IQ)]
                    bs = [be1_scr[i0 + ii, tg, h] for ii in range(PEER_IQ)]
                    for k in range(PEER_JQ):
                        rows = slice(j0 + k * SUBLANE, j0 + (k + 1) * SUBLANE)
                        s2t = s2_ref[h, tg, rows, :]
                        e2t = e2_ref[h, tg, rows, :]
                        for ii in range(PEER_IQ):
                            gh = jnp.where(s2t >= cs[ii], bs[ii] * e2t, 0.0)
                            g[ii][k] = gh if g[ii][k] is None else g[ii][k] + gh
                for ii in range(PEER_IQ):
                    for k in range(PEER_JQ):
                        r0 = (i0 + ii) * PEER_NKEYS + j0 + k * SUBLANE
                        g_scr[tg, r0:r0 + SUBLANE, :] = g[ii][k]
        return carry

    lax.fori_loop(0, n_tg, token_strip, 0)
    for tg in range(n_tg):
        hv = h_scr[:, tg * LANE:(tg + 1) * LANE]
        act_scr[:, tg * LANE:(tg + 1) * LANE] = (
            0.5 * hv * (1.0 + lax.erf(hv * (2.0 ** -0.5))) * g_scr[tg]).astype(BF16)

    @pl.when(e == n_blocks)
    def _():
        o_ref[...] = acc_scr[...].T


def peer_dense(xm, wqt, k0, k1, u_bf, vt_bf, tt=PEER_TT, eb=PEER_EB):
    t, d = xm.shape
    n_blocks = u_bf.shape[0] // eb
    xb = xm.astype(BF16)
    hk = (PEER_HEADS, t // LANE, PEER_NKEYS, LANE)
    sblk = pl.BlockSpec((PEER_HEADS, tt // LANE, PEER_NKEYS, LANE), lambda i: (0, i, 0, 0))
    c1, e1, s2, e2 = pl.pallas_call(
        _peer_score_body,
        grid=(t // tt,),
        in_specs=[pl.BlockSpec((tt, d), lambda i: (i, 0)),
                  pl.BlockSpec(wqt.shape, lambda i: (0, 0)),
                  pl.BlockSpec(k0.shape, lambda i: (0, 0)),
                  pl.BlockSpec(k1.shape, lambda i: (0, 0))],
        out_specs=[sblk, sblk, sblk, sblk],
        out_shape=[jax.ShapeDtypeStruct(hk, F32)] * 4,
        scratch_shapes=[pltpu.VMEM((PEER_HEADS * PEER_DKEY, tt), F32),
                        pltpu.VMEM((PEER_TOPK, LANE), F32), pltpu.VMEM((PEER_TOPK, LANE), F32),
                        pltpu.VMEM((-(-PEER_NCAND // SUBLANE) * SUBLANE, LANE), F32)],
        compiler_params=pltpu.CompilerParams(vmem_limit_bytes=VMEM_LIMIT),
        name='peer_score',
    )(xb, wqt, k0, k1)
    sblk2 = pl.BlockSpec((PEER_HEADS, tt // LANE, PEER_NKEYS, LANE), lambda i, j: (0, i, 0, 0))
    return pl.pallas_call(
        _peer_expert_body,
        grid=(t // tt, n_blocks + 1),
        in_specs=[pl.BlockSpec((tt, d), lambda i, j: (i, 0)),
                  pl.BlockSpec((eb, d), lambda i, j: (jnp.minimum(j, n_blocks - 1), 0)),
                  pl.BlockSpec((d, eb), lambda i, j: (0, jnp.maximum(j - 1, 0))),
                  sblk2, sblk2, sblk2, sblk2],
        out_specs=pl.BlockSpec((tt, d), lambda i, j: (i, 0)),
        out_shape=jax.ShapeDtypeStruct((t, d), F32),
        scratch_shapes=[pltpu.VMEM((eb, tt), F32), pltpu.VMEM((tt // LANE, eb, LANE), F32), pltpu.VMEM((eb, tt), BF16),
                        pltpu.VMEM((d, tt), F32)]
        + [pltpu.VMEM((eb // PEER_NKEYS, tt // LANE, PEER_HEADS, SUBLANE, LANE), F32)] * 2,
        compiler_params=pltpu.CompilerParams(vmem_limit_bytes=VMEM_LIMIT,
                                             dimension_semantics=("arbitrary", "arbitrary")),
        name='peer_expert',
    )(xb, u_bf, vt_bf, c1, e1, s2, e2)


SCAN_TL = 256
CTX_LEN = 256
_HI = lax.Precision.HIGHEST
_TN = (((0,), (0,)), ((), ()))


def _scan_body(*refs, chunk, heads, groups, per_dim, scaled):
    n_in = 6 if scaled else 5
    of_ref, ob_ref, sf_scr, sb_scr = refs[2 * n_in:]
    tl, dk = refs[0].shape[1], refs[0].shape[2] // groups
    vd = refs[2].shape[2]
    kd, dv = heads * dk, vd // heads

    @pl.when(pl.program_id(1) == 0)
    def _():
        sf_scr[...] = jnp.zeros_like(sf_scr)
        sb_scr[...] = jnp.zeros_like(sb_scr)

    ri = lax.broadcasted_iota(jnp.int32, (chunk, chunk), 0)
    ci = lax.broadcasted_iota(jnp.int32, (chunk, chunk), 1)
    khead = lax.broadcasted_iota(jnp.int32, (1, kd), 1) // dk
    vhead = lax.broadcasted_iota(jnp.int32, (1, vd), 1) // dv
    same_head = (lax.broadcasted_iota(jnp.int32, (vd, kd), 0) // dv
                 == lax.broadcasted_iota(jnp.int32, (vd, kd), 1) // dk)

    def per_key(x8):
        out = jnp.zeros((x8.shape[0], kd), F32)
        for h in range(heads):
            out = jnp.where(khead == h, x8[:, h:h + 1], out)
        return out

    def per_head(t):
        if groups == heads:
            return t
        rep = heads // groups
        return jnp.concatenate([t[:, (h // rep) * dk:(h // rep + 1) * dk] for h in range(heads)], axis=1)

    def sweep(in_refs, o_ref, s_scr, fwd):
        q_ref, k_ref, v_ref, lg_ref, lt_ref = in_refs[:5]
        keep = (ri >= ci) if fwd else (ri <= ci)
        tri = keep.astype(F32)
        order = range(tl // chunk) if fwd else reversed(range(tl // chunk))
        for c in order:
            rows = slice(c * chunk, (c + 1) * chunk)
            q, k, v = per_head(q_ref[0, rows, :]), per_head(k_ref[0, rows, :]), v_ref[0, rows, :]
            if scaled:
                k = k * per_key(in_refs[5][0, rows, :])
            cum = jnp.dot(tri, lg_ref[0, rows, :], precision=_HI, preferred_element_type=F32)
            tot = cum[chunk - 1:chunk, :] if fwd else cum[0:1, :]
            if per_dim:
                e_in, e_out, e_tot = jnp.exp(cum), jnp.exp(tot - cum), jnp.exp(tot)
                mid = cum[chunk // 2:chunk // 2 + 1, :]
                q_att = (q * jnp.exp(cum - mid))
                k_att = (k * jnp.exp(mid - cum)).astype(BF16)
            else:
                e_in, e_out, e_tot = per_key(jnp.exp(cum)), per_key(jnp.exp(tot - cum)), per_key(jnp.exp(tot))
                cum_t = lax.dot_general(lt_ref[0, :, rows], tri, _NT, precision=_HI, preferred_element_type=F32)
                q_att, k_att = q, k.astype(BF16)
            s = s_scr[...]
            o = lax.dot_general((q * e_in).astype(BF16), s.astype(BF16), _NT, preferred_element_type=F32)
            for h in range(heads):
                sc = lax.dot_general(jnp.where(khead == h, q_att, 0.0).astype(BF16), k_att, _NT,
                                     preferred_element_type=F32)
                if per_dim:
                    att = jnp.where(keep, sc, 0.0)
                else:
                    att = sc * jnp.exp(jnp.where(keep, cum[:, h:h + 1] - cum_t[h:h + 1, :], NEG_INF))
                o = o + jnp.dot(att.astype(BF16), jnp.where(vhead == h, v, 0.0).astype(BF16),
                                preferred_element_type=F32)
            o_ref[0, rows, :] = o
            kv = lax.dot_general(v.astype(BF16), (k * e_out).astype(BF16), _TN, preferred_element_type=F32)
            s_scr[...] = s * e_tot + jnp.where(same_head, kv, 0.0)

    sweep(refs[:n_in], of_ref, sf_scr, True)
    sweep(refs[n_in:2 * n_in], ob_ref, sb_scr, False)


def scan_bidir(q, k_f, k_b, v, lg_f, lg_b, lt_f, lt_b, ks_f=None, ks_b=None, *, heads, per_dim, chunk, groups=None):
    b, l, qw = q.shape
    groups = groups or heads
    kd = qw // groups * heads
    vd = v.shape[-1]
    scaled = ks_f is not None
    n, n_ctx = l // SCAN_TL, CTX_LEN // SCAN_TL

    def bwd_group(s):
        return jnp.where(s < n_ctx, n_ctx - 1 - s, n - 1 - (s - n_ctx))

    def specs(group):
        tok = lambda w: pl.BlockSpec((1, SCAN_TL, w), lambda bi, s: (bi, group(s), 0))
        return [tok(qw), tok(qw), tok(vd), tok(LANE),
                pl.BlockSpec((1, SUBLANE, SCAN_TL), lambda bi, s: (bi, 0, group(s)))] + ([tok(LANE)] if scaled else [])

    fwd_group = lambda s: s
    o_f, o_b = pl.pallas_call(
        functools.partial(_scan_body, chunk=chunk, heads=heads, groups=groups, per_dim=per_dim, scaled=scaled),
        grid=(b, n),
        in_specs=specs(fwd_group) + specs(bwd_group),
        out_specs=[pl.BlockSpec((1, SCAN_TL, vd), lambda bi, s: (bi, s, 0)),
                   pl.BlockSpec((1, SCAN_TL, vd), lambda bi, s: (bi, bwd_group(s), 0))],
        out_shape=[jax.ShapeDtypeStruct((b, l, vd), F32)] * 2,
        scratch_shapes=[pltpu.VMEM((vd, kd), F32)] * 2,
        compiler_params=pltpu.CompilerParams(dimension_semantics=("arbitrary", "arbitrary")),
        name='scan_bidir',
    )(*((q, k_f, v, lg_f, lt_f) + ((ks_f,) if scaled else ()) + (q, k_b, v, lg_b, lt_b) + ((ks_b,) if scaled else ())))
    return o_f, o_b


def _ctx_identity(tab):
    cos, sin = tab
    pad = (CTX_LEN, cos.shape[1])
    return jnp.concatenate([jnp.ones(pad, F32), cos], 0), jnp.concatenate([jnp.zeros(pad, F32), sin], 0)


def _pad_lanes(t):
    return jnp.pad(t, ((0, 0), (0, 0), (0, LANE - t.shape[-1])))


def _head_rows(t):
    return jnp.pad(t.transpose(0, 2, 1), ((0, 0), (0, SUBLANE - t.shape[-1]), (0, 0)))


def _split_heads(t, h):
    return t.reshape(t.shape[:-1] + (h, t.shape[-1] // h))


def _rotate_heads(t, heads, tab):
    cos, sin = tab
    return rotate(_split_heads(t, heads), (cos[:, None, :], sin[:, None, :])).reshape(t.shape)


CONV_ROWS = 64


def _conv_silu_body(x_ref, w_ref, b_ref, o_ref, xp_scr):
    l, c = x_ref.shape[1], x_ref.shape[2]
    zeros = jnp.zeros((SUBLANE, c), F32)
    xp_scr[0:SUBLANE, :] = zeros
    xp_scr[SUBLANE:SUBLANE + l, :] = x_ref[0]
    xp_scr[SUBLANE + l:2 * SUBLANE + l, :] = zeros
    w0, w1, w2, bias = w_ref[0:1, :], w_ref[1:2, :], w_ref[2:3, :], b_ref[...]
    for r0 in range(0, l, CONV_ROWS):
        row = r0 + lax.broadcasted_iota(jnp.int32, (CONV_ROWS, 1), 0)
        prev = xp_scr[SUBLANE - 1 + r0:SUBLANE - 1 + r0 + CONV_ROWS, :]
        nxt = xp_scr[SUBLANE + 1 + r0:SUBLANE + 1 + r0 + CONV_ROWS, :]
        if r0 <= CTX_LEN < r0 + CONV_ROWS:
            prev = jnp.where(row == CTX_LEN, 0.0, prev)
        if r0 <= CTX_LEN - 1 < r0 + CONV_ROWS:
            nxt = jnp.where(row == CTX_LEN - 1, 0.0, nxt)
        y = prev * w0 + xp_scr[SUBLANE + r0:SUBLANE + r0 + CONV_ROWS, :] * w1 + nxt * w2 + bias
        o_ref[0, r0:r0 + CONV_ROWS, :] = y * jax.nn.sigmoid(y)


def _segment_conv_silu(x, w, b):
    bsz, l, c = x.shape
    return pl.pallas_call(
        _conv_silu_body,
        grid=(bsz,),
        in_specs=[pl.BlockSpec((1, l, c), lambda i: (i, 0, 0)),
                  pl.BlockSpec(w.shape, lambda i: (0, 0)),
                  pl.BlockSpec((1, c), lambda i: (0, 0))],
        out_specs=pl.BlockSpec((1, l, c), lambda i: (i, 0, 0)),
        out_shape=jax.ShapeDtypeStruct(x.shape, F32),
        scratch_shapes=[pltpu.VMEM((l + 2 * SUBLANE, c), F32)],
        compiler_params=pltpu.CompilerParams(vmem_limit_bytes=VMEM_LIMIT),
        name='conv_silu',
    )(x, w, b.reshape(1, c))


def gla_tokens(p, w_gf, b_gf, w_gb, b_gb):
    q, k, v, r, lr_f, lr_b = p
    no_rows = jnp.zeros((q.shape[0], SUBLANE, q.shape[1]), F32)

    def log_gate(lr, w, b):
        return jax.nn.log_sigmoid(_mm(lr, w) + b) / GLA_GATE_NORM

    return scan_bidir(q * GLA_DK ** -0.5, k, k, v, log_gate(lr_f, w_gf, b_gf), log_gate(lr_b, w_gb, b_gb),
                      no_rows, no_rows, heads=GLA_HEADS, per_dim=True, chunk=CHUNK) + (r,)


def ssd_tokens(p, conv_w, conv_b, dtb_f, dtb_b, alog_f, alog_b):
    z, xbc, dt_f, dt_b = p
    xbc = _segment_conv_silu(xbc, conv_w, conv_b)
    xs, bm, cm = split_cols(xbc, (SSD_HEADS * SSD_HEADDIM, SSD_BC, SSD_BC))

    def direction(dt_raw, dt_bias, a_log):
        dt = jax.nn.softplus(dt_raw + dt_bias)
        return dt, dt * -jnp.exp(a_log)

    dt_f, la_f = direction(dt_f, dtb_f, alog_f)
    dt_b, la_b = direction(dt_b, dtb_b, alog_b)
    return scan_bidir(cm, bm, bm, xs, _pad_lanes(la_f), _pad_lanes(la_b), _head_rows(la_f), _head_rows(la_b),
                      _pad_lanes(dt_f), _pad_lanes(dt_b), heads=SSD_HEADS, groups=SSD_GROUPS, per_dim=False,
                      chunk=SCAN_TL) + (xs, z)


MIX_TM = 512


def _silu(t):
    return t * jax.nn.sigmoid(t)


def _mix_out_body(mla_ref, gf_ref, gb_ref, gr_ref, sf_ref, sb_ref, sx_ref, sz_ref, rf_ref, rb_ref, rg_ref,
                  gla_g_ref, ssd_d_ref, ssd_g_ref, o_ref):
    w = mla_ref.shape[1]
    dv = w // GLA_HEADS
    avg = jnp.where(lax.broadcasted_iota(jnp.int32, (w, w), 0) // dv == lax.broadcasted_iota(jnp.int32, (w, w), 1) // dv,
                    1.0 / dv, 0.0)

    def head_mean(t):
        return jnp.dot(t, avg, precision=_HI, preferred_element_type=F32)

    o_ref[:, 0:w] = mla_ref[...].astype(o_ref.dtype)
    gla = gf_ref[...] + gb_ref[...]
    gla = gla * lax.rsqrt(head_mean(gla * gla) + 1e-6) * gla_g_ref[...] * _silu(gr_ref[...])
    o_ref[:, w:2 * w] = gla.astype(o_ref.dtype)
    ssd = (sf_ref[...] + sb_ref[...] + ssd_d_ref[...] * sx_ref[...]) * _silu(sz_ref[...])
    ssd = ssd * lax.rsqrt(jnp.mean(ssd * ssd, axis=-1, keepdims=True) + 1e-6) * ssd_g_ref[...]
    o_ref[:, 2 * w:3 * w] = ssd.astype(o_ref.dtype)
    ret = rf_ref[...] + rb_ref[...]
    ret = ret - head_mean(ret)
    ret = ret * lax.rsqrt(head_mean(ret * ret) + 1e-6) * _silu(rg_ref[...])
    o_ref[:, 3 * w:4 * w] = ret.astype(o_ref.dtype)


def mix_out(o_mla, gla, ssd, ret, gla_g, ssd_d, ssd_g):
    b, l, w = o_mla.shape
    rows = b * l
    acts = [t.reshape(rows, w) for t in (o_mla,) + tuple(gla) + tuple(ssd) + tuple(ret)]
    params = [gla_g.reshape(1, w), jnp.repeat(ssd_d, SSD_HEADDIM).reshape(1, w), ssd_g.reshape(1, w)]
    tm = math.gcd(MIX_TM, rows)
    out = pl.pallas_call(
        _mix_out_body,
        grid=(rows // tm,),
        in_specs=[pl.BlockSpec((tm, w), lambda i: (i, 0))] * len(acts)
        + [pl.BlockSpec((1, w), lambda i: (0, 0))] * len(params),
        out_specs=pl.BlockSpec((tm, 4 * w), lambda i: (i, 0)),
        out_shape=jax.ShapeDtypeStruct((rows, 4 * w), BF16),
        name='mix_out',
    )(*acts, *params)
    return out.reshape(b, l, 4 * w)


def ret_tokens(p, rope):
    q, k, v, g = p
    q = _rotate_heads(q, RET_HEADS, rope)
    k = _rotate_heads(k * RET_DK ** -0.5, RET_HEADS, rope)
    la = jnp.broadcast_to(ret_log_decay(), q.shape[:2] + (RET_HEADS,))
    return scan_bidir(q, k, k, v, _pad_lanes(la), _pad_lanes(la), _head_rows(la), _head_rows(la),
                      heads=RET_HEADS, per_dim=False, chunk=SCAN_TL) + (g,)


def mla_tokens(p, g_q, w_uq, g_kv, w_uk, w_uv, axial):
    c_q, c_kv, k_r = p
    kv = rms_norm(c_kv, g_kv)
    k_nope = to_heads(_mm(kv, w_uk), MLA_HEADS)
    v = to_heads(_mm(kv, w_uv), MLA_HEADS)
    k_rope = axial_rope(k_r, axial)
    k = jnp.concatenate([k_nope, jnp.broadcast_to(k_rope[:, None], k_nope.shape[:3] + (MLA_ROPE,))], -1)
    q = to_heads(_mm(rms_norm(c_q, g_q), w_uq), MLA_HEADS)
    q = jnp.concatenate([q[..., :MLA_NOPE], axial_rope(q[..., MLA_NOPE:], axial)], -1)
    return attention_tokens(q.astype(BF16), k.astype(BF16), v.astype(BF16))


ATTN_TQ = 256


def _attn_body(q_ref, k_ref, v_ref, o_ref):
    heads, l, dv = v_ref.shape[1], v_ref.shape[2], v_ref.shape[3]
    col = lax.broadcasted_iota(jnp.int32, (1, l), 1)
    visible = (col < CTX_LEN) | (pl.program_id(1) > 0)
    scale = q_ref.shape[-1] ** -0.5
    for h in range(heads):
        s = lax.dot_general(q_ref[0, h], k_ref[0, h], _NT, preferred_element_type=F32) * scale
        s = jnp.where(visible, s, NEG_INF)
        p = jnp.exp(s - jnp.max(s, axis=1, keepdims=True))
        o = jnp.dot(p.astype(BF16), v_ref[0, h], preferred_element_type=F32)
        o_ref[0, :, h * dv:(h + 1) * dv] = o / jnp.sum(p, axis=1, keepdims=True)


def attention_tokens(q, k, v):
    b, h, l, d = q.shape
    dv = v.shape[-1]
    return pl.pallas_call(
        _attn_body,
        grid=(b, l // ATTN_TQ),
        in_specs=[pl.BlockSpec((1, h, ATTN_TQ, d), lambda bi, qi: (bi, 0, qi, 0)),
                  pl.BlockSpec((1, h, l, d), lambda bi, qi: (bi, 0, 0, 0)),
                  pl.BlockSpec((1, h, l, dv), lambda bi, qi: (bi, 0, 0, 0))],
        out_specs=pl.BlockSpec((1, ATTN_TQ, h * dv), lambda bi, qi: (bi, qi, 0)),
        out_shape=jax.ShapeDtypeStruct((b, l, h * dv), F32),
        compiler_params=pltpu.CompilerParams(vmem_limit_bytes=VMEM_LIMIT),
        name='mla_attention',
    )(q, k, v)


RES_TM = CTX_LEN


def _residual_norm_body(x_ref, y_ref, gate_ref, g_ref, b_ref, *rest):
    v = DEEPNORM_ALPHA * x_ref[0] + gate_ref[0, 0] * y_ref[0]
    cen = v - jnp.mean(v, axis=-1, keepdims=True)
    r = cen * lax.rsqrt(jnp.mean(cen * cen, axis=-1, keepdims=True) + 1e-5) * g_ref[...] + b_ref[...]
    if len(rest) == 1:
        rest[0][0] = r
    else:
        shift_ref, scale_ref, r_ref, m_ref = rest
        r_ref[0] = r
        m_ref[0] = (r * (1 + scale_ref[0, 0]) + shift_ref[0, 0]).astype(BF16)


def residual_norm(x, y, gate, g, b, shift=None, scale=None):
    bsz, lo, d = y.shape
    off = (x.shape[1] - lo) // RES_TM
    modulated = shift is not None
    tok = lambda o: pl.BlockSpec((1, RES_TM, d), lambda bi, t: (bi, t + o, 0))
    vspec = pl.BlockSpec((1, 1, 1, d), lambda bi, t: (bi, jnp.minimum(t + off, 1), 0, 0))
    pspec = pl.BlockSpec((1, d), lambda bi, t: (0, 0))
    out_spec = pl.BlockSpec((1, RES_TM, d), lambda bi, t: (bi, t, 0))
    r_shape = jax.ShapeDtypeStruct((bsz, lo, d), F32)
    return pl.pallas_call(
        _residual_norm_body,
        grid=(bsz, lo // RES_TM),
        in_specs=[tok(off), tok(0), vspec, pspec, pspec] + ([vspec, vspec] if modulated else []),
        out_specs=[out_spec, out_spec] if modulated else out_spec,
        out_shape=[r_shape, jax.ShapeDtypeStruct((bsz, lo, d), BF16)] if modulated else r_shape,
        name='residual_norm',
    )(x, y, gate, g.reshape(1, d), b.reshape(1, d), *((shift, scale) if modulated else ()))


def kernel(x, c, ctx, c_ctx, w_ada, b_ada, w_in, mla_g_q, mla_w_uq, mla_g_kv, mla_w_uk, mla_w_uv, gla_w_gf, gla_b_gf, gla_w_gb, gla_b_gb, gla_g, ssd_conv_w, ssd_conv_b, ssd_dt_bias_f, ssd_dt_bias_b, ssd_a_log_f, ssd_a_log_b, ssd_d, ssd_g, w_out, ln1_g, ln1_b, peer_w_q, peer_keys, peer_u, peer_v, ln2_g, ln2_b):
    seq = x.shape[1]
    rows = seq // GRID_W
    row = jnp.repeat(jnp.arange(rows), GRID_W)
    col = jnp.tile(jnp.arange(GRID_W), rows)
    axial = (_ctx_identity(rope_tables(row, MLA_ROPE // 2)), _ctx_identity(rope_tables(col, MLA_ROPE // 2)))
    ret_rope = _ctx_identity(rope_tables(jnp.arange(seq), RET_DK))
    s_lat = jax.nn.silu(c)
    s_ctx = jax.nn.silu(c_ctx)
    bsz = x.shape[0]

    def ada(i):
        lat = s_lat @ w_ada[i] + b_ada[i]
        ctx_v = jnp.broadcast_to(s_ctx @ w_ada[i] + b_ada[i], lat.shape)
        return jnp.stack([ctx_v, lat], 1).reshape(bsz, 2, 6, 1, D_MODEL)

    vecs = [ada(i) for i in range(DEPTH)]
    xa = jnp.concatenate([ctx, x], 1)
    sel = jnp.concatenate([jnp.zeros((CTX_LEN,), jnp.int32), jnp.ones((seq,), jnp.int32)])[None, :, None]
    v0 = vecs[0]
    xin = modulate(xa, jnp.where(sel == 0, v0[:, 0, 0], v0[:, 1, 0]), jnp.where(sel == 0, v0[:, 0, 1], v0[:, 1, 1]))
    for i in range(DEPTH):
        last = i == DEPTH - 1
        vec = vecs[i]
        p_mla, p_gla, p_ssd, p_ret = partition(split_cols(_mm(xin, w_in[i]), IN_SIZES))
        mixed = mix_out(
            mla_tokens(p_mla, mla_g_q[i], mla_w_uq[i], mla_g_kv[i], mla_w_uk[i], mla_w_uv[i], axial),
            gla_tokens(p_gla, gla_w_gf[i], gla_b_gf[i], gla_w_gb[i], gla_b_gb[i]),
            ssd_tokens(p_ssd, ssd_conv_w[i], ssd_conv_b[i], ssd_dt_bias_f[i], ssd_dt_bias_b[i],
                       ssd_a_log_f[i], ssd_a_log_b[i]),
            ret_tokens(p_ret, ret_rope), gla_g[i], ssd_d[i], ssd_g[i])
        y_all = _mm(mixed, w_out[i])
        x1, xm = residual_norm(xa, y_all, vec[:, :, 2], ln1_g[i], ln1_b[i], vec[:, :, 3], vec[:, :, 4])
        toks = xm[:, CTX_LEN:] if last else xm
        f = peer_dense(toks.reshape(-1, D_MODEL), peer_w_q[i].T.astype(BF16), peer_keys[i, 0].astype(BF16),
                       peer_keys[i, 1].astype(BF16), peer_u[i].astype(BF16), peer_v[i].T.astype(BF16))
        f = f.reshape(bsz, -1, D_MODEL)
        if last:
            return residual_norm(x1, f, vec[:, :, 5], ln2_g[i], ln2_b[i])
        nxt = vecs[i + 1]
        xa, xin = residual_norm(x1, f, vec[:, :, 5], ln2_g[i], ln2_b[i], nxt[:, :, 0], nxt[:, :, 1])
```

```python
import functools
import math

import jax
import jax.numpy as jnp
import numpy as np
from jax import lax
from jax.experimental import pallas as pl
from jax.experimental.pallas import tpu as pltpu

D_MODEL = 1024
DEPTH = 4
GRID_W = 64
CHUNK = 64
ROPE_BASE = 10000.0
MLA_HEADS, MLA_NOPE, MLA_ROPE, MLA_V, MLA_Q_LORA, MLA_KV_LORA = 4, 64, 32, 64, 256, 128
GLA_HEADS, GLA_DK, GLA_DV, GLA_GATE_RANK, GLA_GATE_NORM = 4, 32, 64, 16, 16.0
SSD_HEADS, SSD_HEADDIM, SSD_GROUPS, SSD_STATE = 4, 64, 2, 128
RET_HEADS, RET_DK, RET_DV = 4, 32, 64
PEER_HEADS, PEER_NKEYS, PEER_DKEY, PEER_TOPK = 8, 128, 256, 16
DEEPNORM_ALPHA = (2 * DEPTH) ** 0.25

SSD_BC = SSD_GROUPS * SSD_STATE
SSD_XBC = SSD_HEADS * SSD_HEADDIM + 2 * SSD_BC
MLA_SIZES = (MLA_Q_LORA, MLA_KV_LORA, MLA_ROPE)
GLA_SIZES = (GLA_HEADS * GLA_DK, GLA_HEADS * GLA_DK, GLA_HEADS * GLA_DV, GLA_HEADS * GLA_DV, GLA_GATE_RANK, GLA_GATE_RANK)
SSD_SIZES = (SSD_HEADS * SSD_HEADDIM, SSD_XBC, SSD_HEADS, SSD_HEADS)
RET_SIZES = (RET_HEADS * RET_DK, RET_HEADS * RET_DK, RET_HEADS * RET_DV, RET_HEADS * RET_DV)
IN_SIZES = MLA_SIZES + GLA_SIZES + SSD_SIZES + RET_SIZES

F32 = jnp.float32
BF16 = jnp.bfloat16
LANE = 128
VMEM_LIMIT = 48 * 2 ** 20


def _mm_body(a_ref, b_ref, o_ref):
    o_ref[...] = jnp.dot(a_ref[...].astype(BF16), b_ref[...].astype(BF16), preferred_element_type=F32)


def _mm(a, b, tm=512, tn=512):
    lead = a.shape[:-1]
    k = a.shape[-1]
    n = b.shape[-1]
    a2 = a.reshape(-1, k)
    m = a2.shape[0]
    tm = math.gcd(tm, m)
    n_pad = -(-n // LANE) * LANE
    tn = max(t for t in range(LANE, tn + 1, LANE) if n_pad % t == 0)
    if n_pad != n:
        b = jnp.pad(b, ((0, 0), (0, n_pad - n)))
    out = pl.pallas_call(
        _mm_body,
        grid=(m // tm, n_pad // tn),
        in_specs=[pl.BlockSpec((tm, k), lambda i, j: (i, 0)), pl.BlockSpec((k, tn), lambda i, j: (0, j))],
        out_specs=pl.BlockSpec((tm, tn), lambda i, j: (i, j)),
        out_shape=jax.ShapeDtypeStruct((m, n_pad), F32),
    )(a2, b)
    return out[:, :n].reshape(lead + (n,))


def rms_norm(x, g, eps=1e-6):
    return x * lax.rsqrt(jnp.mean(x * x, -1, keepdims=True) + eps) * g


def modulate(x, shift, scale):
    return x * (1 + scale) + shift


def split_cols(p, sizes):
    return jnp.split(p, np.cumsum(sizes)[:-1].tolist(), axis=-1)


def partition(pieces):
    a = len(MLA_SIZES)
    b = a + len(GLA_SIZES)
    s = b + len(SSD_SIZES)
    return pieces[:a], pieces[a:b], pieces[b:s], pieces[s:]


def to_heads(t, h):
    b, l, _ = t.shape
    return t.reshape(b, l, h, -1).transpose(0, 2, 1, 3)


def rope_tables(pos, dim):
    half = dim // 2
    freqs = ROPE_BASE ** (-jnp.arange(half, dtype=F32) / half)
    ang = pos.astype(F32)[:, None] * freqs
    return jnp.cos(ang), jnp.sin(ang)


def rotate(x, tab):
    cos, sin = tab
    h = x.shape[-1] // 2
    x1, x2 = x[..., :h], x[..., h:]
    return jnp.concatenate([x1 * cos - x2 * sin, x1 * sin + x2 * cos], -1)


def axial_rope(x, axial):
    h = x.shape[-1] // 2
    return jnp.concatenate([rotate(x[..., :h], axial[0]), rotate(x[..., h:], axial[1])], -1)


def ret_log_decay():
    return jnp.log1p(-jnp.exp2(-5.0 - jnp.arange(RET_HEADS, dtype=F32)))


PEER_TT = 512
PEER_EB = 512
SUBLANE = 8
_NT = (((1,), (1,)), ((), ()))
NEG_INF = float('-inf')
PEER_NCAND = sum(PEER_TOPK // (b + 1) for b in range(PEER_TOPK))


def _merge_exchanges(lo, hi, r):
    step = r * 2
    if step < hi - lo:
        yield from _merge_exchanges(lo, hi, step)
        yield from _merge_exchanges(lo + r, hi, step)
        yield from ((i, i + r) for i in range(lo + r, hi - r, step))
    else:
        yield (lo, lo + r)


def _sort_exchanges(lo, hi):
    if hi > lo:
        mid = lo + (hi - lo) // 2
        yield from _sort_exchanges(lo, mid)
        yield from _sort_exchanges(mid + 1, hi)
        yield from _merge_exchanges(lo, hi, 1)


def _top_values(s, rows_ref):
    n = s.shape[0] // SUBLANE
    v = [s[t * SUBLANE:(t + 1) * SUBLANE, :] for t in range(n)]
    for i, j in _sort_exchanges(0, n - 1):
        v[i], v[j] = jnp.maximum(v[i], v[j]), jnp.minimum(v[i], v[j])
    for k in range(PEER_TOPK):
        m = jnp.max(v[0], axis=0, keepdims=True)
        rows_ref[k:k + 1, :] = m
        pop = v[0] == m
        live = min(n, PEER_TOPK - k) - 1
        for r in range(live):
            v[r] = jnp.where(pop, v[r + 1] if r + 1 < n else NEG_INF, v[r])


def _peer_score_body(x_ref, wqt_ref, k0_ref, k1_ref, s1_ref, e1_ref, s2_ref, e2_ref,
                     qt_scr, v1_scr, v2_scr, cand_scr):
    half = PEER_DKEY // 2
    qt_scr[...] = lax.dot_general(wqt_ref[...], x_ref[...], _NT, preferred_element_type=F32)

    def strip(h, tg):
        s1 = s1_ref[h, tg]
        s2 = s2_ref[h, tg]
        _top_values(s1, v1_scr)
        _top_values(s2, v2_scr)
        r = 0
        for b in range(PEER_TOPK):
            for a in range(PEER_TOPK // (b + 1)):
                cand_scr[r:r + 1, :] = v1_scr[a:a + 1, :] + v2_scr[b:b + 1, :]
                r += 1
        cand_scr[r:, :] = jnp.full((cand_scr.shape[0] - r, LANE), NEG_INF, F32)
        cand = [cand_scr[t0:t0 + SUBLANE, :] for t0 in range(0, cand_scr.shape[0], SUBLANE)]
        prev = None
        cnt = jnp.zeros((1, LANE), F32)
        tau = jnp.full((1, LANE), NEG_INF, F32)
        for _ in range(PEER_TOPK):
            best = None
            for c in cand:
                cur = c if prev is None else jnp.where(c < prev, c, NEG_INF)
                best = cur if best is None else jnp.maximum(best, cur)
            m = jnp.max(best, axis=0, keepdims=True)
            hits = None
            for c in cand:
                eq = jnp.where(c == m, 1.0, 0.0)
                hits = eq if hits is None else hits + eq
            tau = jnp.where(cnt < PEER_TOPK, m, tau)
            cnt = cnt + jnp.sum(hits, axis=0, keepdims=True)
            prev = m
        m1 = v1_scr[0:1, :]
        m2 = v2_scr[0:1, :]
        zs = None
        for c in cand:
            z = jnp.where(c >= tau, jnp.exp(c - (m1 + m2)), 0.0)
            zs = z if zs is None else zs + z
        z_tot = jnp.sum(zs, axis=0, keepdims=True)
        e1_ref[h, tg] = jnp.exp(s1 - m1) / (2.0 * z_tot)
        e2_ref[h, tg] = jnp.exp(s2 - m2)
        c1 = jnp.full(s1.shape, jnp.inf, F32)
        for b in range(PEER_TOPK):
            v2b = v2_scr[b:b + 1, :]
            c1 = jnp.where(s1 + v2b >= tau, v2b, c1)
        s1_ref[h, tg] = c1

    def head(h, carry):
        base = pl.multiple_of(h * PEER_DKEY, PEER_DKEY)
        s1 = jnp.dot(k0_ref[...], qt_scr[pl.ds(base, half), :].astype(BF16), preferred_element_type=F32)
        s2 = jnp.dot(k1_ref[...], qt_scr[pl.ds(base + half, half), :].astype(BF16), preferred_element_type=F32)
        for tg in range(x_ref.shape[0] // LANE):
            s1_ref[h, tg] = s1[:, tg * LANE:(tg + 1) * LANE]
            s2_ref[h, tg] = s2[:, tg * LANE:(tg + 1) * LANE]
        for tg in range(x_ref.shape[0] // LANE):
            strip(h, tg)
        return carry

    lax.fori_loop(0, PEER_HEADS, head, 0)


PEER_JQ = 4
PEER_IQ = 4


def _peer_expert_body(x_ref, u_ref, vt_ref, c1_ref, e1_ref, s2_ref, e2_ref, o_ref,
                      h_scr, g_scr, act_scr, acc_scr, bc1_scr, be1_scr):
    e = pl.program_id(1)
    n_blocks = pl.num_programs(1) - 1
    tt = x_ref.shape[0]
    n_i = u_ref.shape[0] // PEER_NKEYS
    span = PEER_JQ * SUBLANE

    @pl.when(e == 0)
    def _():
        acc_scr[...] = jnp.zeros_like(acc_scr)
        act_scr[...] = jnp.zeros_like(act_scr)

    n_tg = tt // LANE
    m_rows = acc_scr.shape[0] // n_tg
    i_base = jnp.minimum(e, n_blocks - 1) * n_i
    for i_loc in range(n_i):
        for tg in range(n_tg):
            for h in range(PEER_HEADS):
                bc1_scr[i_loc, tg, h] = jnp.broadcast_to(c1_ref[h, tg, pl.ds(i_base + i_loc, 1), :], (SUBLANE, LANE))
                be1_scr[i_loc, tg, h] = jnp.broadcast_to(e1_ref[h, tg, pl.ds(i_base + i_loc, 1), :], (SUBLANE, LANE))
    h_scr[...] = lax.dot_general(u_ref[...], x_ref[...], _NT, preferred_element_type=F32)

    def token_strip(tg, carry):
        m0 = pl.multiple_of(tg * m_rows, m_rows)
        acc_scr[pl.ds(m0, m_rows), :] += jnp.dot(vt_ref[pl.ds(m0, m_rows), :], act_scr[...],
                                                  preferred_element_type=F32)
        for i0 in range(0, n_i, PEER_IQ):
            for j0 in range(0, PEER_NKEYS, span):
                g = [[None] * PEER_JQ for _ in range(PEER_IQ)]
                for h in range(PEER_HEADS):
                    cs = [bc1_scr[i0 + ii, tg, h] for ii in range(PEER_IQ)]
                    bs = [be1_scr[i0 + ii, tg, h] for ii in range(PEER_IQ)]
                    for k in range(PEER_JQ):
                        rows = slice(j0 + k * SUBLANE, j0 + (k + 1) * SUBLANE)
                        s2t = s2_ref[h, tg, rows, :]
                        e2t = e2_ref[h, tg, rows, :]
                        for ii in range(PEER_IQ):
                            gh = jnp.where(s2t >= cs[ii], bs[ii] * e2t, 0.0)
                            g[ii][k] = gh if g[ii][k] is None else g[ii][k] + gh
                for ii in range(PEER_IQ):
                    for k in range(PEER_JQ):
                        r0 = (i0 + ii) * PEER_NKEYS + j0 + k * SUBLANE
                        g_scr[tg, r0:r0 + SUBLANE, :] = g[ii][k]
        return carry

    lax.fori_loop(0, n_tg, token_strip, 0)
    for tg in range(n_tg):
        hv = h_scr[:, tg * LANE:(tg + 1) * LANE]
        act_scr[:, tg * LANE:(tg + 1) * LANE] = (
            hv * (1.0 + lax.erf(hv * (2.0 ** -0.5))) * g_scr[tg]).astype(BF16)

    @pl.when(e == n_blocks)
    def _():
        o_ref[...] = acc_scr[...].T


def peer_dense(xm, wqt, k0, k1, u_bf, vt_bf, tt=PEER_TT, eb=PEER_EB):
    t, d = xm.shape
    n_blocks = u_bf.shape[0] // eb
    xb = xm.astype(BF16)
    hk = (PEER_HEADS, t // LANE, PEER_NKEYS, LANE)
    sblk = pl.BlockSpec((PEER_HEADS, tt // LANE, PEER_NKEYS, LANE), lambda i: (0, i, 0, 0))
    c1, e1, s2, e2 = pl.pallas_call(
        _peer_score_body,
        grid=(t // tt,),
        in_specs=[pl.BlockSpec((tt, d), lambda i: (i, 0)),
                  pl.BlockSpec(wqt.shape, lambda i: (0, 0)),
                  pl.BlockSpec(k0.shape, lambda i: (0, 0)),
                  pl.BlockSpec(k1.shape, lambda i: (0, 0))],
        out_specs=[sblk, sblk, sblk, sblk],
        out_shape=[jax.ShapeDtypeStruct(hk, F32)] * 4,
        scratch_shapes=[pltpu.VMEM((PEER_HEADS * PEER_DKEY, tt), F32),
                        pltpu.VMEM((PEER_TOPK, LANE), F32), pltpu.VMEM((PEER_TOPK, LANE), F32),
                        pltpu.VMEM((-(-PEER_NCAND // SUBLANE) * SUBLANE, LANE), F32)],
        compiler_params=pltpu.CompilerParams(vmem_limit_bytes=VMEM_LIMIT),
        name='peer_score',
    )(xb, wqt, k0, k1)
    sblk2 = pl.BlockSpec((PEER_HEADS, tt // LANE, PEER_NKEYS, LANE), lambda i, j: (0, i, 0, 0))
    return pl.pallas_call(
        _peer_expert_body,
        grid=(t // tt, n_blocks + 1),
        in_specs=[pl.BlockSpec((tt, d), lambda i, j: (i, 0)),
                  pl.BlockSpec((eb, d), lambda i, j: (jnp.minimum(j, n_blocks - 1), 0)),
                  pl.BlockSpec((d, eb), lambda i, j: (0, jnp.maximum(j - 1, 0))),
                  sblk2, sblk2, sblk2, sblk2],
        out_specs=pl.BlockSpec((tt, d), lambda i, j: (i, 0)),
        out_shape=jax.ShapeDtypeStruct((t, d), F32),
        scratch_shapes=[pltpu.VMEM((eb, tt), F32), pltpu.VMEM((tt // LANE, eb, LANE), F32), pltpu.VMEM((eb, tt), BF16),
                        pltpu.VMEM((d, tt), F32)]
        + [pltpu.VMEM((eb // PEER_NKEYS, tt // LANE, PEER_HEADS, SUBLANE, LANE), F32)] * 2,
        compiler_params=pltpu.CompilerParams(vmem_limit_bytes=VMEM_LIMIT,
                                             dimension_semantics=("arbitrary", "arbitrary")),
        name='peer_expert',
    )(xb, u_bf, vt_bf, c1, e1, s2, e2)


SCAN_TL = 256
CTX_LEN = 256
_HI = lax.Precision.HIGHEST
_TN = (((0,), (0,)), ((), ()))


def _scan_body(*refs, chunk, heads, groups, per_dim, scaled):
    n_in = 6 if scaled else 5
    of_ref, ob_ref, sf_scr, sb_scr = refs[2 * n_in:]
    tl, dk = refs[0].shape[1], refs[0].shape[2] // groups
    vd = refs[2].shape[2]
    kd, dv = heads * dk, vd // heads

    @pl.when(pl.program_id(1) == 0)
    def _():
        sf_scr[...] = jnp.zeros_like(sf_scr)
        sb_scr[...] = jnp.zeros_like(sb_scr)

    ri = lax.broadcasted_iota(jnp.int32, (chunk, chunk), 0)
    ci = lax.broadcasted_iota(jnp.int32, (chunk, chunk), 1)
    khead = lax.broadcasted_iota(jnp.int32, (1, kd), 1) // dk
    vhead = lax.broadcasted_iota(jnp.int32, (1, vd), 1) // dv
    same_head = (lax.broadcasted_iota(jnp.int32, (vd, kd), 0) // dv
                 == lax.broadcasted_iota(jnp.int32, (vd, kd), 1) // dk)

    def per_key(x8):
        out = jnp.zeros((x8.shape[0], kd), F32)
        for h in range(heads):
            out = jnp.where(khead == h, x8[:, h:h + 1], out)
        return out

    def per_head(t):
        if groups == heads:
            return t
        rep = heads // groups
        return jnp.concatenate([t[:, (h // rep) * dk:(h // rep + 1) * dk] for h in range(heads)], axis=1)

    def sweep(in_refs, o_ref, s_scr, fwd):
        q_ref, k_ref, v_ref, lg_ref, lt_ref = in_refs[:5]
        keep = (ri >= ci) if fwd else (ri <= ci)
        tri = keep.astype(F32)
        keep_all = jnp.concatenate([keep] * heads, axis=0)
        order = range(tl // chunk) if fwd else reversed(range(tl // chunk))
        s = s_scr[...]
        for c in order:
            rows = slice(c * chunk, (c + 1) * chunk)
            q, k, v = per_head(q_ref[0, rows, :]), per_head(k_ref[0, rows, :]), v_ref[0, rows, :]
            if scaled:
                k = k * per_key(in_refs[5][0, rows, :])
            cum = jnp.dot(tri, lg_ref[0, rows, :], precision=_HI, preferred_element_type=F32)
            tot = cum[chunk - 1:chunk, :] if fwd else cum[0:1, :]
            if per_dim:
                e_in, e_out, e_tot = jnp.exp(cum), jnp.exp(tot - cum), jnp.exp(tot)
                mid = cum[chunk // 2:chunk // 2 + 1, :]
                q_att = (q * jnp.exp(cum - mid))
                k_att = (k * jnp.exp(mid - cum)).astype(BF16)
            else:
                e_in, e_out, e_tot = per_key(jnp.exp(cum)), per_key(jnp.exp(tot - cum)), per_key(jnp.exp(tot))
                cum_t = lax.dot_general(lt_ref[0, :, rows], tri, _NT, precision=_HI, preferred_element_type=F32)
                q_att, k_att = q, k.astype(BF16)
            o = lax.dot_general((q * e_in).astype(BF16), s.astype(BF16), _NT, preferred_element_type=F32)
            q_stack = jnp.concatenate([jnp.where(khead == h, q_att, 0.0) for h in range(heads)], axis=0)
            sc = lax.dot_general(q_stack.astype(BF16), k_att, _NT, preferred_element_type=F32)
            if per_dim:
                att = jnp.where(keep_all, sc, 0.0)
            else:
                att = jnp.concatenate(
                    [sc[h * chunk:(h + 1) * chunk]
                     * jnp.exp(jnp.where(keep, cum[:, h:h + 1] - cum_t[h:h + 1, :], NEG_INF)) for h in range(heads)],
                    axis=0)
            pv = jnp.dot(att.astype(BF16), v.astype(BF16), preferred_element_type=F32)
            for h in range(heads):
                o = o + jnp.where(vhead == h, pv[h * chunk:(h + 1) * chunk], 0.0)
            o_ref[0, rows, :] = o
            kv = lax.dot_general(v.astype(BF16), (k * e_out).astype(BF16), _TN, preferred_element_type=F32)
            s = s * e_tot + jnp.where(same_head, kv, 0.0)
        s_scr[...] = s

    sweep(refs[:n_in], of_ref, sf_scr, True)
    sweep(refs[n_in:2 * n_in], ob_ref, sb_scr, False)


def scan_bidir(q, k_f, k_b, v, lg_f, lg_b, lt_f, lt_b, ks_f=None, ks_b=None, *, heads, per_dim, chunk, groups=None):
    b, l, qw = q.shape
    groups = groups or heads
    kd = qw // groups * heads
    vd = v.shape[-1]
    scaled = ks_f is not None
    n, n_ctx = l // SCAN_TL, CTX_LEN // SCAN_TL

    def bwd_group(s):
        return jnp.where(s < n_ctx, n_ctx - 1 - s, n - 1 - (s - n_ctx))

    def specs(group):
        tok = lambda w: pl.BlockSpec((1, SCAN_TL, w), lambda bi, s: (bi, group(s), 0))
        return [tok(qw), tok(qw), tok(vd), tok(LANE),
                pl.BlockSpec((1, SUBLANE, SCAN_TL), lambda bi, s: (bi, 0, group(s)))] + ([tok(LANE)] if scaled else [])

    fwd_group = lambda s: s
    o_f, o_b = pl.pallas_call(
        functools.partial(_scan_body, chunk=chunk, heads=heads, groups=groups, per_dim=per_dim, scaled=scaled),
        grid=(b, n),
        in_specs=specs(fwd_group) + specs(bwd_group),
        out_specs=[pl.BlockSpec((1, SCAN_TL, vd), lambda bi, s: (bi, s, 0)),
                   pl.BlockSpec((1, SCAN_TL, vd), lambda bi, s: (bi, bwd_group(s), 0))],
        out_shape=[jax.ShapeDtypeStruct((b, l, vd), F32)] * 2,
        scratch_shapes=[pltpu.VMEM((vd, kd), F32)] * 2,
        compiler_params=pltpu.CompilerParams(dimension_semantics=("arbitrary", "arbitrary")),
        name='scan_bidir',
    )(*((q, k_f, v, lg_f, lt_f) + ((ks_f,) if scaled else ()) + (q, k_b, v, lg_b, lt_b) + ((ks_b,) if scaled else ())))
    return o_f, o_b


def _ctx_identity(tab):
    cos, sin = tab
    pad = (CTX_LEN, cos.shape[1])
    return jnp.concatenate([jnp.ones(pad, F32), cos], 0), jnp.concatenate([jnp.zeros(pad, F32), sin], 0)


def _pad_lanes(t):
    return jnp.pad(t, ((0, 0), (0, 0), (0, LANE - t.shape[-1])))


def _head_rows(t):
    return jnp.pad(t.transpose(0, 2, 1), ((0, 0), (0, SUBLANE - t.shape[-1]), (0, 0)))


def _split_heads(t, h):
    return t.reshape(t.shape[:-1] + (h, t.shape[-1] // h))


def _rotate_heads(t, heads, tab):
    cos, sin = tab
    return rotate(_split_heads(t, heads), (cos[:, None, :], sin[:, None, :])).reshape(t.shape)


CONV_ROWS = 64


def _conv_silu_body(x_ref, w_ref, b_ref, o_ref, xp_scr):
    l, c = x_ref.shape[1], x_ref.shape[2]
    zeros = jnp.zeros((SUBLANE, c), F32)
    xp_scr[0:SUBLANE, :] = zeros
    xp_scr[SUBLANE:SUBLANE + l, :] = x_ref[0]
    xp_scr[SUBLANE + l:2 * SUBLANE + l, :] = zeros
    w0, w1, w2, bias = w_ref[0:1, :], w_ref[1:2, :], w_ref[2:3, :], b_ref[...]
    for r0 in range(0, l, CONV_ROWS):
        row = r0 + lax.broadcasted_iota(jnp.int32, (CONV_ROWS, 1), 0)
        prev = xp_scr[SUBLANE - 1 + r0:SUBLANE - 1 + r0 + CONV_ROWS, :]
        nxt = xp_scr[SUBLANE + 1 + r0:SUBLANE + 1 + r0 + CONV_ROWS, :]
        if r0 <= CTX_LEN < r0 + CONV_ROWS:
            prev = jnp.where(row == CTX_LEN, 0.0, prev)
        if r0 <= CTX_LEN - 1 < r0 + CONV_ROWS:
            nxt = jnp.where(row == CTX_LEN - 1, 0.0, nxt)
        y = prev * w0 + xp_scr[SUBLANE + r0:SUBLANE + r0 + CONV_ROWS, :] * w1 + nxt * w2 + bias
        o_ref[0, r0:r0 + CONV_ROWS, :] = y * jax.nn.sigmoid(y)


def _segment_conv_silu(x, w, b):
    bsz, l, c = x.shape
    return pl.pallas_call(
        _conv_silu_body,
        grid=(bsz,),
        in_specs=[pl.BlockSpec((1, l, c), lambda i: (i, 0, 0)),
                  pl.BlockSpec(w.shape, lambda i: (0, 0)),
                  pl.BlockSpec((1, c), lambda i: (0, 0))],
        out_specs=pl.BlockSpec((1, l, c), lambda i: (i, 0, 0)),
        out_shape=jax.ShapeDtypeStruct(x.shape, F32),
        scratch_shapes=[pltpu.VMEM((l + 2 * SUBLANE, c), F32)],
        compiler_params=pltpu.CompilerParams(vmem_limit_bytes=VMEM_LIMIT),
        name='conv_silu',
    )(x, w, b.reshape(1, c))


def gla_tokens(p, w_gf, b_gf, w_gb, b_gb):
    q, k, v, r, lr_f, lr_b = p
    no_rows = jnp.zeros((q.shape[0], SUBLANE, q.shape[1]), F32)

    def log_gate(lr, w, b):
        return jax.nn.log_sigmoid(_mm(lr, w) + b) / GLA_GATE_NORM

    return scan_bidir(q * GLA_DK ** -0.5, k, k, v, log_gate(lr_f, w_gf, b_gf), log_gate(lr_b, w_gb, b_gb),
                      no_rows, no_rows, heads=GLA_HEADS, per_dim=True, chunk=CHUNK) + (r,)


def ssd_tokens(p, conv_w, conv_b, dtb_f, dtb_b, alog_f, alog_b):
    z, xbc, dt_f, dt_b = p
    xbc = _segment_conv_silu(xbc, conv_w, conv_b)
    xs, bm, cm = split_cols(xbc, (SSD_HEADS * SSD_HEADDIM, SSD_BC, SSD_BC))

    def direction(dt_raw, dt_bias, a_log):
        dt = jax.nn.softplus(dt_raw + dt_bias)
        return dt, dt * -jnp.exp(a_log)

    dt_f, la_f = direction(dt_f, dtb_f, alog_f)
    dt_b, la_b = direction(dt_b, dtb_b, alog_b)
    return scan_bidir(cm, bm, bm, xs, _pad_lanes(la_f), _pad_lanes(la_b), _head_rows(la_f), _head_rows(la_b),
                      _pad_lanes(dt_f), _pad_lanes(dt_b), heads=SSD_HEADS, groups=SSD_GROUPS, per_dim=False,
                      chunk=SCAN_TL) + (xs, z)


MIX_TM = 512


def _silu(t):
    return t * jax.nn.sigmoid(t)


def _mix_out_body(mla_ref, gf_ref, gb_ref, gr_ref, sf_ref, sb_ref, sx_ref, sz_ref, rf_ref, rb_ref, rg_ref,
                  gla_g_ref, ssd_d_ref, ssd_g_ref, o_ref):
    w = mla_ref.shape[1]
    dv = w // GLA_HEADS
    avg = jnp.where(lax.broadcasted_iota(jnp.int32, (w, w), 0) // dv == lax.broadcasted_iota(jnp.int32, (w, w), 1) // dv,
                    1.0 / dv, 0.0)

    def head_mean(t):
        return jnp.dot(t, avg, precision=_HI, preferred_element_type=F32)

    o_ref[:, 0:w] = mla_ref[...].astype(o_ref.dtype)
    gla = gf_ref[...] + gb_ref[...]
    gla = gla * lax.rsqrt(head_mean(gla * gla) + 1e-6) * gla_g_ref[...] * _silu(gr_ref[...])
    o_ref[:, w:2 * w] = gla.astype(o_ref.dtype)
    ssd = (sf_ref[...] + sb_ref[...] + ssd_d_ref[...] * sx_ref[...]) * _silu(sz_ref[...])
    ssd = ssd * lax.rsqrt(jnp.mean(ssd * ssd, axis=-1, keepdims=True) + 1e-6) * ssd_g_ref[...]
    o_ref[:, 2 * w:3 * w] = ssd.astype(o_ref.dtype)
    ret = rf_ref[...] + rb_ref[...]
    ret = ret - head_mean(ret)
    ret = ret * lax.rsqrt(head_mean(ret * ret) + 1e-6) * _silu(rg_ref[...])
    o_ref[:, 3 * w:4 * w] = ret.astype(o_ref.dtype)


def mix_out(o_mla, gla, ssd, ret, gla_g, ssd_d, ssd_g):
    b, l, w = o_mla.shape
    rows = b * l
    acts = [t.reshape(rows, w) for t in (o_mla,) + tuple(gla) + tuple(ssd) + tuple(ret)]
    params = [gla_g.reshape(1, w), jnp.repeat(ssd_d, SSD_HEADDIM).reshape(1, w), ssd_g.reshape(1, w)]
    tm = math.gcd(MIX_TM, rows)
    out = pl.pallas_call(
        _mix_out_body,
        grid=(rows // tm,),
        in_specs=[pl.BlockSpec((tm, w), lambda i: (i, 0))] * len(acts)
        + [pl.BlockSpec((1, w), lambda i: (0, 0))] * len(params),
        out_specs=pl.BlockSpec((tm, 4 * w), lambda i: (i, 0)),
        out_shape=jax.ShapeDtypeStruct((rows, 4 * w), BF16),
        name='mix_out',
    )(*acts, *params)
    return out.reshape(b, l, 4 * w)


def ret_tokens(p, rope):
    q, k, v, g = p
    q = _rotate_heads(q, RET_HEADS, rope)
    k = _rotate_heads(k * RET_DK ** -0.5, RET_HEADS, rope)
    la = jnp.broadcast_to(ret_log_decay(), q.shape[:2] + (RET_HEADS,))
    return scan_bidir(q, k, k, v, _pad_lanes(la), _pad_lanes(la), _head_rows(la), _head_rows(la),
                      heads=RET_HEADS, per_dim=False, chunk=SCAN_TL) + (g,)


def mla_tokens(p, g_q, w_uq, g_kv, w_uk, w_uv, axial):
    c_q, c_kv, k_r = p
    kv = rms_norm(c_kv, g_kv)
    k_nope = to_heads(_mm(kv, w_uk), MLA_HEADS)
    v = to_heads(_mm(kv, w_uv), MLA_HEADS)
    k_rope = axial_rope(k_r, axial)
    k = jnp.concatenate([k_nope, jnp.broadcast_to(k_rope[:, None], k_nope.shape[:3] + (MLA_ROPE,))], -1)
    q = to_heads(_mm(rms_norm(c_q, g_q), w_uq), MLA_HEADS)
    q = jnp.concatenate([q[..., :MLA_NOPE], axial_rope(q[..., MLA_NOPE:], axial)], -1)
    q = q * (MLA_NOPE + MLA_ROPE) ** -0.5
    return attention_tokens(q.astype(BF16), k.astype(BF16), v.astype(BF16))


ATTN_TQ = 256


def _attn_body(q_ref, k_ref, v_ref, o_ref):
    heads, l, dv = v_ref.shape[1], v_ref.shape[2], v_ref.shape[3]

    def attend(n_keys):
        for h in range(heads):
            s = lax.dot_general(q_ref[0, h], k_ref[0, h, :n_keys, :], _NT, preferred_element_type=F32)
            p = jnp.exp(s - jnp.max(s, axis=1, keepdims=True))
            o = jnp.dot(p.astype(BF16), v_ref[0, h, :n_keys, :], preferred_element_type=F32)
            o_ref[0, :, h * dv:(h + 1) * dv] = o / jnp.sum(p, axis=1, keepdims=True)

    is_ctx = pl.program_id(1) < CTX_LEN // ATTN_TQ
    pl.when(is_ctx)(lambda: attend(CTX_LEN))
    pl.when(jnp.logical_not(is_ctx))(lambda: attend(l))


def attention_tokens(q, k, v):
    b, h, l, d = q.shape
    dv = v.shape[-1]
    return pl.pallas_call(
        _attn_body,
        grid=(b, l // ATTN_TQ),
        in_specs=[pl.BlockSpec((1, h, ATTN_TQ, d), lambda bi, qi: (bi, 0, qi, 0)),
                  pl.BlockSpec((1, h, l, d), lambda bi, qi: (bi, 0, 0, 0)),
                  pl.BlockSpec((1, h, l, dv), lambda bi, qi: (bi, 0, 0, 0))],
        out_specs=pl.BlockSpec((1, ATTN_TQ, h * dv), lambda bi, qi: (bi, qi, 0)),
        out_shape=jax.ShapeDtypeStruct((b, l, h * dv), F32),
        compiler_params=pltpu.CompilerParams(vmem_limit_bytes=VMEM_LIMIT),
        name='mla_attention',
    )(q, k, v)


RES_TM = CTX_LEN


def _residual_norm_body(x_ref, y_ref, gate_ref, g_ref, b_ref, *rest):
    v = DEEPNORM_ALPHA * x_ref[0] + gate_ref[0, 0] * y_ref[0]
    cen = v - jnp.mean(v, axis=-1, keepdims=True)
    r = cen * lax.rsqrt(jnp.mean(cen * cen, axis=-1, keepdims=True) + 1e-5) * g_ref[...] + b_ref[...]
    if len(rest) == 1:
        rest[0][0] = r
    else:
        shift_ref, scale_ref, r_ref, m_ref = rest
        r_ref[0] = r
        m_ref[0] = (r * (1 + scale_ref[0, 0]) + shift_ref[0, 0]).astype(BF16)


def residual_norm(x, y, gate, g, b, shift=None, scale=None):
    bsz, lo, d = y.shape
    off = (x.shape[1] - lo) // RES_TM
    modulated = shift is not None
    tok = lambda o: pl.BlockSpec((1, RES_TM, d), lambda bi, t: (bi, t + o, 0))
    vspec = pl.BlockSpec((1, 1, 1, d), lambda bi, t: (bi, jnp.minimum(t + off, 1), 0, 0))
    pspec = pl.BlockSpec((1, d), lambda bi, t: (0, 0))
    out_spec = pl.BlockSpec((1, RES_TM, d), lambda bi, t: (bi, t, 0))
    r_shape = jax.ShapeDtypeStruct((bsz, lo, d), F32)
    return pl.pallas_call(
        _residual_norm_body,
        grid=(bsz, lo // RES_TM),
        in_specs=[tok(off), tok(0), vspec, pspec, pspec] + ([vspec, vspec] if modulated else []),
        out_specs=[out_spec, out_spec] if modulated else out_spec,
        out_shape=[r_shape, jax.ShapeDtypeStruct((bsz, lo, d), BF16)] if modulated else r_shape,
        name='residual_norm',
    )(x, y, gate, g.reshape(1, d), b.reshape(1, d), *((shift, scale) if modulated else ()))


def kernel(x, c, ctx, c_ctx, w_ada, b_ada, w_in, mla_g_q, mla_w_uq, mla_g_kv, mla_w_uk, mla_w_uv, gla_w_gf, gla_b_gf, gla_w_gb, gla_b_gb, gla_g, ssd_conv_w, ssd_conv_b, ssd_dt_bias_f, ssd_dt_bias_b, ssd_a_log_f, ssd_a_log_b, ssd_d, ssd_g, w_out, ln1_g, ln1_b, peer_w_q, peer_keys, peer_u, peer_v, ln2_g, ln2_b):
    seq = x.shape[1]
    rows = seq // GRID_W
    row = jnp.repeat(jnp.arange(rows), GRID_W)
    col = jnp.tile(jnp.arange(GRID_W), rows)
    axial = (_ctx_identity(rope_tables(row, MLA_ROPE // 2)), _ctx_identity(rope_tables(col, MLA_ROPE // 2)))
    ret_rope = _ctx_identity(rope_tables(jnp.arange(seq), RET_DK))
    s_lat = jax.nn.silu(c)
    s_ctx = jax.nn.silu(c_ctx)
    bsz = x.shape[0]

    def ada(i):
        lat = s_lat @ w_ada[i] + b_ada[i]
        ctx_v = jnp.broadcast_to(s_ctx @ w_ada[i] + b_ada[i], lat.shape)
        return jnp.stack([ctx_v, lat], 1).reshape(bsz, 2, 6, 1, D_MODEL)

    vecs = [ada(i) for i in range(DEPTH)]
    xa = jnp.concatenate([ctx, x], 1)
    sel = jnp.concatenate([jnp.zeros((CTX_LEN,), jnp.int32), jnp.ones((seq,), jnp.int32)])[None, :, None]
    v0 = vecs[0]
    xin = modulate(xa, jnp.where(sel == 0, v0[:, 0, 0], v0[:, 1, 0]), jnp.where(sel == 0, v0[:, 0, 1], v0[:, 1, 1]))
    for i in range(DEPTH):
        last = i == DEPTH - 1
        vec = vecs[i]
        p_mla, p_gla, p_ssd, p_ret = partition(split_cols(_mm(xin, w_in[i]), IN_SIZES))
        mixed = mix_out(
            mla_tokens(p_mla, mla_g_q[i], mla_w_uq[i], mla_g_kv[i], mla_w_uk[i], mla_w_uv[i], axial),
            gla_tokens(p_gla, gla_w_gf[i], gla_b_gf[i], gla_w_gb[i], gla_b_gb[i]),
            ssd_tokens(p_ssd, ssd_conv_w[i], ssd_conv_b[i], ssd_dt_bias_f[i], ssd_dt_bias_b[i],
                       ssd_a_log_f[i], ssd_a_log_b[i]),
            ret_tokens(p_ret, ret_rope), gla_g[i], ssd_d[i], ssd_g[i])
        y_all = _mm(mixed, w_out[i])
        x1, xm = residual_norm(xa, y_all, vec[:, :, 2], ln1_g[i], ln1_b[i], vec[:, :, 3], vec[:, :, 4])
        toks = xm[:, CTX_LEN:] if last else xm
        f = peer_dense(toks.reshape(-1, D_MODEL), peer_w_q[i].T.astype(BF16), peer_keys[i, 0].astype(BF16),
                       peer_keys[i, 1].astype(BF16), peer_u[i].astype(BF16), peer_v[i].T.astype(BF16))
        f = f.reshape(bsz, -1, D_MODEL)
        if last:
            return residual_norm(x1, f, vec[:, :, 5], ln2_g[i], ln2_b[i])
        nxt = vecs[i + 1]
        xa, xin = residual_norm(x1, f, vec[:, :, 5], ln2_g[i], ln2_b[i], nxt[:, :, 0], nxt[:, :, 1])
```

```python
import functools
import math

import jax
import jax.numpy as jnp
import numpy as np
from jax import lax
from jax.experimental import pallas as pl
from jax.experimental.pallas import tpu as pltpu

D_MODEL = 1024
DEPTH = 4
GRID_W = 64
CHUNK = 64
ROPE_BASE = 10000.0
MLA_HEADS, MLA_NOPE, MLA_ROPE, MLA_V, MLA_Q_LORA, MLA_KV_LORA = 4, 64, 32, 64, 256, 128
GLA_HEADS, GLA_DK, GLA_DV, GLA_GATE_RANK, GLA_GATE_NORM = 4, 32, 64, 16, 16.0
SSD_HEADS, SSD_HEADDIM, SSD_GROUPS, SSD_STATE = 4, 64, 2, 128
RET_HEADS, RET_DK, RET_DV = 4, 32, 64
PEER_HEADS, PEER_NKEYS, PEER_DKEY, PEER_TOPK = 8, 128, 256, 16
DEEPNORM_ALPHA = (2 * DEPTH) ** 0.25

SSD_BC = SSD_GROUPS * SSD_STATE
SSD_XBC = SSD_HEADS * SSD_HEADDIM + 2 * SSD_BC
MLA_SIZES = (MLA_Q_LORA, MLA_KV_LORA, MLA_ROPE)
GLA_SIZES = (GLA_HEADS * GLA_DK, GLA_HEADS * GLA_DK, GLA_HEADS * GLA_DV, GLA_HEADS * GLA_DV, GLA_GATE_RANK, GLA_GATE_RANK)
SSD_SIZES = (SSD_HEADS * SSD_HEADDIM, SSD_XBC, SSD_HEADS, SSD_HEADS)
RET_SIZES = (RET_HEADS * RET_DK, RET_HEADS * RET_DK, RET_HEADS * RET_DV, RET_HEADS * RET_DV)
IN_SIZES = MLA_SIZES + GLA_SIZES + SSD_SIZES + RET_SIZES

F32 = jnp.float32
BF16 = jnp.bfloat16
LANE = 128
VMEM_LIMIT = 48 * 2 ** 20


def _mm_body(a_ref, b_ref, o_ref):
    o_ref[...] = jnp.dot(a_ref[...].astype(BF16), b_ref[...].astype(BF16), preferred_element_type=F32)


def _mm(a, b, tm=1024, tn=1024):
    lead = a.shape[:-1]
    k = a.shape[-1]
    n = b.shape[-1]
    a2 = a.reshape(-1, k)
    m = a2.shape[0]
    tm = math.gcd(tm, m)
    n_pad = -(-n // LANE) * LANE
    tn = max(t for t in range(LANE, tn + 1, LANE) if n_pad % t == 0)
    b = b.astype(BF16)
    if n_pad != n:
        b = jnp.pad(b, ((0, 0), (0, n_pad - n)))
    out = pl.pallas_call(
        _mm_body,
        grid=(m // tm, n_pad // tn),
        in_specs=[pl.BlockSpec((tm, k), lambda i, j: (i, 0)), pl.BlockSpec((k, tn), lambda i, j: (0, j))],
        out_specs=pl.BlockSpec((tm, tn), lambda i, j: (i, j)),
        out_shape=jax.ShapeDtypeStruct((m, n_pad), F32),
        compiler_params=pltpu.CompilerParams(vmem_limit_bytes=VMEM_LIMIT),
        name='matmul',
    )(a2, b)
    return out[:, :n].reshape(lead + (n,))


def rms_norm(x, g, eps=1e-6):
    return x * lax.rsqrt(jnp.mean(x * x, -1, keepdims=True) + eps) * g


def modulate(x, shift, scale):
    return x * (1 + scale) + shift


def split_cols(p, sizes):
    return jnp.split(p, np.cumsum(sizes)[:-1].tolist(), axis=-1)


def partition(pieces):
    a = len(MLA_SIZES)
    b = a + len(GLA_SIZES)
    s = b + len(SSD_SIZES)
    return pieces[:a], pieces[a:b], pieces[b:s], pieces[s:]


def to_heads(t, h):
    b, l, _ = t.shape
    return t.reshape(b, l, h, -1).transpose(0, 2, 1, 3)


def rope_tables(pos, dim):
    half = dim // 2
    freqs = ROPE_BASE ** (-jnp.arange(half, dtype=F32) / half)
    ang = pos.astype(F32)[:, None] * freqs
    return jnp.cos(ang), jnp.sin(ang)


def rotate(x, tab):
    cos, sin = tab
    h = x.shape[-1] // 2
    x1, x2 = x[..., :h], x[..., h:]
    return jnp.concatenate([x1 * cos - x2 * sin, x1 * sin + x2 * cos], -1)


def axial_rope(x, axial):
    h = x.shape[-1] // 2
    return jnp.concatenate([rotate(x[..., :h], axial[0]), rotate(x[..., h:], axial[1])], -1)


def ret_log_decay():
    return jnp.log1p(-jnp.exp2(-5.0 - jnp.arange(RET_HEADS, dtype=F32)))


PEER_TT = 512
PEER_EB = 512
SUBLANE = 8
_NT = (((1,), (1,)), ((), ()))
NEG_INF = float('-inf')
PEER_NCAND = sum(PEER_TOPK // (b + 1) for b in range(PEER_TOPK))


def _merge_exchanges(lo, hi, r):
    step = r * 2
    if step < hi - lo:
        yield from _merge_exchanges(lo, hi, step)
        yield from _merge_exchanges(lo + r, hi, step)
        yield from ((i, i + r) for i in range(lo + r, hi - r, step))
    else:
        yield (lo, lo + r)


def _sort_exchanges(lo, hi):
    if hi > lo:
        mid = lo + (hi - lo) // 2
        yield from _sort_exchanges(lo, mid)
        yield from _sort_exchanges(mid + 1, hi)
        yield from _merge_exchanges(lo, hi, 1)


def _top_values(s, rows_ref):
    n = s.shape[0] // SUBLANE
    v = [s[t * SUBLANE:(t + 1) * SUBLANE, :] for t in range(n)]
    for i, j in _sort_exchanges(0, n - 1):
        v[i], v[j] = jnp.maximum(v[i], v[j]), jnp.minimum(v[i], v[j])
    for k in range(PEER_TOPK):
        m = jnp.max(v[0], axis=0, keepdims=True)
        rows_ref[k:k + 1, :] = m
        pop = v[0] == m
        live = min(n, PEER_TOPK - k) - 1
        for r in range(live):
            v[r] = jnp.where(pop, v[r + 1] if r + 1 < n else NEG_INF, v[r])


def _peer_score_body(x_ref, wqt_ref, k0_ref, k1_ref, s1_ref, e1_ref, s2_ref, e2_ref,
                     qt_scr, v1_scr, v2_scr, cand_scr):
    half = PEER_DKEY // 2
    qt_scr[...] = lax.dot_general(wqt_ref[...], x_ref[...], _NT, preferred_element_type=F32)

    def strip(h, tg):
        s1 = s1_ref[h, tg]
        s2 = s2_ref[h, tg]
        _top_values(s1, v1_scr)
        _top_values(s2, v2_scr)
        r = 0
        for b in range(PEER_TOPK):
            for a in range(PEER_TOPK // (b + 1)):
                cand_scr[r:r + 1, :] = v1_scr[a:a + 1, :] + v2_scr[b:b + 1, :]
                r += 1
        cand_scr[r:, :] = jnp.full((cand_scr.shape[0] - r, LANE), NEG_INF, F32)
        cand = [cand_scr[t0:t0 + SUBLANE, :] for t0 in range(0, cand_scr.shape[0], SUBLANE)]
        prev = None
        cnt = jnp.zeros((1, LANE), F32)
        tau = jnp.full((1, LANE), NEG_INF, F32)
        for _ in range(PEER_TOPK):
            best = None
            for c in cand:
                cur = c if prev is None else jnp.where(c < prev, c, NEG_INF)
                best = cur if best is None else jnp.maximum(best, cur)
            m = jnp.max(best, axis=0, keepdims=True)
            hits = None
            for c in cand:
                eq = jnp.where(c == m, 1.0, 0.0)
                hits = eq if hits is None else hits + eq
            tau = jnp.where(cnt < PEER_TOPK, m, tau)
            cnt = cnt + jnp.sum(hits, axis=0, keepdims=True)
            prev = m
        m1 = v1_scr[0:1, :]
        m2 = v2_scr[0:1, :]
        zs = None
        for c in cand:
            z = jnp.where(c >= tau, jnp.exp(c - (m1 + m2)), 0.0)
            zs = z if zs is None else zs + z
        z_tot = jnp.sum(zs, axis=0, keepdims=True)
        e1_ref[h, tg] = jnp.exp(s1 - m1) / (2.0 * z_tot)
        e2_ref[h, tg] = jnp.exp(s2 - m2)
        c1 = jnp.full(s1.shape, jnp.inf, F32)
        for b in range(PEER_TOPK):
            v2b = v2_scr[b:b + 1, :]
            c1 = jnp.where(s1 + v2b >= tau, v2b, c1)
        s1_ref[h, tg] = c1

    def head(h, carry):
        base = pl.multiple_of(h * PEER_DKEY, PEER_DKEY)
        s1 = jnp.dot(k0_ref[...], qt_scr[pl.ds(base, half), :].astype(BF16), preferred_element_type=F32)
        s2 = jnp.dot(k1_ref[...], qt_scr[pl.ds(base + half, half), :].astype(BF16), preferred_element_type=F32)
        for tg in range(x_ref.shape[0] // LANE):
            s1_ref[h, tg] = s1[:, tg * LANE:(tg + 1) * LANE]
            s2_ref[h, tg] = s2[:, tg * LANE:(tg + 1) * LANE]
        for tg in range(x_ref.shape[0] // LANE):
            strip(h, tg)
        return carry

    lax.fori_loop(0, PEER_HEADS, head, 0)


PEER_JQ = 4
PEER_IQ = 4


def _peer_expert_body(x_ref, u_ref, vt_ref, c1_ref, e1_ref, s2_ref, e2_ref, o_ref,
                      h_scr, g_scr, act_scr, acc_scr, bc1_scr, be1_scr):
    e = pl.program_id(1)
    n_blocks = pl.num_programs(1) - 1
    tt = x_ref.shape[0]
    n_i = u_ref.shape[0] // PEER_NKEYS
    span = PEER_JQ * SUBLANE

    @pl.when(e == 0)
    def _():
        acc_scr[...] = jnp.zeros_like(acc_scr)
        act_scr[...] = jnp.zeros_like(act_scr)

    n_tg = tt // LANE
    m_rows = acc_scr.shape[0] // n_tg
    i_base = jnp.minimum(e, n_blocks - 1) * n_i
    for i_loc in range(n_i):
        for tg in range(n_tg):
            for h in range(PEER_HEADS):
                bc1_scr[i_loc, tg, h] = jnp.broadcast_to(c1_ref[h, tg, pl.ds(i_base + i_loc, 1), :], (SUBLANE, LANE))
                be1_scr[i_loc, tg, h] = jnp.broadcast_to(e1_ref[h, tg, pl.ds(i_base + i_loc, 1), :], (SUBLANE, LANE))
    h_scr[...] = lax.dot_general(u_ref[...], x_ref[...], _NT, preferred_element_type=F32)

    def token_strip(tg, carry):
        m0 = pl.multiple_of(tg * m_rows, m_rows)
        acc_scr[pl.ds(m0, m_rows), :] += jnp.dot(vt_ref[pl.ds(m0, m_rows), :], act_scr[...],
                                                  preferred_element_type=F32)
        for i0 in range(0, n_i, PEER_IQ):
            for j0 in range(0, PEER_NKEYS, span):
                g = [[None] * PEER_JQ for _ in range(PEER_IQ)]
                for h in range(PEER_HEADS):
                    cs = [bc1_scr[i0 + ii, tg, h] for ii in range(PEER_IQ)]
                    bs = [be1_scr[i0 + ii, tg, h] for ii in range(PEER_IQ)]
                    for k in range(PEER_JQ):
                        rows = slice(j0 + k * SUBLANE, j0 + (k + 1) * SUBLANE)
                        s2t = s2_ref[h, tg, rows, :]
                        e2t = e2_ref[h, tg, rows, :]
                        for ii in range(PEER_IQ):
                            gh = jnp.where(s2t >= cs[ii], bs[ii] * e2t, 0.0)
                            g[ii][k] = gh if g[ii][k] is None else g[ii][k] + gh
                for ii in range(PEER_IQ):
                    for k in range(PEER_JQ):
                        r0 = (i0 + ii) * PEER_NKEYS + j0 + k * SUBLANE
                        g_scr[tg, r0:r0 + SUBLANE, :] = g[ii][k]
        return carry

    lax.fori_loop(0, n_tg, token_strip, 0)
    for tg in range(n_tg):
        hv = h_scr[:, tg * LANE:(tg + 1) * LANE]
        act_scr[:, tg * LANE:(tg + 1) * LANE] = (
            hv * (1.0 + lax.erf(hv * (2.0 ** -0.5))) * g_scr[tg]).astype(BF16)

    @pl.when(e == n_blocks)
    def _():
        o_ref[...] = acc_scr[...].T


def peer_dense(xm, wqt, k0, k1, u_bf, vt_bf, tt=PEER_TT, eb=PEER_EB):
    t, d = xm.shape
    n_blocks = u_bf.shape[0] // eb
    xb = xm.astype(BF16)
    hk = (PEER_HEADS, t // LANE, PEER_NKEYS, LANE)
    sblk = pl.BlockSpec((PEER_HEADS, tt // LANE, PEER_NKEYS, LANE), lambda i: (0, i, 0, 0))
    c1, e1, s2, e2 = pl.pallas_call(
        _peer_score_body,
        grid=(t // tt,),
        in_specs=[pl.BlockSpec((tt, d), lambda i: (i, 0)),
                  pl.BlockSpec(wqt.shape, lambda i: (0, 0)),
                  pl.BlockSpec(k0.shape, lambda i: (0, 0)),
                  pl.BlockSpec(k1.shape, lambda i: (0, 0))],
        out_specs=[sblk, sblk, sblk, sblk],
        out_shape=[jax.ShapeDtypeStruct(hk, F32)] * 4,
        scratch_shapes=[pltpu.VMEM((PEER_HEADS * PEER_DKEY, tt), F32),
                        pltpu.VMEM((PEER_TOPK, LANE), F32), pltpu.VMEM((PEER_TOPK, LANE), F32),
                        pltpu.VMEM((-(-PEER_NCAND // SUBLANE) * SUBLANE, LANE), F32)],
        compiler_params=pltpu.CompilerParams(vmem_limit_bytes=VMEM_LIMIT),
        name='peer_score',
    )(xb, wqt, k0, k1)
    sblk2 = pl.BlockSpec((PEER_HEADS, tt // LANE, PEER_NKEYS, LANE), lambda i, j: (0, i, 0, 0))
    return pl.pallas_call(
        _peer_expert_body,
        grid=(t // tt, n_blocks + 1),
        in_specs=[pl.BlockSpec((tt, d), lambda i, j: (i, 0)),
                  pl.BlockSpec((eb, d), lambda i, j: (jnp.minimum(j, n_blocks - 1), 0)),
                  pl.BlockSpec((d, eb), lambda i, j: (0, jnp.maximum(j - 1, 0))),
                  sblk2, sblk2, sblk2, sblk2],
        out_specs=pl.BlockSpec((tt, d), lambda i, j: (i, 0)),
        out_shape=jax.ShapeDtypeStruct((t, d), F32),
        scratch_shapes=[pltpu.VMEM((eb, tt), F32), pltpu.VMEM((tt // LANE, eb, LANE), F32), pltpu.VMEM((eb, tt), BF16),
                        pltpu.VMEM((d, tt), F32)]
        + [pltpu.VMEM((eb // PEER_NKEYS, tt // LANE, PEER_HEADS, SUBLANE, LANE), F32)] * 2,
        compiler_params=pltpu.CompilerParams(vmem_limit_bytes=VMEM_LIMIT,
                                             dimension_semantics=("arbitrary", "arbitrary")),
        name='peer_expert',
    )(xb, u_bf, vt_bf, c1, e1, s2, e2)


SCAN_TL = 256
CTX_LEN = 256
_HI = lax.Precision.HIGHEST
_TN = (((0,), (0,)), ((), ()))


def _scan_body(*refs, chunk, heads, groups, per_dim, scaled):
    n_in = 6 if scaled else 5
    of_ref, ob_ref, sf_scr, sb_scr = refs[2 * n_in:]
    tl, dk = refs[0].shape[1], refs[0].shape[2] // groups
    vd = refs[2].shape[2]
    kd, dv = heads * dk, vd // heads

    @pl.when(pl.program_id(1) == 0)
    def _():
        sf_scr[...] = jnp.zeros_like(sf_scr)
        sb_scr[...] = jnp.zeros_like(sb_scr)

    ri = lax.broadcasted_iota(jnp.int32, (chunk, chunk), 0)
    ci = lax.broadcasted_iota(jnp.int32, (chunk, chunk), 1)
    khead = lax.broadcasted_iota(jnp.int32, (1, kd), 1) // dk
    vhead = lax.broadcasted_iota(jnp.int32, (1, vd), 1) // dv
    same_head = (lax.broadcasted_iota(jnp.int32, (vd, kd), 0) // dv
                 == lax.broadcasted_iota(jnp.int32, (vd, kd), 1) // dk)

    def per_key(x8):
        out = jnp.zeros((x8.shape[0], kd), F32)
        for h in range(heads):
            out = jnp.where(khead == h, x8[:, h:h + 1], out)
        return out

    def per_head(t):
        if groups == heads:
            return t
        rep = heads // groups
        return jnp.concatenate([t[:, (h // rep) * dk:(h // rep + 1) * dk] for h in range(heads)], axis=1)

    def sweep(in_refs, o_ref, s_scr, fwd):
        q_ref, k_ref, v_ref, lg_ref, lt_ref = in_refs[:5]
        keep = (ri >= ci) if fwd else (ri <= ci)
        tri = keep.astype(F32)
        keep_all = jnp.concatenate([keep] * heads, axis=0)
        order = range(tl // chunk) if fwd else reversed(range(tl // chunk))
        s = s_scr[...]
        for c in order:
            rows = slice(c * chunk, (c + 1) * chunk)
            q, k, v = per_head(q_ref[0, rows, :]), per_head(k_ref[0, rows, :]), v_ref[0, rows, :]
            if scaled:
                k = k * per_key(in_refs[5][0, rows, :])
            cum = jnp.dot(tri, lg_ref[0, rows, :], precision=_HI, preferred_element_type=F32)
            tot = cum[chunk - 1:chunk, :] if fwd else cum[0:1, :]
            if per_dim:
                e_in, e_out, e_tot = jnp.exp(cum), jnp.exp(tot - cum), jnp.exp(tot)
                mid = cum[chunk // 2:chunk // 2 + 1, :]
                q_att = (q * jnp.exp(cum - mid))
                k_att = (k * jnp.exp(mid - cum)).astype(BF16)
            else:
                e_in, e_out, e_tot = per_key(jnp.exp(cum)), per_key(jnp.exp(tot - cum)), per_key(jnp.exp(tot))
                cum_t = lax.dot_general(lt_ref[0, :, rows], tri, _NT, precision=_HI, preferred_element_type=F32)
                q_att, k_att = q, k.astype(BF16)
            o = lax.dot_general((q * e_in).astype(BF16), s.astype(BF16), _NT, preferred_element_type=F32)
            q_stack = jnp.concatenate([jnp.where(khead == h, q_att, 0.0) for h in range(heads)], axis=0)
            sc = lax.dot_general(q_stack.astype(BF16), k_att, _NT, preferred_element_type=F32)
            if per_dim:
                att = jnp.where(keep_all, sc, 0.0)
            else:
                att = jnp.concatenate(
                    [sc[h * chunk:(h + 1) * chunk]
                     * jnp.exp(jnp.where(keep, cum[:, h:h + 1] - cum_t[h:h + 1, :], NEG_INF)) for h in range(heads)],
                    axis=0)
            pv = jnp.dot(att.astype(BF16), v.astype(BF16), preferred_element_type=F32)
            for h in range(heads):
                o = o + jnp.where(vhead == h, pv[h * chunk:(h + 1) * chunk], 0.0)
            o_ref[0, rows, :] = o
            kv = lax.dot_general(v.astype(BF16), (k * e_out).astype(BF16), _TN, preferred_element_type=F32)
            s = s * e_tot + jnp.where(same_head, kv, 0.0)
        s_scr[...] = s

    sweep(refs[:n_in], of_ref, sf_scr, True)
    sweep(refs[n_in:2 * n_in], ob_ref, sb_scr, False)


def scan_bidir(q, k_f, k_b, v, lg_f, lg_b, lt_f, lt_b, ks_f=None, ks_b=None, *, heads, per_dim, chunk, groups=None):
    b, l, qw = q.shape
    groups = groups or heads
    kd = qw // groups * heads
    vd = v.shape[-1]
    scaled = ks_f is not None
    n, n_ctx = l // SCAN_TL, CTX_LEN // SCAN_TL

    def bwd_group(s):
        return jnp.where(s < n_ctx, n_ctx - 1 - s, n - 1 - (s - n_ctx))

    def specs(group):
        tok = lambda w: pl.BlockSpec((1, SCAN_TL, w), lambda bi, s: (bi, group(s), 0))
        return [tok(qw), tok(qw), tok(vd), tok(LANE),
                pl.BlockSpec((1, SUBLANE, SCAN_TL), lambda bi, s: (bi, 0, group(s)))] + ([tok(LANE)] if scaled else [])

    fwd_group = lambda s: s
    o_f, o_b = pl.pallas_call(
        functools.partial(_scan_body, chunk=chunk, heads=heads, groups=groups, per_dim=per_dim, scaled=scaled),
        grid=(b, n),
        in_specs=specs(fwd_group) + specs(bwd_group),
        out_specs=[pl.BlockSpec((1, SCAN_TL, vd), lambda bi, s: (bi, s, 0)),
                   pl.BlockSpec((1, SCAN_TL, vd), lambda bi, s: (bi, bwd_group(s), 0))],
        out_shape=[jax.ShapeDtypeStruct((b, l, vd), F32)] * 2,
        scratch_shapes=[pltpu.VMEM((vd, kd), F32)] * 2,
        compiler_params=pltpu.CompilerParams(dimension_semantics=("arbitrary", "arbitrary")),
        name='scan_bidir',
    )(*((q, k_f, v, lg_f, lt_f) + ((ks_f,) if scaled else ()) + (q, k_b, v, lg_b, lt_b) + ((ks_b,) if scaled else ())))
    return o_f, o_b


def _ctx_identity(tab):
    cos, sin = tab
    pad = (CTX_LEN, cos.shape[1])
    return jnp.concatenate([jnp.ones(pad, F32), cos], 0), jnp.concatenate([jnp.zeros(pad, F32), sin], 0)


def _pad_lanes(t):
    return jnp.pad(t, ((0, 0), (0, 0), (0, LANE - t.shape[-1])))


def _head_rows(t):
    return jnp.pad(t.transpose(0, 2, 1), ((0, 0), (0, SUBLANE - t.shape[-1]), (0, 0)))


def _split_heads(t, h):
    return t.reshape(t.shape[:-1] + (h, t.shape[-1] // h))


def _rotate_heads(t, heads, tab):
    cos, sin = tab
    return rotate(_split_heads(t, heads), (cos[:, None, :], sin[:, None, :])).reshape(t.shape)


CONV_ROWS = 64


def _conv_silu_body(x_ref, w_ref, b_ref, o_ref, xp_scr):
    l, c = x_ref.shape[1], x_ref.shape[2]
    zeros = jnp.zeros((SUBLANE, c), F32)
    xp_scr[0:SUBLANE, :] = zeros
    xp_scr[SUBLANE:SUBLANE + l, :] = x_ref[0]
    xp_scr[SUBLANE + l:2 * SUBLANE + l, :] = zeros
    w0, w1, w2, bias = w_ref[0:1, :], w_ref[1:2, :], w_ref[2:3, :], b_ref[...]
    for r0 in range(0, l, CONV_ROWS):
        row = r0 + lax.broadcasted_iota(jnp.int32, (CONV_ROWS, 1), 0)
        prev = xp_scr[SUBLANE - 1 + r0:SUBLANE - 1 + r0 + CONV_ROWS, :]
        nxt = xp_scr[SUBLANE + 1 + r0:SUBLANE + 1 + r0 + CONV_ROWS, :]
        if r0 <= CTX_LEN < r0 + CONV_ROWS:
            prev = jnp.where(row == CTX_LEN, 0.0, prev)
        if r0 <= CTX_LEN - 1 < r0 + CONV_ROWS:
            nxt = jnp.where(row == CTX_LEN - 1, 0.0, nxt)
        y = prev * w0 + xp_scr[SUBLANE + r0:SUBLANE + r0 + CONV_ROWS, :] * w1 + nxt * w2 + bias
        o_ref[0, r0:r0 + CONV_ROWS, :] = y * jax.nn.sigmoid(y)


def _segment_conv_silu(x, w, b):
    bsz, l, c = x.shape
    return pl.pallas_call(
        _conv_silu_body,
        grid=(bsz,),
        in_specs=[pl.BlockSpec((1, l, c), lambda i: (i, 0, 0)),
                  pl.BlockSpec(w.shape, lambda i: (0, 0)),
                  pl.BlockSpec((1, c), lambda i: (0, 0))],
        out_specs=pl.BlockSpec((1, l, c), lambda i: (i, 0, 0)),
        out_shape=jax.ShapeDtypeStruct(x.shape, F32),
        scratch_shapes=[pltpu.VMEM((l + 2 * SUBLANE, c), F32)],
        compiler_params=pltpu.CompilerParams(vmem_limit_bytes=VMEM_LIMIT),
        name='conv_silu',
    )(x, w, b.reshape(1, c))


def gla_tokens(p, w_gf, b_gf, w_gb, b_gb):
    q, k, v, r, lr_f, lr_b = p
    no_rows = jnp.zeros((q.shape[0], SUBLANE, q.shape[1]), F32)

    def log_gate(lr, w, b):
        return jax.nn.log_sigmoid(_mm(lr, w) + b) / GLA_GATE_NORM

    return scan_bidir(q * GLA_DK ** -0.5, k, k, v, log_gate(lr_f, w_gf, b_gf), log_gate(lr_b, w_gb, b_gb),
                      no_rows, no_rows, heads=GLA_HEADS, per_dim=True, chunk=CHUNK) + (r,)


def ssd_tokens(p, conv_w, conv_b, dtb_f, dtb_b, alog_f, alog_b):
    z, xbc, dt_f, dt_b = p
    xbc = _segment_conv_silu(xbc, conv_w, conv_b)
    xs, bm, cm = split_cols(xbc, (SSD_HEADS * SSD_HEADDIM, SSD_BC, SSD_BC))

    def direction(dt_raw, dt_bias, a_log):
        dt = jax.nn.softplus(dt_raw + dt_bias)
        return dt, dt * -jnp.exp(a_log)

    dt_f, la_f = direction(dt_f, dtb_f, alog_f)
    dt_b, la_b = direction(dt_b, dtb_b, alog_b)
    return scan_bidir(cm, bm, bm, xs, _pad_lanes(la_f), _pad_lanes(la_b), _head_rows(la_f), _head_rows(la_b),
                      _pad_lanes(dt_f), _pad_lanes(dt_b), heads=SSD_HEADS, groups=SSD_GROUPS, per_dim=False,
                      chunk=SCAN_TL) + (xs, z)


MIX_TM = 512


def _silu(t):
    return t * jax.nn.sigmoid(t)


def _mix_out_body(mla_ref, gf_ref, gb_ref, gr_ref, sf_ref, sb_ref, sx_ref, sz_ref, rf_ref, rb_ref, rg_ref,
                  gla_g_ref, ssd_d_ref, ssd_g_ref, o_ref):
    w = mla_ref.shape[1]
    dv = w // GLA_HEADS
    avg = jnp.where(lax.broadcasted_iota(jnp.int32, (w, w), 0) // dv == lax.broadcasted_iota(jnp.int32, (w, w), 1) // dv,
                    1.0 / dv, 0.0)

    def head_mean(t):
        return jnp.dot(t, avg, precision=_HI, preferred_element_type=F32)

    o_ref[:, 0:w] = mla_ref[...].astype(o_ref.dtype)
    gla = gf_ref[...] + gb_ref[...]
    gla = gla * lax.rsqrt(head_mean(gla * gla) + 1e-6) * gla_g_ref[...] * _silu(gr_ref[...])
    o_ref[:, w:2 * w] = gla.astype(o_ref.dtype)
    ssd = (sf_ref[...] + sb_ref[...] + ssd_d_ref[...] * sx_ref[...]) * _silu(sz_ref[...])
    ssd = ssd * lax.rsqrt(jnp.mean(ssd * ssd, axis=-1, keepdims=True) + 1e-6) * ssd_g_ref[...]
    o_ref[:, 2 * w:3 * w] = ssd.astype(o_ref.dtype)
    ret = rf_ref[...] + rb_ref[...]
    ret = ret - head_mean(ret)
    ret = ret * lax.rsqrt(head_mean(ret * ret) + 1e-6) * _silu(rg_ref[...])
    o_ref[:, 3 * w:4 * w] = ret.astype(o_ref.dtype)


def mix_out(o_mla, gla, ssd, ret, gla_g, ssd_d, ssd_g):
    b, l, w = o_mla.shape
    rows = b * l
    acts = [t.reshape(rows, w) for t in (o_mla,) + tuple(gla) + tuple(ssd) + tuple(ret)]
    params = [gla_g.reshape(1, w), jnp.repeat(ssd_d, SSD_HEADDIM).reshape(1, w), ssd_g.reshape(1, w)]
    tm = math.gcd(MIX_TM, rows)
    out = pl.pallas_call(
        _mix_out_body,
        grid=(rows // tm,),
        in_specs=[pl.BlockSpec((tm, w), lambda i: (i, 0))] * len(acts)
        + [pl.BlockSpec((1, w), lambda i: (0, 0))] * len(params),
        out_specs=pl.BlockSpec((tm, 4 * w), lambda i: (i, 0)),
        out_shape=jax.ShapeDtypeStruct((rows, 4 * w), BF16),
        name='mix_out',
    )(*acts, *params)
    return out.reshape(b, l, 4 * w)


def ret_tokens(p, rope):
    q, k, v, g = p
    q = _rotate_heads(q, RET_HEADS, rope)
    k = _rotate_heads(k * RET_DK ** -0.5, RET_HEADS, rope)
    la = jnp.broadcast_to(ret_log_decay(), q.shape[:2] + (RET_HEADS,))
    return scan_bidir(q, k, k, v, _pad_lanes(la), _pad_lanes(la), _head_rows(la), _head_rows(la),
                      heads=RET_HEADS, per_dim=False, chunk=SCAN_TL) + (g,)


def mla_tokens(p, g_q, w_uq, g_kv, w_uk, w_uv, axial):
    c_q, c_kv, k_r = p
    kv = rms_norm(c_kv, g_kv)
    k_nope = to_heads(_mm(kv, w_uk), MLA_HEADS)
    v = to_heads(_mm(kv, w_uv), MLA_HEADS)
    k_rope = axial_rope(k_r, axial)
    k = jnp.concatenate([k_nope, jnp.broadcast_to(k_rope[:, None], k_nope.shape[:3] + (MLA_ROPE,))], -1)
    q = to_heads(_mm(rms_norm(c_q, g_q), w_uq * (MLA_NOPE + MLA_ROPE) ** -0.5), MLA_HEADS)
    q = jnp.concatenate([q[..., :MLA_NOPE], axial_rope(q[..., MLA_NOPE:], axial)], -1)
    return attention_tokens(q.astype(BF16), k.astype(BF16), v.astype(BF16))


ATTN_TQ = 256


def _attn_body(q_ref, k_ref, v_ref, o_ref):
    heads, l, dv = v_ref.shape[1], v_ref.shape[2], v_ref.shape[3]

    def attend(n_keys):
        for h in range(heads):
            s = lax.dot_general(q_ref[0, h], k_ref[0, h, :n_keys, :], _NT, preferred_element_type=F32)
            p = jnp.exp(s - jnp.max(s, axis=1, keepdims=True))
            o = jnp.dot(p.astype(BF16), v_ref[0, h, :n_keys, :], preferred_element_type=F32)
            o_ref[0, :, h * dv:(h + 1) * dv] = o / jnp.sum(p, axis=1, keepdims=True)

    is_ctx = pl.program_id(1) < CTX_LEN // ATTN_TQ
    pl.when(is_ctx)(lambda: attend(CTX_LEN))
    pl.when(jnp.logical_not(is_ctx))(lambda: attend(l))


def attention_tokens(q, k, v):
    b, h, l, d = q.shape
    dv = v.shape[-1]
    return pl.pallas_call(
        _attn_body,
        grid=(b, l // ATTN_TQ),
        in_specs=[pl.BlockSpec((1, h, ATTN_TQ, d), lambda bi, qi: (bi, 0, qi, 0)),
                  pl.BlockSpec((1, h, l, d), lambda bi, qi: (bi, 0, 0, 0)),
                  pl.BlockSpec((1, h, l, dv), lambda bi, qi: (bi, 0, 0, 0))],
        out_specs=pl.BlockSpec((1, ATTN_TQ, h * dv), lambda bi, qi: (bi, qi, 0)),
        out_shape=jax.ShapeDtypeStruct((b, l, h * dv), F32),
        compiler_params=pltpu.CompilerParams(vmem_limit_bytes=VMEM_LIMIT),
        name='mla_attention',
    )(q, k, v)


RES_TM = CTX_LEN


def _residual_norm_body(x_ref, y_ref, gate_ref, g_ref, b_ref, *rest):
    v = DEEPNORM_ALPHA * x_ref[0] + gate_ref[0, 0] * y_ref[0]
    cen = v - jnp.mean(v, axis=-1, keepdims=True)
    r = cen * lax.rsqrt(jnp.mean(cen * cen, axis=-1, keepdims=True) + 1e-5) * g_ref[...] + b_ref[...]
    if len(rest) == 1:
        rest[0][0] = r
    else:
        shift_ref, scale_ref, r_ref, m_ref = rest
        r_ref[0] = r
        m_ref[0] = (r * (1 + scale_ref[0, 0]) + shift_ref[0, 0]).astype(BF16)


def residual_norm(x, y, gate, g, b, shift=None, scale=None):
    bsz, lo, d = y.shape
    off = (x.shape[1] - lo) // RES_TM
    modulated = shift is not None
    tok = lambda o: pl.BlockSpec((1, RES_TM, d), lambda bi, t: (bi, t + o, 0))
    vspec = pl.BlockSpec((1, 1, 1, d), lambda bi, t: (bi, jnp.minimum(t + off, 1), 0, 0))
    pspec = pl.BlockSpec((1, d), lambda bi, t: (0, 0))
    out_spec = pl.BlockSpec((1, RES_TM, d), lambda bi, t: (bi, t, 0))
    r_shape = jax.ShapeDtypeStruct((bsz, lo, d), F32)
    return pl.pallas_call(
        _residual_norm_body,
        grid=(bsz, lo // RES_TM),
        in_specs=[tok(off), tok(0), vspec, pspec, pspec] + ([vspec, vspec] if modulated else []),
        out_specs=[out_spec, out_spec] if modulated else out_spec,
        out_shape=[r_shape, jax.ShapeDtypeStruct((bsz, lo, d), BF16)] if modulated else r_shape,
        name='residual_norm',
    )(x, y, gate, g.reshape(1, d), b.reshape(1, d), *((shift, scale) if modulated else ()))


def kernel(x, c, ctx, c_ctx, w_ada, b_ada, w_in, mla_g_q, mla_w_uq, mla_g_kv, mla_w_uk, mla_w_uv, gla_w_gf, gla_b_gf, gla_w_gb, gla_b_gb, gla_g, ssd_conv_w, ssd_conv_b, ssd_dt_bias_f, ssd_dt_bias_b, ssd_a_log_f, ssd_a_log_b, ssd_d, ssd_g, w_out, ln1_g, ln1_b, peer_w_q, peer_keys, peer_u, peer_v, ln2_g, ln2_b):
    seq = x.shape[1]
    rows = seq // GRID_W
    row = jnp.repeat(jnp.arange(rows), GRID_W)
    col = jnp.tile(jnp.arange(GRID_W), rows)
    axial = (_ctx_identity(rope_tables(row, MLA_ROPE // 2)), _ctx_identity(rope_tables(col, MLA_ROPE // 2)))
    ret_rope = _ctx_identity(rope_tables(jnp.arange(seq), RET_DK))
    s_lat = jax.nn.silu(c)
    s_ctx = jax.nn.silu(c_ctx)
    bsz = x.shape[0]

    def ada(i):
        lat = s_lat @ w_ada[i] + b_ada[i]
        ctx_v = jnp.broadcast_to(s_ctx @ w_ada[i] + b_ada[i], lat.shape)
        return jnp.stack([ctx_v, lat], 1).reshape(bsz, 2, 6, 1, D_MODEL)

    vecs = [ada(i) for i in range(DEPTH)]
    xa = jnp.concatenate([ctx, x], 1)
    sel = jnp.concatenate([jnp.zeros((CTX_LEN,), jnp.int32), jnp.ones((seq,), jnp.int32)])[None, :, None]
    v0 = vecs[0]
    xin = modulate(xa, jnp.where(sel == 0, v0[:, 0, 0], v0[:, 1, 0]), jnp.where(sel == 0, v0[:, 0, 1], v0[:, 1, 1]))
    for i in range(DEPTH):
        last = i == DEPTH - 1
        vec = vecs[i]
        p_mla, p_gla, p_ssd, p_ret = partition(split_cols(_mm(xin, w_in[i]), IN_SIZES))
        mixed = mix_out(
            mla_tokens(p_mla, mla_g_q[i], mla_w_uq[i], mla_g_kv[i], mla_w_uk[i], mla_w_uv[i], axial),
            gla_tokens(p_gla, gla_w_gf[i], gla_b_gf[i], gla_w_gb[i], gla_b_gb[i]),
            ssd_tokens(p_ssd, ssd_conv_w[i], ssd_conv_b[i], ssd_dt_bias_f[i], ssd_dt_bias_b[i],
                       ssd_a_log_f[i], ssd_a_log_b[i]),
            ret_tokens(p_ret, ret_rope), gla_g[i], ssd_d[i], ssd_g[i])
        y_all = _mm(mixed, w_out[i])
        x1, xm = residual_norm(xa, y_all, vec[:, :, 2], ln1_g[i], ln1_b[i], vec[:, :, 3], vec[:, :, 4])
        toks = xm[:, CTX_LEN:] if last else xm
        f = peer_dense(toks.reshape(-1, D_MODEL), peer_w_q[i].T.astype(BF16), peer_keys[i, 0].astype(BF16),
                       peer_keys[i, 1].astype(BF16), peer_u[i].astype(BF16), peer_v[i].T.astype(BF16))
        f = f.reshape(bsz, -1, D_MODEL)
        if last:
            return residual_norm(x1, f, vec[:, :, 5], ln2_g[i], ln2_b[i])
        nxt = vecs[i + 1]
        xa, xin = residual_norm(x1, f, vec[:, :, 5], ln2_g[i], ln2_b[i], nxt[:, :, 0], nxt[:, :, 1])
```

```python
import functools
import math

import jax
import jax.numpy as jnp
import numpy as np
from jax import lax
from jax.experimental import pallas as pl
from jax.experimental.pallas import tpu as pltpu

D_MODEL = 1024
DEPTH = 4
GRID_W = 64
CHUNK = 64
ROPE_BASE = 10000.0
MLA_HEADS, MLA_NOPE, MLA_ROPE, MLA_V, MLA_Q_LORA, MLA_KV_LORA = 4, 64, 32, 64, 256, 128
GLA_HEADS, GLA_DK, GLA_DV, GLA_GATE_RANK, GLA_GATE_NORM = 4, 32, 64, 16, 16.0
SSD_HEADS, SSD_HEADDIM, SSD_GROUPS, SSD_STATE = 4, 64, 2, 128
RET_HEADS, RET_DK, RET_DV = 4, 32, 64
PEER_HEADS, PEER_NKEYS, PEER_DKEY, PEER_TOPK = 8, 128, 256, 16
DEEPNORM_ALPHA = (2 * DEPTH) ** 0.25

SSD_BC = SSD_GROUPS * SSD_STATE
SSD_XBC = SSD_HEADS * SSD_HEADDIM + 2 * SSD_BC
MLA_SIZES = (MLA_Q_LORA, MLA_KV_LORA, MLA_ROPE)
GLA_SIZES = (GLA_HEADS * GLA_DK, GLA_HEADS * GLA_DK, GLA_HEADS * GLA_DV, GLA_HEADS * GLA_DV, GLA_GATE_RANK, GLA_GATE_RANK)
SSD_SIZES = (SSD_HEADS * SSD_HEADDIM, SSD_XBC, SSD_HEADS, SSD_HEADS)
RET_SIZES = (RET_HEADS * RET_DK, RET_HEADS * RET_DK, RET_HEADS * RET_DV, RET_HEADS * RET_DV)
IN_SIZES = MLA_SIZES + GLA_SIZES + SSD_SIZES + RET_SIZES

F32 = jnp.float32
BF16 = jnp.bfloat16
LANE = 128
VMEM_LIMIT = 48 * 2 ** 20


def _mm_body(a_ref, b_ref, o_ref):
    o_ref[...] = jnp.dot(a_ref[...].astype(BF16), b_ref[...].astype(BF16), preferred_element_type=F32)


def _mm(a, b, tm=1024, tn=1024):
    lead = a.shape[:-1]
    k = a.shape[-1]
    n = b.shape[-1]
    a2 = a.reshape(-1, k)
    m = a2.shape[0]
    tm = math.gcd(tm, m)
    n_pad = -(-n // LANE) * LANE
    tn = max(t for t in range(LANE, tn + 1, LANE) if n_pad % t == 0)
    b = b.astype(BF16)
    if n_pad != n:
        b = jnp.pad(b, ((0, 0), (0, n_pad - n)))
    out = pl.pallas_call(
        _mm_body,
        grid=(m // tm, n_pad // tn),
        in_specs=[pl.BlockSpec((tm, k), lambda i, j: (i, 0)), pl.BlockSpec((k, tn), lambda i, j: (0, j))],
        out_specs=pl.BlockSpec((tm, tn), lambda i, j: (i, j)),
        out_shape=jax.ShapeDtypeStruct((m, n_pad), F32),
        compiler_params=pltpu.CompilerParams(vmem_limit_bytes=VMEM_LIMIT),
        name='matmul',
    )(a2, b)
    return out[:, :n].reshape(lead + (n,))


def rms_norm(x, g, eps=1e-6):
    return x * lax.rsqrt(jnp.mean(x * x, -1, keepdims=True) + eps) * g


def modulate(x, shift, scale):
    return x * (1 + scale) + shift


def split_cols(p, sizes):
    return jnp.split(p, np.cumsum(sizes)[:-1].tolist(), axis=-1)


def partition(pieces):
    a = len(MLA_SIZES)
    b = a + len(GLA_SIZES)
    s = b + len(SSD_SIZES)
    return pieces[:a], pieces[a:b], pieces[b:s], pieces[s:]


def to_heads(t, h):
    b, l, _ = t.shape
    return t.reshape(b, l, h, -1).transpose(0, 2, 1, 3)


def rope_tables(pos, dim):
    half = dim // 2
    freqs = ROPE_BASE ** (-jnp.arange(half, dtype=F32) / half)
    ang = pos.astype(F32)[:, None] * freqs
    return jnp.cos(ang), jnp.sin(ang)


def rotate(x, tab):
    cos, sin = tab
    h = x.shape[-1] // 2
    x1, x2 = x[..., :h], x[..., h:]
    return jnp.concatenate([x1 * cos - x2 * sin, x1 * sin + x2 * cos], -1)


def axial_rope(x, axial):
    h = x.shape[-1] // 2
    return jnp.concatenate([rotate(x[..., :h], axial[0]), rotate(x[..., h:], axial[1])], -1)


def ret_log_decay():
    return jnp.log1p(-jnp.exp2(-5.0 - jnp.arange(RET_HEADS, dtype=F32)))


PEER_TT = 512
PEER_EB = 512
SUBLANE = 8
_NT = (((1,), (1,)), ((), ()))
NEG_INF = float('-inf')
PEER_NCAND = sum(PEER_TOPK // (b + 1) for b in range(PEER_TOPK))


def _merge_exchanges(lo, hi, r):
    step = r * 2
    if step < hi - lo:
        yield from _merge_exchanges(lo, hi, step)
        yield from _merge_exchanges(lo + r, hi, step)
        yield from ((i, i + r) for i in range(lo + r, hi - r, step))
    else:
        yield (lo, lo + r)


def _sort_exchanges(lo, hi):
    if hi > lo:
        mid = lo + (hi - lo) // 2
        yield from _sort_exchanges(lo, mid)
        yield from _sort_exchanges(mid + 1, hi)
        yield from _merge_exchanges(lo, hi, 1)


def _top_values(s, rows_ref):
    n = s.shape[0] // SUBLANE
    v = [s[t * SUBLANE:(t + 1) * SUBLANE, :] for t in range(n)]
    for i, j in _sort_exchanges(0, n - 1):
        v[i], v[j] = jnp.maximum(v[i], v[j]), jnp.minimum(v[i], v[j])
    for k in range(PEER_TOPK):
        m = jnp.max(v[0], axis=0, keepdims=True)
        rows_ref[k:k + 1, :] = m
        pop = v[0] == m
        live = min(n, PEER_TOPK - k) - 1
        for r in range(live):
            v[r] = jnp.where(pop, v[r + 1] if r + 1 < n else NEG_INF, v[r])


def _peer_score_body(x_ref, wqt_ref, k0_ref, k1_ref, s1_ref, e1_ref, s2_ref, e2_ref,
                     qt_scr, v1_scr, v2_scr, cand_scr):
    half = PEER_DKEY // 2
    qt_scr[...] = lax.dot_general(wqt_ref[...], x_ref[...], _NT, preferred_element_type=F32)

    def strip(h, tg):
        s1 = s1_ref[h, tg]
        s2 = s2_ref[h, tg]
        _top_values(s1, v1_scr)
        _top_values(s2, v2_scr)
        r = 0
        for b in range(PEER_TOPK):
            for a in range(PEER_TOPK // (b + 1)):
                cand_scr[r:r + 1, :] = v1_scr[a:a + 1, :] + v2_scr[b:b + 1, :]
                r += 1
        cand_scr[r:, :] = jnp.full((cand_scr.shape[0] - r, LANE), NEG_INF, F32)
        cand = [cand_scr[t0:t0 + SUBLANE, :] for t0 in range(0, cand_scr.shape[0], SUBLANE)]
        prev = None
        cnt = jnp.zeros((1, LANE), F32)
        tau = jnp.full((1, LANE), NEG_INF, F32)
        for _ in range(PEER_TOPK):
            best = None
            for c in cand:
                cur = c if prev is None else jnp.where(c < prev, c, NEG_INF)
                best = cur if best is None else jnp.maximum(best, cur)
            m = jnp.max(best, axis=0, keepdims=True)
            hits = None
            for c in cand:
                eq = jnp.where(c == m, 1.0, 0.0)
                hits = eq if hits is None else hits + eq
            tau = jnp.where(cnt < PEER_TOPK, m, tau)
            cnt = cnt + jnp.sum(hits, axis=0, keepdims=True)
            prev = m
        m1 = v1_scr[0:1, :]
        m2 = v2_scr[0:1, :]
        zs = None
        for c in cand:
            z = jnp.where(c >= tau, jnp.exp(c - (m1 + m2)), 0.0)
            zs = z if zs is None else zs + z
        z_tot = jnp.sum(zs, axis=0, keepdims=True)
        e1_ref[h, tg] = jnp.exp(s1 - m1) / (2.0 * z_tot)
        e2_ref[h, tg] = jnp.exp(s2 - m2)
        c1 = jnp.full(s1.shape, jnp.inf, F32)
        for b in range(PEER_TOPK):
            v2b = v2_scr[b:b + 1, :]
            c1 = jnp.where(s1 + v2b >= tau, v2b, c1)
        s1_ref[h, tg] = c1

    def head(h, carry):
        base = pl.multiple_of(h * PEER_DKEY, PEER_DKEY)
        s1 = jnp.dot(k0_ref[...], qt_scr[pl.ds(base, half), :].astype(BF16), preferred_element_type=F32)
        s2 = jnp.dot(k1_ref[...], qt_scr[pl.ds(base + half, half), :].astype(BF16), preferred_element_type=F32)
        for tg in range(x_ref.shape[0] // LANE):
            s1_ref[h, tg] = s1[:, tg * LANE:(tg + 1) * LANE]
            s2_ref[h, tg] = s2[:, tg * LANE:(tg + 1) * LANE]
        for tg in range(x_ref.shape[0] // LANE):
            strip(h, tg)
        return carry

    lax.fori_loop(0, PEER_HEADS, head, 0)


PEER_JQ = 4
PEER_IQ = 4


def _peer_expert_body(x_ref, u_ref, vt_ref, c1_ref, e1_ref, s2_ref, e2_ref, o_ref,
                      h_scr, g_scr, act_scr, acc_scr, bc1_scr, be1_scr):
    e = pl.program_id(1)
    n_blocks = pl.num_programs(1) - 1
    tt = x_ref.shape[0]
    n_i = u_ref.shape[0] // PEER_NKEYS
    span = PEER_JQ * SUBLANE

    @pl.when(e == 0)
    def _():
        acc_scr[...] = jnp.zeros_like(acc_scr)
        act_scr[...] = jnp.zeros_like(act_scr)

    n_tg = tt // LANE
    m_rows = acc_scr.shape[0] // n_tg
    i_base = jnp.minimum(e, n_blocks - 1) * n_i
    for i_loc in range(n_i):
        for tg in range(n_tg):
            for h in range(PEER_HEADS):
                bc1_scr[i_loc, tg, h] = jnp.broadcast_to(c1_ref[h, tg, pl.ds(i_base + i_loc, 1), :], (SUBLANE, LANE))
                be1_scr[i_loc, tg, h] = jnp.broadcast_to(e1_ref[h, tg, pl.ds(i_base + i_loc, 1), :], (SUBLANE, LANE))
    h_scr[...] = lax.dot_general(u_ref[...], x_ref[...], _NT, preferred_element_type=F32)

    def token_strip(tg, carry):
        m0 = pl.multiple_of(tg * m_rows, m_rows)
        acc_scr[pl.ds(m0, m_rows), :] += jnp.dot(vt_ref[pl.ds(m0, m_rows), :], act_scr[...],
                                                  preferred_element_type=F32)
        for i0 in range(0, n_i, PEER_IQ):
            for j0 in range(0, PEER_NKEYS, span):
                g = [[None] * PEER_JQ for _ in range(PEER_IQ)]
                for h in range(PEER_HEADS):
                    cs = [bc1_scr[i0 + ii, tg, h] for ii in range(PEER_IQ)]
                    bs = [be1_scr[i0 + ii, tg, h] for ii in range(PEER_IQ)]
                    for k in range(PEER_JQ):
                        rows = slice(j0 + k * SUBLANE, j0 + (k + 1) * SUBLANE)
                        s2t = s2_ref[h, tg, rows, :]
                        e2t = e2_ref[h, tg, rows, :]
                        for ii in range(PEER_IQ):
                            gh = jnp.where(s2t >= cs[ii], bs[ii] * e2t, 0.0)
                            g[ii][k] = gh if g[ii][k] is None else g[ii][k] + gh
                for ii in range(PEER_IQ):
                    for k in range(PEER_JQ):
                        r0 = (i0 + ii) * PEER_NKEYS + j0 + k * SUBLANE
                        g_scr[tg, r0:r0 + SUBLANE, :] = g[ii][k]
        return carry

    lax.fori_loop(0, n_tg, token_strip, 0)
    for tg in range(n_tg):
        hv = h_scr[:, tg * LANE:(tg + 1) * LANE]
        act_scr[:, tg * LANE:(tg + 1) * LANE] = (
            hv * (1.0 + lax.erf(hv * (2.0 ** -0.5))) * g_scr[tg]).astype(BF16)

    @pl.when(e == n_blocks)
    def _():
        o_ref[...] = acc_scr[...].T


def peer_dense(xm, wqt, k0, k1, u_bf, vt_bf, tt=PEER_TT, eb=PEER_EB):
    t, d = xm.shape
    n_blocks = u_bf.shape[0] // eb
    xb = xm.astype(BF16)
    hk = (PEER_HEADS, t // LANE, PEER_NKEYS, LANE)
    sblk = pl.BlockSpec((PEER_HEADS, tt // LANE, PEER_NKEYS, LANE), lambda i: (0, i, 0, 0))
    c1, e1, s2, e2 = pl.pallas_call(
        _peer_score_body,
        grid=(t // tt,),
        in_specs=[pl.BlockSpec((tt, d), lambda i: (i, 0)),
                  pl.BlockSpec(wqt.shape, lambda i: (0, 0)),
                  pl.BlockSpec(k0.shape, lambda i: (0, 0)),
                  pl.BlockSpec(k1.shape, lambda i: (0, 0))],
        out_specs=[sblk, sblk, sblk, sblk],
        out_shape=[jax.ShapeDtypeStruct(hk, F32)] * 4,
        scratch_shapes=[pltpu.VMEM((PEER_HEADS * PEER_DKEY, tt), F32),
                        pltpu.VMEM((PEER_TOPK, LANE), F32), pltpu.VMEM((PEER_TOPK, LANE), F32),
                        pltpu.VMEM((-(-PEER_NCAND // SUBLANE) * SUBLANE, LANE), F32)],
        compiler_params=pltpu.CompilerParams(vmem_limit_bytes=VMEM_LIMIT),
        name='peer_score',
    )(xb, wqt, k0, k1)
    sblk2 = pl.BlockSpec((PEER_HEADS, tt // LANE, PEER_NKEYS, LANE), lambda i, j: (0, i, 0, 0))
    return pl.pallas_call(
        _peer_expert_body,
        grid=(t // tt, n_blocks + 1),
        in_specs=[pl.BlockSpec((tt, d), lambda i, j: (i, 0)),
                  pl.BlockSpec((eb, d), lambda i, j: (jnp.minimum(j, n_blocks - 1), 0)),
                  pl.BlockSpec((d, eb), lambda i, j: (0, jnp.maximum(j - 1, 0))),
                  sblk2, sblk2, sblk2, sblk2],
        out_specs=pl.BlockSpec((tt, d), lambda i, j: (i, 0)),
        out_shape=jax.ShapeDtypeStruct((t, d), F32),
        scratch_shapes=[pltpu.VMEM((eb, tt), F32), pltpu.VMEM((tt // LANE, eb, LANE), F32), pltpu.VMEM((eb, tt), BF16),
                        pltpu.VMEM((d, tt), F32)]
        + [pltpu.VMEM((eb // PEER_NKEYS, tt // LANE, PEER_HEADS, SUBLANE, LANE), F32)] * 2,
        compiler_params=pltpu.CompilerParams(vmem_limit_bytes=VMEM_LIMIT,
                                             dimension_semantics=("arbitrary", "arbitrary")),
        name='peer_expert',
    )(xb, u_bf, vt_bf, c1, e1, s2, e2)


SCAN_TL = 256
CTX_LEN = 256
_HI = lax.Precision.HIGHEST
_TN = (((0,), (0,)), ((), ()))


def _scan_body(*refs, chunk, heads, groups, per_dim, scaled):
    n_in = 6 if scaled else 5
    of_ref, ob_ref, sf_scr, sb_scr = refs[2 * n_in:]
    tl, dk = refs[0].shape[1], refs[0].shape[2] // groups
    vd = refs[2].shape[2]
    kd, dv = heads * dk, vd // heads

    @pl.when(pl.program_id(1) == 0)
    def _():
        sf_scr[...] = jnp.zeros_like(sf_scr)
        sb_scr[...] = jnp.zeros_like(sb_scr)

    ri = lax.broadcasted_iota(jnp.int32, (chunk, chunk), 0)
    ci = lax.broadcasted_iota(jnp.int32, (chunk, chunk), 1)
    khead = lax.broadcasted_iota(jnp.int32, (1, kd), 1) // dk
    vhead = lax.broadcasted_iota(jnp.int32, (1, vd), 1) // dv
    same_head = (lax.broadcasted_iota(jnp.int32, (vd, kd), 0) // dv
                 == lax.broadcasted_iota(jnp.int32, (vd, kd), 1) // dk)

    def per_key(x8):
        out = jnp.zeros((x8.shape[0], kd), F32)
        for h in range(heads):
            out = jnp.where(khead == h, x8[:, h:h + 1], out)
        return out

    def per_head(t):
        if groups == heads:
            return t
        rep = heads // groups
        return jnp.concatenate([t[:, (h // rep) * dk:(h // rep + 1) * dk] for h in range(heads)], axis=1)

    def sweep(in_refs, o_ref, s_scr, fwd):
        q_ref, k_ref, v_ref, lg_ref, lt_ref = in_refs[:5]
        keep = (ri >= ci) if fwd else (ri <= ci)
        tri = keep.astype(F32)
        keep_all = jnp.concatenate([keep] * heads, axis=0)
        order = range(tl // chunk) if fwd else reversed(range(tl // chunk))
        s = s_scr[...]
        for c in order:
            rows = slice(c * chunk, (c + 1) * chunk)
            q, k, v = per_head(q_ref[0, rows, :]), per_head(k_ref[0, rows, :]), v_ref[0, rows, :]
            if scaled:
                k = k * per_key(in_refs[5][0, rows, :])
            cum = jnp.dot(tri, lg_ref[0, rows, :], precision=_HI, preferred_element_type=F32)
            tot = cum[chunk - 1:chunk, :] if fwd else cum[0:1, :]
            if per_dim:
                e_in, e_out, e_tot = jnp.exp(cum), jnp.exp(tot - cum), jnp.exp(tot)
                mid = cum[chunk // 2:chunk // 2 + 1, :]
                q_att = (q * jnp.exp(cum - mid))
                k_att = (k * jnp.exp(mid - cum)).astype(BF16)
            else:
                e_in, e_out, e_tot = per_key(jnp.exp(cum)), per_key(jnp.exp(tot - cum)), per_key(jnp.exp(tot))
                cum_t = lax.dot_general(lt_ref[0, :, rows], tri, _NT, precision=_HI, preferred_element_type=F32)
                q_att, k_att = q, k.astype(BF16)
            o = lax.dot_general((q * e_in).astype(BF16), s.astype(BF16), _NT, preferred_element_type=F32)
            q_stack = jnp.concatenate([jnp.where(khead == h, q_att, 0.0) for h in range(heads)], axis=0)
            sc = lax.dot_general(q_stack.astype(BF16), k_att, _NT, preferred_element_type=F32)
            if per_dim:
                att = jnp.where(keep_all, sc, 0.0)
            else:
                att = jnp.concatenate(
                    [sc[h * chunk:(h + 1) * chunk]
                     * jnp.exp(jnp.where(keep, cum[:, h:h + 1] - cum_t[h:h + 1, :], NEG_INF)) for h in range(heads)],
                    axis=0)
            pv = jnp.dot(att.astype(BF16), v.astype(BF16), preferred_element_type=F32)
            for h in range(heads):
                o = o + jnp.where(vhead == h, pv[h * chunk:(h + 1) * chunk], 0.0)
            o_ref[0, rows, :] = o
            kv = lax.dot_general(v.astype(BF16), (k * e_out).astype(BF16), _TN, preferred_element_type=F32)
            s = s * e_tot + jnp.where(same_head, kv, 0.0)
        s_scr[...] = s

    def sweep_blocked(in_refs, o_ref, s_scr, fwd):
        q_ref, k_ref, v_ref, lg_ref = in_refs[:4]
        n = tl // chunk
        rr = lax.broadcasted_iota(jnp.int32, (tl, tl), 0)
        cc = lax.broadcasted_iota(jnp.int32, (tl, tl), 1)
        keep = (rr // chunk == cc // chunk) & ((rr >= cc) if fwd else (rr <= cc))
        q, k, v = q_ref[0], k_ref[0], v_ref[0]
        cum = jnp.dot(keep.astype(F32), lg_ref[0], precision=_HI, preferred_element_type=F32)
        end = chunk - 1 if fwd else 0

        def chunk_rows(idx):
            return jnp.concatenate([jnp.broadcast_to(cum[c * chunk + idx:c * chunk + idx + 1, :], (chunk, kd))
                                    for c in range(n)], axis=0)

        tot, mid = chunk_rows(end), chunk_rows(chunk // 2)
        q_in = (q * jnp.exp(cum)).astype(BF16)
        k_out = (k * jnp.exp(tot - cum)).astype(BF16)
        q_att = q * jnp.exp(cum - mid)
        k_att = (k * jnp.exp(mid - cum)).astype(BF16)
        q_stack = jnp.concatenate([jnp.where(khead == h, q_att, 0.0) for h in range(heads)], axis=0)
        sc = lax.dot_general(q_stack.astype(BF16), k_att, _NT, preferred_element_type=F32)
        att = jnp.where(jnp.concatenate([keep] * heads, axis=0), sc, 0.0)
        pv = jnp.dot(att.astype(BF16), v.astype(BF16), preferred_element_type=F32)
        o_intra = jnp.where(vhead == 0, pv[0:tl], 0.0)
        for h in range(1, heads):
            o_intra = o_intra + jnp.where(vhead == h, pv[h * tl:(h + 1) * tl], 0.0)
        s = s_scr[...]
        for c in (range(n) if fwd else reversed(range(n))):
            rows = slice(c * chunk, (c + 1) * chunk)
            o_ref[0, rows, :] = o_intra[rows] + lax.dot_general(q_in[rows], s.astype(BF16), _NT,
                                                                preferred_element_type=F32)
            kv = lax.dot_general(v[rows].astype(BF16), k_out[rows], _TN, preferred_element_type=F32)
            s = s * jnp.exp(cum[c * chunk + end:c * chunk + end + 1, :]) + jnp.where(same_head, kv, 0.0)
        s_scr[...] = s

    run = sweep_blocked if per_dim and groups == heads and not scaled else sweep
    run(refs[:n_in], of_ref, sf_scr, True)
    run(refs[n_in:2 * n_in], ob_ref, sb_scr, False)


def scan_bidir(q, k_f, k_b, v, lg_f, lg_b, lt_f, lt_b, ks_f=None, ks_b=None, *, heads, per_dim, chunk, groups=None):
    b, l, qw = q.shape
    groups = groups or heads
    kd = qw // groups * heads
    vd = v.shape[-1]
    scaled = ks_f is not None
    n, n_ctx = l // SCAN_TL, CTX_LEN // SCAN_TL

    def bwd_group(s):
        return jnp.where(s < n_ctx, n_ctx - 1 - s, n - 1 - (s - n_ctx))

    def specs(group):
        tok = lambda w: pl.BlockSpec((1, SCAN_TL, w), lambda bi, s: (bi, group(s), 0))
        return [tok(qw), tok(qw), tok(vd), tok(LANE),
                pl.BlockSpec((1, SUBLANE, SCAN_TL), lambda bi, s: (bi, 0, group(s)))] + ([tok(LANE)] if scaled else [])

    fwd_group = lambda s: s
    o_f, o_b = pl.pallas_call(
        functools.partial(_scan_body, chunk=chunk, heads=heads, groups=groups, per_dim=per_dim, scaled=scaled),
        grid=(b, n),
        in_specs=specs(fwd_group) + specs(bwd_group),
        out_specs=[pl.BlockSpec((1, SCAN_TL, vd), lambda bi, s: (bi, s, 0)),
                   pl.BlockSpec((1, SCAN_TL, vd), lambda bi, s: (bi, bwd_group(s), 0))],
        out_shape=[jax.ShapeDtypeStruct((b, l, vd), F32)] * 2,
        scratch_shapes=[pltpu.VMEM((vd, kd), F32)] * 2,
        compiler_params=pltpu.CompilerParams(dimension_semantics=("arbitrary", "arbitrary")),
        name='scan_bidir',
    )(*((q, k_f, v, lg_f, lt_f) + ((ks_f,) if scaled else ()) + (q, k_b, v, lg_b, lt_b) + ((ks_b,) if scaled else ())))
    return o_f, o_b


def _ctx_identity(tab):
    cos, sin = tab
    pad = (CTX_LEN, cos.shape[1])
    return jnp.concatenate([jnp.ones(pad, F32), cos], 0), jnp.concatenate([jnp.zeros(pad, F32), sin], 0)


def _pad_lanes(t):
    return jnp.pad(t, ((0, 0), (0, 0), (0, LANE - t.shape[-1])))


def _head_rows(t):
    return jnp.pad(t.transpose(0, 2, 1), ((0, 0), (0, SUBLANE - t.shape[-1]), (0, 0)))


def _split_heads(t, h):
    return t.reshape(t.shape[:-1] + (h, t.shape[-1] // h))


def _rotate_heads(t, heads, tab):
    cos, sin = tab
    return rotate(_split_heads(t, heads), (cos[:, None, :], sin[:, None, :])).reshape(t.shape)


CONV_ROWS = 64


def _conv_silu_body(x_ref, w_ref, b_ref, o_ref, xp_scr):
    l, c = x_ref.shape[1], x_ref.shape[2]
    zeros = jnp.zeros((SUBLANE, c), F32)
    xp_scr[0:SUBLANE, :] = zeros
    xp_scr[SUBLANE:SUBLANE + l, :] = x_ref[0]
    xp_scr[SUBLANE + l:2 * SUBLANE + l, :] = zeros
    w0, w1, w2, bias = w_ref[0:1, :], w_ref[1:2, :], w_ref[2:3, :], b_ref[...]
    for r0 in range(0, l, CONV_ROWS):
        row = r0 + lax.broadcasted_iota(jnp.int32, (CONV_ROWS, 1), 0)
        prev = xp_scr[SUBLANE - 1 + r0:SUBLANE - 1 + r0 + CONV_ROWS, :]
        nxt = xp_scr[SUBLANE + 1 + r0:SUBLANE + 1 + r0 + CONV_ROWS, :]
        if r0 <= CTX_LEN < r0 + CONV_ROWS:
            prev = jnp.where(row == CTX_LEN, 0.0, prev)
        if r0 <= CTX_LEN - 1 < r0 + CONV_ROWS:
            nxt = jnp.where(row == CTX_LEN - 1, 0.0, nxt)
        y = prev * w0 + xp_scr[SUBLANE + r0:SUBLANE + r0 + CONV_ROWS, :] * w1 + nxt * w2 + bias
        o_ref[0, r0:r0 + CONV_ROWS, :] = y * jax.nn.sigmoid(y)


def _segment_conv_silu(x, w, b):
    bsz, l, c = x.shape
    return pl.pallas_call(
        _conv_silu_body,
        grid=(bsz,),
        in_specs=[pl.BlockSpec((1, l, c), lambda i: (i, 0, 0)),
                  pl.BlockSpec(w.shape, lambda i: (0, 0)),
                  pl.BlockSpec((1, c), lambda i: (0, 0))],
        out_specs=pl.BlockSpec((1, l, c), lambda i: (i, 0, 0)),
        out_shape=jax.ShapeDtypeStruct(x.shape, F32),
        scratch_shapes=[pltpu.VMEM((l + 2 * SUBLANE, c), F32)],
        compiler_params=pltpu.CompilerParams(vmem_limit_bytes=VMEM_LIMIT),
        name='conv_silu',
    )(x, w, b.reshape(1, c))


def gla_tokens(p, w_gf, b_gf, w_gb, b_gb):
    q, k, v, r, lr_f, lr_b = p
    no_rows = jnp.zeros((q.shape[0], SUBLANE, q.shape[1]), F32)

    def log_gate(lr, w, b):
        return jax.nn.log_sigmoid(_mm(lr, w) + b) / GLA_GATE_NORM

    return scan_bidir(q * GLA_DK ** -0.5, k, k, v, log_gate(lr_f, w_gf, b_gf), log_gate(lr_b, w_gb, b_gb),
                      no_rows, no_rows, heads=GLA_HEADS, per_dim=True, chunk=CHUNK) + (r,)


def ssd_tokens(p, conv_w, conv_b, dtb_f, dtb_b, alog_f, alog_b):
    z, xbc, dt_f, dt_b = p
    xbc = _segment_conv_silu(xbc, conv_w, conv_b)
    xs, bm, cm = split_cols(xbc, (SSD_HEADS * SSD_HEADDIM, SSD_BC, SSD_BC))

    def direction(dt_raw, dt_bias, a_log):
        dt = jax.nn.softplus(dt_raw + dt_bias)
        return dt, dt * -jnp.exp(a_log)

    dt_f, la_f = direction(dt_f, dtb_f, alog_f)
    dt_b, la_b = direction(dt_b, dtb_b, alog_b)
    return scan_bidir(cm, bm, bm, xs, _pad_lanes(la_f), _pad_lanes(la_b), _head_rows(la_f), _head_rows(la_b),
                      _pad_lanes(dt_f), _pad_lanes(dt_b), heads=SSD_HEADS, groups=SSD_GROUPS, per_dim=False,
                      chunk=SCAN_TL) + (xs, z)


MIX_TM = 512


def _silu(t):
    return t * jax.nn.sigmoid(t)


def _mix_out_body(mla_ref, gf_ref, gb_ref, gr_ref, sf_ref, sb_ref, sx_ref, sz_ref, rf_ref, rb_ref, rg_ref,
                  gla_g_ref, ssd_d_ref, ssd_g_ref, o_ref):
    w = mla_ref.shape[1]
    dv = w // GLA_HEADS
    avg = jnp.where(lax.broadcasted_iota(jnp.int32, (w, w), 0) // dv == lax.broadcasted_iota(jnp.int32, (w, w), 1) // dv,
                    1.0 / dv, 0.0)

    def head_mean(t):
        return jnp.dot(t, avg, precision=_HI, preferred_element_type=F32)

    o_ref[:, 0:w] = mla_ref[...].astype(o_ref.dtype)
    gla = gf_ref[...] + gb_ref[...]
    gla = gla * lax.rsqrt(head_mean(gla * gla) + 1e-6) * gla_g_ref[...] * _silu(gr_ref[...])
    o_ref[:, w:2 * w] = gla.astype(o_ref.dtype)
    ssd = (sf_ref[...] + sb_ref[...] + ssd_d_ref[...] * sx_ref[...]) * _silu(sz_ref[...])
    ssd = ssd * lax.rsqrt(jnp.mean(ssd * ssd, axis=-1, keepdims=True) + 1e-6) * ssd_g_ref[...]
    o_ref[:, 2 * w:3 * w] = ssd.astype(o_ref.dtype)
    ret = rf_ref[...] + rb_ref[...]
    ret = ret - head_mean(ret)
    ret = ret * lax.rsqrt(head_mean(ret * ret) + 1e-6) * _silu(rg_ref[...])
    o_ref[:, 3 * w:4 * w] = ret.astype(o_ref.dtype)


def mix_out(o_mla, gla, ssd, ret, gla_g, ssd_d, ssd_g):
    b, l, w = o_mla.shape
    rows = b * l
    acts = [t.reshape(rows, w) for t in (o_mla,) + tuple(gla) + tuple(ssd) + tuple(ret)]
    params = [gla_g.reshape(1, w), jnp.repeat(ssd_d, SSD_HEADDIM).reshape(1, w), ssd_g.reshape(1, w)]
    tm = math.gcd(MIX_TM, rows)
    out = pl.pallas_call(
        _mix_out_body,
        grid=(rows // tm,),
        in_specs=[pl.BlockSpec((tm, w), lambda i: (i, 0))] * len(acts)
        + [pl.BlockSpec((1, w), lambda i: (0, 0))] * len(params),
        out_specs=pl.BlockSpec((tm, 4 * w), lambda i: (i, 0)),
        out_shape=jax.ShapeDtypeStruct((rows, 4 * w), BF16),
        name='mix_out',
    )(*acts, *params)
    return out.reshape(b, l, 4 * w)


def ret_tokens(p, rope):
    q, k, v, g = p
    q = _rotate_heads(q, RET_HEADS, rope)
    k = _rotate_heads(k * RET_DK ** -0.5, RET_HEADS, rope)
    la = jnp.broadcast_to(ret_log_decay(), q.shape[:2] + (RET_HEADS,))
    return scan_bidir(q, k, k, v, _pad_lanes(la), _pad_lanes(la), _head_rows(la), _head_rows(la),
                      heads=RET_HEADS, per_dim=False, chunk=SCAN_TL) + (g,)


def mla_tokens(p, g_q, w_uq, g_kv, w_uk, w_uv, axial):
    c_q, c_kv, k_r = p
    kv = rms_norm(c_kv, g_kv)
    k_nope = to_heads(_mm(kv, w_uk), MLA_HEADS)
    v = to_heads(_mm(kv, w_uv), MLA_HEADS)
    k_rope = axial_rope(k_r, axial)
    k = jnp.concatenate([k_nope, jnp.broadcast_to(k_rope[:, None], k_nope.shape[:3] + (MLA_ROPE,))], -1)
    q = to_heads(_mm(rms_norm(c_q, g_q), w_uq * (MLA_NOPE + MLA_ROPE) ** -0.5), MLA_HEADS)
    q = jnp.concatenate([q[..., :MLA_NOPE], axial_rope(q[..., MLA_NOPE:], axial)], -1)
    return attention_tokens(q.astype(BF16), k.astype(BF16), v.astype(BF16))


ATTN_TQ = 256


def _attn_body(q_ref, k_ref, v_ref, o_ref):
    heads, l, dv = v_ref.shape[1], v_ref.shape[2], v_ref.shape[3]

    def attend(n_keys):
        for h in range(heads):
            s = lax.dot_general(q_ref[0, h], k_ref[0, h, :n_keys, :], _NT, preferred_element_type=F32)
            p = jnp.exp(s - jnp.max(s, axis=1, keepdims=True))
            o = jnp.dot(p.astype(BF16), v_ref[0, h, :n_keys, :], preferred_element_type=F32)
            o_ref[0, :, h * dv:(h + 1) * dv] = o / jnp.sum(p, axis=1, keepdims=True)

    is_ctx = pl.program_id(1) < CTX_LEN // ATTN_TQ
    pl.when(is_ctx)(lambda: attend(CTX_LEN))
    pl.when(jnp.logical_not(is_ctx))(lambda: attend(l))


def attention_tokens(q, k, v):
    b, h, l, d = q.shape
    dv = v.shape[-1]
    return pl.pallas_call(
        _attn_body,
        grid=(b, l // ATTN_TQ),
        in_specs=[pl.BlockSpec((1, h, ATTN_TQ, d), lambda bi, qi: (bi, 0, qi, 0)),
                  pl.BlockSpec((1, h, l, d), lambda bi, qi: (bi, 0, 0, 0)),
                  pl.BlockSpec((1, h, l, dv), lambda bi, qi: (bi, 0, 0, 0))],
        out_specs=pl.BlockSpec((1, ATTN_TQ, h * dv), lambda bi, qi: (bi, qi, 0)),
        out_shape=jax.ShapeDtypeStruct((b, l, h * dv), F32),
        compiler_params=pltpu.CompilerParams(vmem_limit_bytes=VMEM_LIMIT),
        name='mla_attention',
    )(q, k, v)


RES_TM = CTX_LEN


def _residual_norm_body(x_ref, y_ref, gate_ref, g_ref, b_ref, *rest):
    v = DEEPNORM_ALPHA * x_ref[0] + gate_ref[0, 0] * y_ref[0]
    cen = v - jnp.mean(v, axis=-1, keepdims=True)
    r = cen * lax.rsqrt(jnp.mean(cen * cen, axis=-1, keepdims=True) + 1e-5) * g_ref[...] + b_ref[...]
    if len(rest) == 1:
        rest[0][0] = r
    else:
        shift_ref, scale_ref, r_ref, m_ref = rest
        r_ref[0] = r
        m_ref[0] = (r * (1 + scale_ref[0, 0]) + shift_ref[0, 0]).astype(BF16)


def residual_norm(x, y, gate, g, b, shift=None, scale=None):
    bsz, lo, d = y.shape
    off = (x.shape[1] - lo) // RES_TM
    modulated = shift is not None
    tok = lambda o: pl.BlockSpec((1, RES_TM, d), lambda bi, t: (bi, t + o, 0))
    vspec = pl.BlockSpec((1, 1, 1, d), lambda bi, t: (bi, jnp.minimum(t + off, 1), 0, 0))
    pspec = pl.BlockSpec((1, d), lambda bi, t: (0, 0))
    out_spec = pl.BlockSpec((1, RES_TM, d), lambda bi, t: (bi, t, 0))
    r_shape = jax.ShapeDtypeStruct((bsz, lo, d), F32)
    return pl.pallas_call(
        _residual_norm_body,
        grid=(bsz, lo // RES_TM),
        in_specs=[tok(off), tok(0), vspec, pspec, pspec] + ([vspec, vspec] if modulated else []),
        out_specs=[out_spec, out_spec] if modulated else out_spec,
        out_shape=[r_shape, jax.ShapeDtypeStruct((bsz, lo, d), BF16)] if modulated else r_shape,
        name='residual_norm',
    )(x, y, gate, g.reshape(1, d), b.reshape(1, d), *((shift, scale) if modulated else ()))


def kernel(x, c, ctx, c_ctx, w_ada, b_ada, w_in, mla_g_q, mla_w_uq, mla_g_kv, mla_w_uk, mla_w_uv, gla_w_gf, gla_b_gf, gla_w_gb, gla_b_gb, gla_g, ssd_conv_w, ssd_conv_b, ssd_dt_bias_f, ssd_dt_bias_b, ssd_a_log_f, ssd_a_log_b, ssd_d, ssd_g, w_out, ln1_g, ln1_b, peer_w_q, peer_keys, peer_u, peer_v, ln2_g, ln2_b):
    seq = x.shape[1]
    rows = seq // GRID_W
    row = jnp.repeat(jnp.arange(rows), GRID_W)
    col = jnp.tile(jnp.arange(GRID_W), rows)
    axial = (_ctx_identity(rope_tables(row, MLA_ROPE // 2)), _ctx_identity(rope_tables(col, MLA_ROPE // 2)))
    ret_rope = _ctx_identity(rope_tables(jnp.arange(seq), RET_DK))
    s_lat = jax.nn.silu(c)
    s_ctx = jax.nn.silu(c_ctx)
    bsz = x.shape[0]

    def ada(i):
        lat = s_lat @ w_ada[i] + b_ada[i]
        ctx_v = jnp.broadcast_to(s_ctx @ w_ada[i] + b_ada[i], lat.shape)
        return jnp.stack([ctx_v, lat], 1).reshape(bsz, 2, 6, 1, D_MODEL)

    vecs = [ada(i) for i in range(DEPTH)]
    xa = jnp.concatenate([ctx, x], 1)
    sel = jnp.concatenate([jnp.zeros((CTX_LEN,), jnp.int32), jnp.ones((seq,), jnp.int32)])[None, :, None]
    v0 = vecs[0]
    xin = modulate(xa, jnp.where(sel == 0, v0[:, 0, 0], v0[:, 1, 0]), jnp.where(sel == 0, v0[:, 0, 1], v0[:, 1, 1]))
    for i in range(DEPTH):
        last = i == DEPTH - 1
        vec = vecs[i]
        p_mla, p_gla, p_ssd, p_ret = partition(split_cols(_mm(xin, w_in[i]), IN_SIZES))
        mixed = mix_out(
            mla_tokens(p_mla, mla_g_q[i], mla_w_uq[i], mla_g_kv[i], mla_w_uk[i], mla_w_uv[i], axial),
            gla_tokens(p_gla, gla_w_gf[i], gla_b_gf[i], gla_w_gb[i], gla_b_gb[i]),
            ssd_tokens(p_ssd, ssd_conv_w[i], ssd_conv_b[i], ssd_dt_bias_f[i], ssd_dt_bias_b[i],
                       ssd_a_log_f[i], ssd_a_log_b[i]),
            ret_tokens(p_ret, ret_rope), gla_g[i], ssd_d[i], ssd_g[i])
        y_all = _mm(mixed, w_out[i])
        x1, xm = residual_norm(xa, y_all, vec[:, :, 2], ln1_g[i], ln1_b[i], vec[:, :, 3], vec[:, :, 4])
        toks = xm[:, CTX_LEN:] if last else xm
        f = peer_dense(toks.reshape(-1, D_MODEL), peer_w_q[i].T.astype(BF16), peer_keys[i, 0].astype(BF16),
                       peer_keys[i, 1].astype(BF16), peer_u[i].astype(BF16), peer_v[i].T.astype(BF16))
        f = f.reshape(bsz, -1, D_MODEL)
        if last:
            return residual_norm(x1, f, vec[:, :, 5], ln2_g[i], ln2_b[i])
        nxt = vecs[i + 1]
        xa, xin = residual_norm(x1, f, vec[:, :, 5], ln2_g[i], ln2_b[i], nxt[:, :, 0], nxt[:, :, 1])
```

```python
import functools
import math

import jax
import jax.numpy as jnp
import numpy as np
from jax import lax
from jax.experimental import pallas as pl
from jax.experimental.pallas import tpu as pltpu

D_MODEL = 1024
DEPTH = 4
GRID_W = 64
CHUNK = 64
ROPE_BASE = 10000.0
MLA_HEADS, MLA_NOPE, MLA_ROPE, MLA_V, MLA_Q_LORA, MLA_KV_LORA = 4, 64, 32, 64, 256, 128
GLA_HEADS, GLA_DK, GLA_DV, GLA_GATE_RANK, GLA_GATE_NORM = 4, 32, 64, 16, 16.0
SSD_HEADS, SSD_HEADDIM, SSD_GROUPS, SSD_STATE = 4, 64, 2, 128
RET_HEADS, RET_DK, RET_DV = 4, 32, 64
PEER_HEADS, PEER_NKEYS, PEER_DKEY, PEER_TOPK = 8, 128, 256, 16
DEEPNORM_ALPHA = (2 * DEPTH) ** 0.25

SSD_BC = SSD_GROUPS * SSD_STATE
SSD_XBC = SSD_HEADS * SSD_HEADDIM + 2 * SSD_BC
MLA_SIZES = (MLA_Q_LORA, MLA_KV_LORA, MLA_ROPE)
GLA_SIZES = (GLA_HEADS * GLA_DK, GLA_HEADS * GLA_DK, GLA_HEADS * GLA_DV, GLA_HEADS * GLA_DV, GLA_GATE_RANK, GLA_GATE_RANK)
SSD_SIZES = (SSD_HEADS * SSD_HEADDIM, SSD_XBC, SSD_HEADS, SSD_HEADS)
RET_SIZES = (RET_HEADS * RET_DK, RET_HEADS * RET_DK, RET_HEADS * RET_DV, RET_HEADS * RET_DV)
IN_SIZES = MLA_SIZES + GLA_SIZES + SSD_SIZES + RET_SIZES

F32 = jnp.float32
BF16 = jnp.bfloat16
LANE = 128
VMEM_LIMIT = 48 * 2 ** 20


def _mm_body(a_ref, b_ref, o_ref):
    o_ref[...] = jnp.dot(a_ref[...].astype(BF16), b_ref[...].astype(BF16), preferred_element_type=F32)


def _mm(a, b, tm=1024, tn=1024):
    lead = a.shape[:-1]
    k = a.shape[-1]
    n = b.shape[-1]
    a2 = a.reshape(-1, k)
    m = a2.shape[0]
    tm = math.gcd(tm, m)
    n_pad = -(-n // LANE) * LANE
    tn = max(t for t in range(LANE, tn + 1, LANE) if n_pad % t == 0)
    b = b.astype(BF16)
    if n_pad != n:
        b = jnp.pad(b, ((0, 0), (0, n_pad - n)))
    out = pl.pallas_call(
        _mm_body,
        grid=(m // tm, n_pad // tn),
        in_specs=[pl.BlockSpec((tm, k), lambda i, j: (i, 0)), pl.BlockSpec((k, tn), lambda i, j: (0, j))],
        out_specs=pl.BlockSpec((tm, tn), lambda i, j: (i, j)),
        out_shape=jax.ShapeDtypeStruct((m, n_pad), F32),
        compiler_params=pltpu.CompilerParams(vmem_limit_bytes=VMEM_LIMIT),
        name='matmul',
    )(a2, b)
    return out[:, :n].reshape(lead + (n,))


def rms_norm(x, g, eps=1e-6):
    return x * lax.rsqrt(jnp.mean(x * x, -1, keepdims=True) + eps) * g


def modulate(x, shift, scale):
    return x * (1 + scale) + shift


def split_cols(p, sizes):
    return jnp.split(p, np.cumsum(sizes)[:-1].tolist(), axis=-1)


def partition(pieces):
    a = len(MLA_SIZES)
    b = a + len(GLA_SIZES)
    s = b + len(SSD_SIZES)
    return pieces[:a], pieces[a:b], pieces[b:s], pieces[s:]


def to_heads(t, h):
    b, l, _ = t.shape
    return t.reshape(b, l, h, -1).transpose(0, 2, 1, 3)


def rope_tables(pos, dim):
    half = dim // 2
    freqs = ROPE_BASE ** (-jnp.arange(half, dtype=F32) / half)
    ang = pos.astype(F32)[:, None] * freqs
    return jnp.cos(ang), jnp.sin(ang)


def rotate(x, tab):
    cos, sin = tab
    h = x.shape[-1] // 2
    x1, x2 = x[..., :h], x[..., h:]
    return jnp.concatenate([x1 * cos - x2 * sin, x1 * sin + x2 * cos], -1)


def axial_rope(x, axial):
    h = x.shape[-1] // 2
    return jnp.concatenate([rotate(x[..., :h], axial[0]), rotate(x[..., h:], axial[1])], -1)


def ret_log_decay():
    return jnp.log1p(-jnp.exp2(-5.0 - jnp.arange(RET_HEADS, dtype=F32)))


PEER_TT = 512
PEER_EB = 512
SUBLANE = 8
_NT = (((1,), (1,)), ((), ()))
NEG_INF = float('-inf')
PEER_NCAND = sum(PEER_TOPK // (b + 1) for b in range(PEER_TOPK))


def _merge_exchanges(lo, hi, r):
    step = r * 2
    if step < hi - lo:
        yield from _merge_exchanges(lo, hi, step)
        yield from _merge_exchanges(lo + r, hi, step)
        yield from ((i, i + r) for i in range(lo + r, hi - r, step))
    else:
        yield (lo, lo + r)


def _sort_exchanges(lo, hi):
    if hi > lo:
        mid = lo + (hi - lo) // 2
        yield from _sort_exchanges(lo, mid)
        yield from _sort_exchanges(mid + 1, hi)
        yield from _merge_exchanges(lo, hi, 1)


def _top_values(s, rows_ref):
    n = s.shape[0] // SUBLANE
    v = [s[t * SUBLANE:(t + 1) * SUBLANE, :] for t in range(n)]
    for i, j in _sort_exchanges(0, n - 1):
        v[i], v[j] = jnp.maximum(v[i], v[j]), jnp.minimum(v[i], v[j])
    for k in range(PEER_TOPK):
        m = jnp.max(v[0], axis=0, keepdims=True)
        rows_ref[k:k + 1, :] = m
        pop = v[0] == m
        live = min(n, PEER_TOPK - k) - 1
        for r in range(live):
            v[r] = jnp.where(pop, v[r + 1] if r + 1 < n else NEG_INF, v[r])


def _peer_score_body(x_ref, wqt_ref, k0_ref, k1_ref, s1_ref, e1_ref, s2_ref, e2_ref,
                     qt_scr, v1_scr, v2_scr, cand_scr):
    half = PEER_DKEY // 2
    qt_scr[...] = lax.dot_general(wqt_ref[...], x_ref[...], _NT, preferred_element_type=F32)

    def strip(h, tg):
        s1 = s1_ref[h, tg]
        s2 = s2_ref[h, tg]
        _top_values(s1, v1_scr)
        _top_values(s2, v2_scr)
        r = 0
        for b in range(PEER_TOPK):
            for a in range(PEER_TOPK // (b + 1)):
                cand_scr[r:r + 1, :] = v1_scr[a:a + 1, :] + v2_scr[b:b + 1, :]
                r += 1
        cand_scr[r:, :] = jnp.full((cand_scr.shape[0] - r, LANE), NEG_INF, F32)
        cand = [cand_scr[t0:t0 + SUBLANE, :] for t0 in range(0, cand_scr.shape[0], SUBLANE)]
        prev = None
        cnt = jnp.zeros((1, LANE), F32)
        tau = jnp.full((1, LANE), NEG_INF, F32)
        for _ in range(PEER_TOPK):
            best = None
            for c in cand:
                cur = c if prev is None else jnp.where(c < prev, c, NEG_INF)
                best = cur if best is None else jnp.maximum(best, cur)
            m = jnp.max(best, axis=0, keepdims=True)
            hits = None
            for c in cand:
                eq = jnp.where(c == m, 1.0, 0.0)
                hits = eq if hits is None else hits + eq
            tau = jnp.where(cnt < PEER_TOPK, m, tau)
            cnt = cnt + jnp.sum(hits, axis=0, keepdims=True)
            prev = m
        m1 = v1_scr[0:1, :]
        m2 = v2_scr[0:1, :]
        zs = None
        for c in cand:
            z = jnp.where(c >= tau, jnp.exp(c - (m1 + m2)), 0.0)
            zs = z if zs is None else zs + z
        z_tot = jnp.sum(zs, axis=0, keepdims=True)
        e1_ref[h, tg] = jnp.exp(s1 - m1) / (2.0 * z_tot)
        e2_ref[h, tg] = jnp.exp(s2 - m2)
        c1 = jnp.full(s1.shape, jnp.inf, F32)
        for b in range(PEER_TOPK):
            v2b = v2_scr[b:b + 1, :]
            c1 = jnp.where(s1 + v2b >= tau, v2b, c1)
        s1_ref[h, tg] = c1

    def head(h, carry):
        base = pl.multiple_of(h * PEER_DKEY, PEER_DKEY)
        s1 = jnp.dot(k0_ref[...], qt_scr[pl.ds(base, half), :].astype(BF16), preferred_element_type=F32)
        s2 = jnp.dot(k1_ref[...], qt_scr[pl.ds(base + half, half), :].astype(BF16), preferred_element_type=F32)
        for tg in range(x_ref.shape[0] // LANE):
            s1_ref[h, tg] = s1[:, tg * LANE:(tg + 1) * LANE]
            s2_ref[h, tg] = s2[:, tg * LANE:(tg + 1) * LANE]
        for tg in range(x_ref.shape[0] // LANE):
            strip(h, tg)
        return carry

    lax.fori_loop(0, PEER_HEADS, head, 0)


PEER_JQ = 4
PEER_IQ = 4


def _peer_expert_body(x_ref, u_ref, vt_ref, c1_ref, e1_ref, s2_ref, e2_ref, o_ref,
                      h_scr, g_scr, act_scr, acc_scr, bc1_scr, be1_scr):
    e = pl.program_id(1)
    n_blocks = pl.num_programs(1) - 1
    tt = x_ref.shape[0]
    n_i = u_ref.shape[0] // PEER_NKEYS
    span = PEER_JQ * SUBLANE

    @pl.when(e == 0)
    def _():
        acc_scr[...] = jnp.zeros_like(acc_scr)
        act_scr[...] = jnp.zeros_like(act_scr)

    n_tg = tt // LANE
    m_rows = acc_scr.shape[0] // n_tg
    i_base = jnp.minimum(e, n_blocks - 1) * n_i
    for i_loc in range(n_i):
        for tg in range(n_tg):
            for h in range(PEER_HEADS):
                bc1_scr[i_loc, tg, h] = jnp.broadcast_to(c1_ref[h, tg, pl.ds(i_base + i_loc, 1), :], (SUBLANE, LANE))
                be1_scr[i_loc, tg, h] = jnp.broadcast_to(e1_ref[h, tg, pl.ds(i_base + i_loc, 1), :], (SUBLANE, LANE))
    h_scr[...] = lax.dot_general(u_ref[...], x_ref[...], _NT, preferred_element_type=F32)

    def token_strip(tg, carry):
        m0 = pl.multiple_of(tg * m_rows, m_rows)
        acc_scr[pl.ds(m0, m_rows), :] += jnp.dot(vt_ref[pl.ds(m0, m_rows), :], act_scr[...],
                                                  preferred_element_type=F32)
        for i0 in range(0, n_i, PEER_IQ):
            for j0 in range(0, PEER_NKEYS, span):
                g = [[None] * PEER_JQ for _ in range(PEER_IQ)]
                for h in range(PEER_HEADS):
                    cs = [bc1_scr[i0 + ii, tg, h] for ii in range(PEER_IQ)]
                    bs = [be1_scr[i0 + ii, tg, h] for ii in range(PEER_IQ)]
                    for k in range(PEER_JQ):
                        rows = slice(j0 + k * SUBLANE, j0 + (k + 1) * SUBLANE)
                        s2t = s2_ref[h, tg, rows, :]
                        e2t = e2_ref[h, tg, rows, :]
                        for ii in range(PEER_IQ):
                            gh = jnp.where(s2t >= cs[ii], bs[ii] * e2t, 0.0)
                            g[ii][k] = gh if g[ii][k] is None else g[ii][k] + gh
                for ii in range(PEER_IQ):
                    for k in range(PEER_JQ):
                        r0 = (i0 + ii) * PEER_NKEYS + j0 + k * SUBLANE
                        g_scr[tg, r0:r0 + SUBLANE, :] = g[ii][k]
        return carry

    lax.fori_loop(0, n_tg, token_strip, 0)
    for tg in range(n_tg):
        hv = h_scr[:, tg * LANE:(tg + 1) * LANE]
        act_scr[:, tg * LANE:(tg + 1) * LANE] = (
            hv * (1.0 + lax.erf(hv * (2.0 ** -0.5))) * g_scr[tg]).astype(BF16)

    @pl.when(e == n_blocks)
    def _():
        o_ref[...] = acc_scr[...].T


def peer_dense(xm, wqt, k0, k1, u_bf, vt_bf, tt=PEER_TT, eb=PEER_EB):
    t, d = xm.shape
    n_blocks = u_bf.shape[0] // eb
    xb = xm.astype(BF16)
    hk = (PEER_HEADS, t // LANE, PEER_NKEYS, LANE)
    sblk = pl.BlockSpec((PEER_HEADS, tt // LANE, PEER_NKEYS, LANE), lambda i: (0, i, 0, 0))
    c1, e1, s2, e2 = pl.pallas_call(
        _peer_score_body,
        grid=(t // tt,),
        in_specs=[pl.BlockSpec((tt, d), lambda i: (i, 0)),
                  pl.BlockSpec(wqt.shape, lambda i: (0, 0)),
                  pl.BlockSpec(k0.shape, lambda i: (0, 0)),
                  pl.BlockSpec(k1.shape, lambda i: (0, 0))],
        out_specs=[sblk, sblk, sblk, sblk],
        out_shape=[jax.ShapeDtypeStruct(hk, F32)] * 4,
        scratch_shapes=[pltpu.VMEM((PEER_HEADS * PEER_DKEY, tt), F32),
                        pltpu.VMEM((PEER_TOPK, LANE), F32), pltpu.VMEM((PEER_TOPK, LANE), F32),
                        pltpu.VMEM((-(-PEER_NCAND // SUBLANE) * SUBLANE, LANE), F32)],
        compiler_params=pltpu.CompilerParams(vmem_limit_bytes=VMEM_LIMIT),
        name='peer_score',
    )(xb, wqt, k0, k1)
    sblk2 = pl.BlockSpec((PEER_HEADS, tt // LANE, PEER_NKEYS, LANE), lambda i, j: (0, i, 0, 0))
    return pl.pallas_call(
        _peer_expert_body,
        grid=(t // tt, n_blocks + 1),
        in_specs=[pl.BlockSpec((tt, d), lambda i, j: (i, 0)),
                  pl.BlockSpec((eb, d), lambda i, j: (jnp.minimum(j, n_blocks - 1), 0)),
                  pl.BlockSpec((d, eb), lambda i, j: (0, jnp.maximum(j - 1, 0))),
                  sblk2, sblk2, sblk2, sblk2],
        out_specs=pl.BlockSpec((tt, d), lambda i, j: (i, 0)),
        out_shape=jax.ShapeDtypeStruct((t, d), F32),
        scratch_shapes=[pltpu.VMEM((eb, tt), F32), pltpu.VMEM((tt // LANE, eb, LANE), F32), pltpu.VMEM((eb, tt), BF16),
                        pltpu.VMEM((d, tt), F32)]
        + [pltpu.VMEM((eb // PEER_NKEYS, tt // LANE, PEER_HEADS, SUBLANE, LANE), F32)] * 2,
        compiler_params=pltpu.CompilerParams(vmem_limit_bytes=VMEM_LIMIT,
                                             dimension_semantics=("arbitrary", "arbitrary")),
        name='peer_expert',
    )(xb, u_bf, vt_bf, c1, e1, s2, e2)


SCAN_TL = 256
CTX_LEN = 256
_HI = lax.Precision.HIGHEST
_TN = (((0,), (0,)), ((), ()))


def _scan_body(*refs, chunk, heads, groups, per_dim, scaled):
    n_in = 6 if scaled else 5
    of_ref, ob_ref, sf_scr, sb_scr = refs[2 * n_in:]
    tl, dk = refs[0].shape[1], refs[0].shape[2] // groups
    vd = refs[2].shape[2]
    kd, dv = heads * dk, vd // heads

    @pl.when(pl.program_id(1) == 0)
    def _():
        sf_scr[...] = jnp.zeros_like(sf_scr)
        sb_scr[...] = jnp.zeros_like(sb_scr)

    ri = lax.broadcasted_iota(jnp.int32, (chunk, chunk), 0)
    ci = lax.broadcasted_iota(jnp.int32, (chunk, chunk), 1)
    khead = lax.broadcasted_iota(jnp.int32, (1, kd), 1) // dk
    vhead = lax.broadcasted_iota(jnp.int32, (1, vd), 1) // dv
    same_head = (lax.broadcasted_iota(jnp.int32, (vd, kd), 0) // dv
                 == lax.broadcasted_iota(jnp.int32, (vd, kd), 1) // dk)

    def per_key(x8):
        out = jnp.zeros((x8.shape[0], kd), F32)
        for h in range(heads):
            out = jnp.where(khead == h, x8[:, h:h + 1], out)
        return out

    def per_head(t):
        if groups == heads:
            return t
        rep = heads // groups
        return jnp.concatenate([t[:, (h // rep) * dk:(h // rep + 1) * dk] for h in range(heads)], axis=1)

    def sweep(in_refs, o_ref, s_scr, fwd):
        q_ref, k_ref, v_ref, lg_ref, lt_ref = in_refs[:5]
        keep = (ri >= ci) if fwd else (ri <= ci)
        tri = keep.astype(F32)
        keep_all = jnp.concatenate([keep] * heads, axis=0)
        order = range(tl // chunk) if fwd else reversed(range(tl // chunk))
        s = s_scr[...]
        for c in order:
            rows = slice(c * chunk, (c + 1) * chunk)
            q, k, v = per_head(q_ref[0, rows, :]), per_head(k_ref[0, rows, :]), v_ref[0, rows, :]
            if scaled:
                k = k * per_key(in_refs[5][0, rows, :])
            cum = jnp.dot(tri, lg_ref[0, rows, :], precision=_HI, preferred_element_type=F32)
            tot = cum[chunk - 1:chunk, :] if fwd else cum[0:1, :]
            if per_dim:
                e_in, e_out, e_tot = jnp.exp(cum), jnp.exp(tot - cum), jnp.exp(tot)
                mid = cum[chunk // 2:chunk // 2 + 1, :]
                q_att = (q * jnp.exp(cum - mid))
                k_att = (k * jnp.exp(mid - cum)).astype(BF16)
            else:
                e_in, e_out, e_tot = per_key(jnp.exp(cum)), per_key(jnp.exp(tot - cum)), per_key(jnp.exp(tot))
                cum_t = lax.dot_general(lt_ref[0, :, rows], tri, _NT, precision=_HI, preferred_element_type=F32)
                q_att, k_att = q, k.astype(BF16)
            o = lax.dot_general((q * e_in).astype(BF16), s.astype(BF16), _NT, preferred_element_type=F32)
            q_stack = jnp.concatenate([jnp.where(khead == h, q_att, 0.0) for h in range(heads)], axis=0)
            sc = lax.dot_general(q_stack.astype(BF16), k_att, _NT, preferred_element_type=F32)
            if per_dim:
                att = jnp.where(keep_all, sc, 0.0)
            else:
                att = jnp.concatenate(
                    [sc[h * chunk:(h + 1) * chunk]
                     * jnp.exp(jnp.where(keep, cum[:, h:h + 1] - cum_t[h:h + 1, :], NEG_INF)) for h in range(heads)],
                    axis=0)
            pv = jnp.dot(att.astype(BF16), v.astype(BF16), preferred_element_type=F32)
            for h in range(heads):
                o = o + jnp.where(vhead == h, pv[h * chunk:(h + 1) * chunk], 0.0)
            o_ref[0, rows, :] = o
            kv = lax.dot_general(v.astype(BF16), (k * e_out).astype(BF16), _TN, preferred_element_type=F32)
            s = s * e_tot + jnp.where(same_head, kv, 0.0)
        s_scr[...] = s

    def sweep_blocked(in_refs, o_ref, s_scr, fwd):
        q_ref, k_ref, v_ref, lg_ref = in_refs[:4]
        n = tl // chunk
        rr = lax.broadcasted_iota(jnp.int32, (tl, tl), 0)
        cc = lax.broadcasted_iota(jnp.int32, (tl, tl), 1)
        keep = (rr // chunk == cc // chunk) & ((rr >= cc) if fwd else (rr <= cc))
        q, k, v = q_ref[0], k_ref[0], v_ref[0]
        cum = jnp.dot(keep.astype(F32), lg_ref[0], precision=_HI, preferred_element_type=F32)
        end = chunk - 1 if fwd else 0

        def chunk_rows(idx):
            return jnp.concatenate([jnp.broadcast_to(cum[c * chunk + idx:c * chunk + idx + 1, :], (chunk, kd))
                                    for c in range(n)], axis=0)

        tot, mid = chunk_rows(end), chunk_rows(chunk // 2)
        q_in = (q * jnp.exp(cum)).astype(BF16)
        k_out = (k * jnp.exp(tot - cum)).astype(BF16)
        q_att = q * jnp.exp(cum - mid)
        k_att = (k * jnp.exp(mid - cum)).astype(BF16)
        q_stack = jnp.concatenate([jnp.where(khead == h, q_att, 0.0) for h in range(heads)], axis=0)
        sc = lax.dot_general(q_stack.astype(BF16), k_att, _NT, preferred_element_type=F32)
        att = jnp.where(jnp.concatenate([keep] * heads, axis=0), sc, 0.0)
        pv = jnp.dot(att.astype(BF16), v.astype(BF16), preferred_element_type=F32)
        o_intra = jnp.where(vhead == 0, pv[0:tl], 0.0)
        for h in range(1, heads):
            o_intra = o_intra + jnp.where(vhead == h, pv[h * tl:(h + 1) * tl], 0.0)
        s = s_scr[...]
        for c in (range(n) if fwd else reversed(range(n))):
            rows = slice(c * chunk, (c + 1) * chunk)
            o_ref[0, rows, :] = o_intra[rows] + lax.dot_general(q_in[rows], s.astype(BF16), _NT,
                                                                preferred_element_type=F32)
            kv = lax.dot_general(v[rows].astype(BF16), k_out[rows], _TN, preferred_element_type=F32)
            s = s * jnp.exp(cum[c * chunk + end:c * chunk + end + 1, :]) + jnp.where(same_head, kv, 0.0)
        s_scr[...] = s

    run = sweep_blocked if per_dim and groups == heads and not scaled else sweep
    run(refs[:n_in], of_ref, sf_scr, True)
    run(refs[n_in:2 * n_in], ob_ref, sb_scr, False)


def scan_bidir(q, k_f, k_b, v, lg_f, lg_b, lt_f, lt_b, ks_f=None, ks_b=None, *, heads, per_dim, chunk, groups=None):
    b, l, qw = q.shape
    groups = groups or heads
    kd = qw // groups * heads
    vd = v.shape[-1]
    scaled = ks_f is not None
    n, n_ctx = l // SCAN_TL, CTX_LEN // SCAN_TL

    def bwd_group(s):
        return jnp.where(s < n_ctx, n_ctx - 1 - s, n - 1 - (s - n_ctx))

    def specs(group):
        tok = lambda w: pl.BlockSpec((1, SCAN_TL, w), lambda bi, s: (bi, group(s), 0))
        return [tok(qw), tok(qw), tok(vd), tok(LANE),
                pl.BlockSpec((1, SUBLANE, SCAN_TL), lambda bi, s: (bi, 0, group(s)))] + ([tok(LANE)] if scaled else [])

    fwd_group = lambda s: s
    o_f, o_b = pl.pallas_call(
        functools.partial(_scan_body, chunk=chunk, heads=heads, groups=groups, per_dim=per_dim, scaled=scaled),
        grid=(b, n),
        in_specs=specs(fwd_group) + specs(bwd_group),
        out_specs=[pl.BlockSpec((1, SCAN_TL, vd), lambda bi, s: (bi, s, 0)),
                   pl.BlockSpec((1, SCAN_TL, vd), lambda bi, s: (bi, bwd_group(s), 0))],
        out_shape=[jax.ShapeDtypeStruct((b, l, vd), F32)] * 2,
        scratch_shapes=[pltpu.VMEM((vd, kd), F32)] * 2,
        compiler_params=pltpu.CompilerParams(dimension_semantics=("arbitrary", "arbitrary")),
        name='scan_bidir',
    )(*((q, k_f, v, lg_f, lt_f) + ((ks_f,) if scaled else ()) + (q, k_b, v, lg_b, lt_b) + ((ks_b,) if scaled else ())))
    return o_f, o_b


def _ctx_identity(tab):
    cos, sin = tab
    pad = (CTX_LEN, cos.shape[1])
    return jnp.concatenate([jnp.ones(pad, F32), cos], 0), jnp.concatenate([jnp.zeros(pad, F32), sin], 0)


def _pad_lanes(t):
    return jnp.pad(t, ((0, 0), (0, 0), (0, LANE - t.shape[-1])))


def _head_rows(t):
    return jnp.pad(t.transpose(0, 2, 1), ((0, 0), (0, SUBLANE - t.shape[-1]), (0, 0)))


def _split_heads(t, h):
    return t.reshape(t.shape[:-1] + (h, t.shape[-1] // h))


def _rotate_heads(t, heads, tab):
    cos, sin = tab
    return rotate(_split_heads(t, heads), (cos[:, None, :], sin[:, None, :])).reshape(t.shape)


CONV_ROWS = 64


def _conv_silu_body(x_ref, w_ref, b_ref, o_ref, xp_scr):
    l, c = x_ref.shape[1], x_ref.shape[2]
    zeros = jnp.zeros((SUBLANE, c), F32)
    xp_scr[0:SUBLANE, :] = zeros
    xp_scr[SUBLANE:SUBLANE + l, :] = x_ref[0]
    xp_scr[SUBLANE + l:2 * SUBLANE + l, :] = zeros
    w0, w1, w2, bias = w_ref[0:1, :], w_ref[1:2, :], w_ref[2:3, :], b_ref[...]
    for r0 in range(0, l, CONV_ROWS):
        row = r0 + lax.broadcasted_iota(jnp.int32, (CONV_ROWS, 1), 0)
        prev = xp_scr[SUBLANE - 1 + r0:SUBLANE - 1 + r0 + CONV_ROWS, :]
        nxt = xp_scr[SUBLANE + 1 + r0:SUBLANE + 1 + r0 + CONV_ROWS, :]
        if r0 <= CTX_LEN < r0 + CONV_ROWS:
            prev = jnp.where(row == CTX_LEN, 0.0, prev)
        if r0 <= CTX_LEN - 1 < r0 + CONV_ROWS:
            nxt = jnp.where(row == CTX_LEN - 1, 0.0, nxt)
        y = prev * w0 + xp_scr[SUBLANE + r0:SUBLANE + r0 + CONV_ROWS, :] * w1 + nxt * w2 + bias
        o_ref[0, r0:r0 + CONV_ROWS, :] = y * jax.nn.sigmoid(y)


def _segment_conv_silu(x, w, b):
    bsz, l, c = x.shape
    return pl.pallas_call(
        _conv_silu_body,
        grid=(bsz,),
        in_specs=[pl.BlockSpec((1, l, c), lambda i: (i, 0, 0)),
                  pl.BlockSpec(w.shape, lambda i: (0, 0)),
                  pl.BlockSpec((1, c), lambda i: (0, 0))],
        out_specs=pl.BlockSpec((1, l, c), lambda i: (i, 0, 0)),
        out_shape=jax.ShapeDtypeStruct(x.shape, F32),
        scratch_shapes=[pltpu.VMEM((l + 2 * SUBLANE, c), F32)],
        compiler_params=pltpu.CompilerParams(vmem_limit_bytes=VMEM_LIMIT),
        name='conv_silu',
    )(x, w, b.reshape(1, c))


def gla_tokens(p, w_gf, b_gf, w_gb, b_gb):
    q, k, v, r, lr_f, lr_b = p
    no_rows = jnp.zeros((q.shape[0], SUBLANE, q.shape[1]), F32)

    def log_gate(lr, w, b):
        return jax.nn.log_sigmoid(_mm(lr, w) + b) / GLA_GATE_NORM

    return scan_bidir(q * GLA_DK ** -0.5, k, k, v, log_gate(lr_f, w_gf, b_gf), log_gate(lr_b, w_gb, b_gb),
                      no_rows, no_rows, heads=GLA_HEADS, per_dim=True, chunk=CHUNK) + (r,)


def ssd_tokens(p, conv_w, conv_b, dtb_f, dtb_b, alog_f, alog_b):
    z, xbc, dt_f, dt_b = p
    xbc = _segment_conv_silu(xbc, conv_w, conv_b)
    xs, bm, cm = split_cols(xbc, (SSD_HEADS * SSD_HEADDIM, SSD_BC, SSD_BC))

    def direction(dt_raw, dt_bias, a_log):
        dt = jax.nn.softplus(dt_raw + dt_bias)
        return dt, dt * -jnp.exp(a_log)

    dt_f, la_f = direction(dt_f, dtb_f, alog_f)
    dt_b, la_b = direction(dt_b, dtb_b, alog_b)
    return scan_bidir(cm, bm, bm, xs, _pad_lanes(la_f), _pad_lanes(la_b), _head_rows(la_f), _head_rows(la_b),
                      _pad_lanes(dt_f), _pad_lanes(dt_b), heads=SSD_HEADS, groups=SSD_GROUPS, per_dim=False,
                      chunk=SCAN_TL) + (xs, z)


MIX_TM = 512


def _silu(t):
    return t * jax.nn.sigmoid(t)


def _mix_out_body(mla_ref, gf_ref, gb_ref, gr_ref, sf_ref, sb_ref, sx_ref, sz_ref, rf_ref, rb_ref, rg_ref,
                  gla_g_ref, ssd_d_ref, ssd_g_ref, o_ref):
    w = mla_ref.shape[1]
    dv = w // GLA_HEADS
    avg = jnp.where(lax.broadcasted_iota(jnp.int32, (w, w), 0) // dv == lax.broadcasted_iota(jnp.int32, (w, w), 1) // dv,
                    1.0 / dv, 0.0)

    def head_mean(t):
        return jnp.dot(t, avg, precision=_HI, preferred_element_type=F32)

    o_ref[:, 0:w] = mla_ref[...].astype(o_ref.dtype)
    gla = gf_ref[...] + gb_ref[...]
    gla = gla * lax.rsqrt(head_mean(gla * gla) + 1e-6) * gla_g_ref[...] * _silu(gr_ref[...])
    o_ref[:, w:2 * w] = gla.astype(o_ref.dtype)
    ssd = (sf_ref[...] + sb_ref[...] + ssd_d_ref[...] * sx_ref[...]) * _silu(sz_ref[...])
    ssd = ssd * lax.rsqrt(jnp.mean(ssd * ssd, axis=-1, keepdims=True) + 1e-6) * ssd_g_ref[...]
    o_ref[:, 2 * w:3 * w] = ssd.astype(o_ref.dtype)
    ret = rf_ref[...] + rb_ref[...]
    ret = ret - head_mean(ret)
    ret = ret * lax.rsqrt(head_mean(ret * ret) + 1e-6) * _silu(rg_ref[...])
    o_ref[:, 3 * w:4 * w] = ret.astype(o_ref.dtype)


def mix_out(o_mla, gla, ssd, ret, gla_g, ssd_d, ssd_g):
    b, l, w = o_mla.shape
    rows = b * l
    acts = [t.reshape(rows, w) for t in (o_mla,) + tuple(gla) + tuple(ssd) + tuple(ret)]
    params = [gla_g.reshape(1, w), jnp.repeat(ssd_d, SSD_HEADDIM).reshape(1, w), ssd_g.reshape(1, w)]
    tm = math.gcd(MIX_TM, rows)
    out = pl.pallas_call(
        _mix_out_body,
        grid=(rows // tm,),
        in_specs=[pl.BlockSpec((tm, w), lambda i: (i, 0))] * len(acts)
        + [pl.BlockSpec((1, w), lambda i: (0, 0))] * len(params),
        out_specs=pl.BlockSpec((tm, 4 * w), lambda i: (i, 0)),
        out_shape=jax.ShapeDtypeStruct((rows, 4 * w), BF16),
        name='mix_out',
    )(*acts, *params)
    return out.reshape(b, l, 4 * w)


def ret_tokens(p, rope):
    q, k, v, g = p
    q = _rotate_heads(q, RET_HEADS, rope)
    k = _rotate_heads(k * RET_DK ** -0.5, RET_HEADS, rope)
    la = jnp.broadcast_to(ret_log_decay(), q.shape[:2] + (RET_HEADS,))
    return scan_bidir(q, k, k, v, _pad_lanes(la), _pad_lanes(la), _head_rows(la), _head_rows(la),
                      heads=RET_HEADS, per_dim=False, chunk=SCAN_TL) + (g,)


def mla_tokens(p, g_q, w_uq, g_kv, w_uk, w_uv, axial):
    c_q, c_kv, k_r = p
    kv = rms_norm(c_kv, g_kv)
    k_nope = to_heads(_mm(kv, w_uk), MLA_HEADS)
    v = to_heads(_mm(kv, w_uv), MLA_HEADS)
    k_rope = axial_rope(k_r, axial)
    k = jnp.concatenate([k_nope, jnp.broadcast_to(k_rope[:, None], k_nope.shape[:3] + (MLA_ROPE,))], -1)
    q = to_heads(_mm(rms_norm(c_q, g_q), w_uq * (MLA_NOPE + MLA_ROPE) ** -0.5), MLA_HEADS)
    q = jnp.concatenate([q[..., :MLA_NOPE], axial_rope(q[..., MLA_NOPE:], axial)], -1)
    return attention_tokens(q.astype(BF16), k.astype(BF16), v.astype(BF16))


ATTN_TQ = 256


def _attn_body(q_ref, k_ref, v_ref, o_ref):
    heads, l, dv = v_ref.shape[1], v_ref.shape[2], v_ref.shape[3] - 1

    def attend(n_keys):
        for h in range(heads):
            s = lax.dot_general(q_ref[0, h], k_ref[0, h, :n_keys, :], _NT, preferred_element_type=F32)
            p = jnp.exp(s - jnp.max(s, axis=1, keepdims=True))
            o = jnp.dot(p.astype(BF16), v_ref[0, h, :n_keys, :], preferred_element_type=F32)
            o_ref[0, :, h * dv:(h + 1) * dv] = o[:, :dv] / o[:, dv:dv + 1]

    is_ctx = pl.program_id(1) < CTX_LEN // ATTN_TQ
    pl.when(is_ctx)(lambda: attend(CTX_LEN))
    pl.when(jnp.logical_not(is_ctx))(lambda: attend(l))


def attention_tokens(q, k, v):
    b, h, l, d = q.shape
    dv = v.shape[-1]
    v = jnp.concatenate([v, jnp.ones(v.shape[:-1] + (1,), v.dtype)], -1)
    return pl.pallas_call(
        _attn_body,
        grid=(b, l // ATTN_TQ),
        in_specs=[pl.BlockSpec((1, h, ATTN_TQ, d), lambda bi, qi: (bi, 0, qi, 0)),
                  pl.BlockSpec((1, h, l, d), lambda bi, qi: (bi, 0, 0, 0)),
                  pl.BlockSpec((1, h, l, dv + 1), lambda bi, qi: (bi, 0, 0, 0))],
        out_specs=pl.BlockSpec((1, ATTN_TQ, h * dv), lambda bi, qi: (bi, qi, 0)),
        out_shape=jax.ShapeDtypeStruct((b, l, h * dv), F32),
        compiler_params=pltpu.CompilerParams(vmem_limit_bytes=VMEM_LIMIT),
        name='mla_attention',
    )(q, k, v)


RES_TM = CTX_LEN


def _residual_norm_body(x_ref, y_ref, gate_ref, g_ref, b_ref, *rest):
    v = DEEPNORM_ALPHA * x_ref[0] + gate_ref[0, 0] * y_ref[0]
    cen = v - jnp.mean(v, axis=-1, keepdims=True)
    r = cen * lax.rsqrt(jnp.mean(cen * cen, axis=-1, keepdims=True) + 1e-5) * g_ref[...] + b_ref[...]
    if len(rest) == 1:
        rest[0][0] = r
    else:
        shift_ref, scale_ref, r_ref, m_ref = rest
        r_ref[0] = r
        m_ref[0] = (r * (1 + scale_ref[0, 0]) + shift_ref[0, 0]).astype(BF16)


def residual_norm(x, y, gate, g, b, shift=None, scale=None):
    bsz, lo, d = y.shape
    off = (x.shape[1] - lo) // RES_TM
    modulated = shift is not None
    tok = lambda o: pl.BlockSpec((1, RES_TM, d), lambda bi, t: (bi, t + o, 0))
    vspec = pl.BlockSpec((1, 1, 1, d), lambda bi, t: (bi, jnp.minimum(t + off, 1), 0, 0))
    pspec = pl.BlockSpec((1, d), lambda bi, t: (0, 0))
    out_spec = pl.BlockSpec((1, RES_TM, d), lambda bi, t: (bi, t, 0))
    r_shape = jax.ShapeDtypeStruct((bsz, lo, d), F32)
    return pl.pallas_call(
        _residual_norm_body,
        grid=(bsz, lo // RES_TM),
        in_specs=[tok(off), tok(0), vspec, pspec, pspec] + ([vspec, vspec] if modulated else []),
        out_specs=[out_spec, out_spec] if modulated else out_spec,
        out_shape=[r_shape, jax.ShapeDtypeStruct((bsz, lo, d), BF16)] if modulated else r_shape,
        name='residual_norm',
    )(x, y, gate, g.reshape(1, d), b.reshape(1, d), *((shift, scale) if modulated else ()))


def kernel(x, c, ctx, c_ctx, w_ada, b_ada, w_in, mla_g_q, mla_w_uq, mla_g_kv, mla_w_uk, mla_w_uv, gla_w_gf, gla_b_gf, gla_w_gb, gla_b_gb, gla_g, ssd_conv_w, ssd_conv_b, ssd_dt_bias_f, ssd_dt_bias_b, ssd_a_log_f, ssd_a_log_b, ssd_d, ssd_g, w_out, ln1_g, ln1_b, peer_w_q, peer_keys, peer_u, peer_v, ln2_g, ln2_b):
    seq = x.shape[1]
    rows = seq // GRID_W
    row = jnp.repeat(jnp.arange(rows), GRID_W)
    col = jnp.tile(jnp.arange(GRID_W), rows)
    axial = (_ctx_identity(rope_tables(row, MLA_ROPE // 2)), _ctx_identity(rope_tables(col, MLA_ROPE // 2)))
    ret_rope = _ctx_identity(rope_tables(jnp.arange(seq), RET_DK))
    s_lat = jax.nn.silu(c)
    s_ctx = jax.nn.silu(c_ctx)
    bsz = x.shape[0]

    def ada(i):
        lat = s_lat @ w_ada[i] + b_ada[i]
        ctx_v = jnp.broadcast_to(s_ctx @ w_ada[i] + b_ada[i], lat.shape)
        return jnp.stack([ctx_v, lat], 1).reshape(bsz, 2, 6, 1, D_MODEL)

    vecs = [ada(i) for i in range(DEPTH)]
    xa = jnp.concatenate([ctx, x], 1)
    sel = jnp.concatenate([jnp.zeros((CTX_LEN,), jnp.int32), jnp.ones((seq,), jnp.int32)])[None, :, None]
    v0 = vecs[0]
    xin = modulate(xa, jnp.where(sel == 0, v0[:, 0, 0], v0[:, 1, 0]), jnp.where(sel == 0, v0[:, 0, 1], v0[:, 1, 1]))
    for i in range(DEPTH):
        last = i == DEPTH - 1
        vec = vecs[i]
        p_mla, p_gla, p_ssd, p_ret = partition(split_cols(_mm(xin, w_in[i]), IN_SIZES))
        mixed = mix_out(
            mla_tokens(p_mla, mla_g_q[i], mla_w_uq[i], mla_g_kv[i], mla_w_uk[i], mla_w_uv[i], axial),
            gla_tokens(p_gla, gla_w_gf[i], gla_b_gf[i], gla_w_gb[i], gla_b_gb[i]),
            ssd_tokens(p_ssd, ssd_conv_w[i], ssd_conv_b[i], ssd_dt_bias_f[i], ssd_dt_bias_b[i],
                       ssd_a_log_f[i], ssd_a_log_b[i]),
            ret_tokens(p_ret, ret_rope), gla_g[i], ssd_d[i], ssd_g[i])
        y_all = _mm(mixed, w_out[i])
        x1, xm = residual_norm(xa, y_all, vec[:, :, 2], ln1_g[i], ln1_b[i], vec[:, :, 3], vec[:, :, 4])
        toks = xm[:, CTX_LEN:] if last else xm
        f = peer_dense(toks.reshape(-1, D_MODEL), peer_w_q[i].T.astype(BF16), peer_keys[i, 0].astype(BF16),
                       peer_keys[i, 1].astype(BF16), peer_u[i].astype(BF16), peer_v[i].T.astype(BF16))
        f = f.reshape(bsz, -1, D_MODEL)
        if last:
            return residual_norm(x1, f, vec[:, :, 5], ln2_g[i], ln2_b[i])
        nxt = vecs[i + 1]
        xa, xin = residual_norm(x1, f, vec[:, :, 5], ln2_g[i], ln2_b[i], nxt[:, :, 0], nxt[:, :, 1])
```
